```python
import math
import jax, jax.numpy as jnp
from jax import lax
import numpy as np

D_MODEL = 1024
BATCH = 16
SEQ = 4096
DEPTH = 2

HEAD_DIM = 64
LRU_WIDTH = D_MODEL // 2
LRU_BLOCKS = LRU_WIDTH // HEAD_DIM
SB_HEADS = D_MODEL // 256
FOX_HEADS = D_MODEL // 256
SB_WIDTH = SB_HEADS * HEAD_DIM
FOX_WIDTH = FOX_HEADS * HEAD_DIM
D_MIX = LRU_WIDTH + SB_WIDTH + FOX_WIDTH
CONV_WIDTH = 4
LRU_C = 8.0
D_FF = 2816
Q_BLOCK = 128
N_SUB = 3
EPS = 1e-6

SPLIT_SIZES = (LRU_WIDTH, LRU_WIDTH,
               SB_WIDTH, SB_WIDTH, SB_WIDTH,
               FOX_WIDTH, FOX_WIDTH, FOX_WIDTH,
               FOX_HEADS)
N_IN = sum(SPLIT_SIZES)
SPLIT_POINTS = tuple(int(v) for v in np.cumsum(SPLIT_SIZES)[:-1])

kernel_name = "hybrid_macaron_rglru_stickbreak_fox"


def _rms(x):
    xf = x.astype(jnp.float32)
    y = xf * lax.rsqrt(jnp.mean(xf * xf, axis=-1, keepdims=True) + EPS)
    return y.astype(x.dtype)


def _swiglu(h, w_up, w_down):
    gu = h @ w_up
    g, u = jnp.split(gu, 2, axis=-1)
    return (jax.nn.silu(g) * u) @ w_down


def _causal_depthwise_conv(x, w, b):
    y = lax.conv_general_dilated(
        x, w[:, None, :], window_strides=(1,), padding=[(CONV_WIDTH - 1, 0)],
        dimension_numbers=("NWC", "WIO", "NWC"), feature_group_count=x.shape[-1])
    return y + b


def _lru_combine(left, right):
    a1, b1 = left
    a2, b2 = right
    return a1 * a2, a2 * b1 + b2


def _rg_lru(u, w_r, b_r, w_i, b_i, lam):
    B, S, _ = u.shape
    ub = u.reshape(B, S, LRU_BLOCKS, HEAD_DIM)
    r = jax.nn.sigmoid(jnp.einsum("bshi,hij->bshj", ub, w_r).reshape(B, S, LRU_WIDTH) + b_r)
    i = jax.nn.sigmoid(jnp.einsum("bshi,hij->bshj", ub, w_i).reshape(B, S, LRU_WIDTH) + b_i)
    r = r.astype(jnp.float32)
    log_a = -LRU_C * r * jax.nn.softplus(-lam.astype(jnp.float32))
    a = jnp.exp(log_a)
    mult = jnp.sqrt(-jnp.expm1(2.0 * log_a))
    bt = mult * (i.astype(jnp.float32) * u.astype(jnp.float32))
    _, h = lax.associative_scan(_lru_combine, (a, bt), axis=1)
    return h.astype(u.dtype)


def _to_heads(t, n_heads):
    B, S, _ = t.shape
    return t.reshape(B, S, n_heads, HEAD_DIM).transpose(0, 2, 1, 3)


def _from_blocks(out):
    nb, B, H, Q, Dh = out.shape
    return out.transpose(1, 0, 3, 2, 4).reshape(B, nb * Q, H * Dh)


def _stick_breaking_attention(q, k, v):
    S = q.shape[2]
    scale = HEAD_DIM ** -0.5
    key_pos = jnp.arange(S)

    def block(bi):
        start = bi * Q_BLOCK
        qb = lax.dynamic_slice_in_dim(q, start, Q_BLOCK, axis=2)
        z = jnp.einsum("bhqd,bhkd->bhqk", qb, k).astype(jnp.float32) * scale
        q_pos = start + jnp.arange(Q_BLOCK)
        past = key_pos[None, :] < q_pos[:, None]
        log_beta = jax.nn.log_sigmoid(z)
        log_1mb = jnp.where(past, jax.nn.log_sigmoid(-z), 0.0)
        after = lax.cumsum(log_1mb, axis=3, reverse=True) - log_1mb
        w = jnp.where(past, jnp.exp(log_beta + after), 0.0)
        return jnp.einsum("bhqk,bhkd->bhqd", w.astype(v.dtype), v)

    return _from_blocks(lax.map(block, jnp.arange(S // Q_BLOCK)))


def _forgetting_attention(q, k, v, log_f):
    S = q.shape[2]
    scale = HEAD_DIM ** -0.5
    key_pos = jnp.arange(S)
    F = jnp.cumsum(log_f, axis=-1)

    def block(bi):
        start = bi * Q_BLOCK
        qb = lax.dynamic_slice_in_dim(q, start, Q_BLOCK, axis=2)
        Fq = lax.dynamic_slice_in_dim(F, start, Q_BLOCK, axis=2)
        logits = jnp.einsum("bhqd,bhkd->bhqk", qb, k).astype(jnp.float32) * scale
        logits = logits + Fq[..., :, None] - F[..., None, :]
        q_pos = start + jnp.arange(Q_BLOCK)
        causal = key_pos[None, :] <= q_pos[:, None]
        logits = jnp.where(causal, logits, -jnp.inf)
        p = jax.nn.softmax(logits, axis=-1)
        return jnp.einsum("bhqk,bhkd->bhqd", p.astype(v.dtype), v)

    return _from_blocks(lax.map(block, jnp.arange(S // Q_BLOCK)))


def _mixer(h, w_in, b_fgate, conv_w, conv_b, w_rgate, b_rgate, w_igate, b_igate,
           lru_lambda, g_qk, g_mix_out, w_out):
    proj = h @ w_in
    (lru_x, lru_g, sb_q, sb_k, sb_v, fx_q, fx_k, fx_v, fx_f) = jnp.split(proj, SPLIT_POINTS, axis=-1)

    u = _causal_depthwise_conv(lru_x, conv_w, conv_b)
    y_lru = _rg_lru(u, w_rgate, b_rgate, w_igate, b_igate, lru_lambda) * jax.nn.gelu(lru_g)

    y_sb = _stick_breaking_attention(_to_heads(sb_q, SB_HEADS), _to_heads(sb_k, SB_HEADS),
                                     _to_heads(sb_v, SB_HEADS))

    fq = _rms(_to_heads(fx_q, FOX_HEADS)) * g_qk[0]
    fk = _rms(_to_heads(fx_k, FOX_HEADS)) * g_qk[1]
    log_f = jax.nn.log_sigmoid(fx_f.astype(jnp.float32) + b_fgate.astype(jnp.float32))
    y_fox = _forgetting_attention(fq, fk, _to_heads(fx_v, FOX_HEADS), log_f.transpose(0, 2, 1))

    y = jnp.concatenate([_rms(y_lru), _rms(y_sb), _rms(y_fox)], axis=-1) * g_mix_out
    return y @ w_out


def setup_inputs(seed: int = 0) -> dict:
    key = jax.random.key(seed)
    ks = jax.random.split(key, 20)
    f32 = jnp.float32
    nrm = lambda k, shape, s: jax.random.normal(k, shape, f32) * s
    x = jax.random.normal(ks[0], (BATCH, SEQ, D_MODEL), f32)
    c = jax.random.normal(ks[1], (BATCH, D_MODEL), f32)
    w_ada = nrm(ks[2], (DEPTH, D_MODEL, N_SUB * 3 * D_MODEL), 0.1 * D_MODEL ** -0.5)
    b_ada = nrm(ks[3], (DEPTH, N_SUB * 3 * D_MODEL), 0.01)
    g_norm = 1.0 + nrm(ks[4], (DEPTH, N_SUB, D_MODEL), 0.05)
    w_ffn_up = nrm(ks[5], (DEPTH, 2, D_MODEL, 2 * D_FF), D_MODEL ** -0.5)
    w_ffn_down = nrm(ks[6], (DEPTH, 2, D_FF, D_MODEL), D_FF ** -0.5)
    w_in = nrm(ks[7], (DEPTH, D_MODEL, N_IN), D_MODEL ** -0.5)
    b_fgate = 3.0 + nrm(ks[8], (DEPTH, FOX_HEADS), 0.1)
    conv_w = nrm(ks[9], (DEPTH, CONV_WIDTH, LRU_WIDTH), CONV_WIDTH ** -0.5)
    conv_b = nrm(ks[10], (DEPTH, LRU_WIDTH), 0.01)
    w_rgate = nrm(ks[11], (DEPTH, LRU_BLOCKS, HEAD_DIM, HEAD_DIM), HEAD_DIM ** -0.5)
    b_rgate = nrm(ks[12], (DEPTH, LRU_WIDTH), 0.01)
    w_igate = nrm(ks[13], (DEPTH, LRU_BLOCKS, HEAD_DIM, HEAD_DIM), HEAD_DIM ** -0.5)
    b_igate = nrm(ks[14], (DEPTH, LRU_WIDTH), 0.01)
    a_c = jax.random.uniform(ks[15], (DEPTH, LRU_WIDTH), f32, 0.9, 0.999)
    s = a_c ** (1.0 / LRU_C)
    lru_lambda = jnp.log(s) - jnp.log1p(-s)
    g_qk = 1.0 + nrm(ks[16], (DEPTH, 2, HEAD_DIM), 0.05)
    g_mix_out = 1.0 + nrm(ks[17], (DEPTH, D_MIX), 0.05)
    w_out = nrm(ks[18], (DEPTH, D_MIX, D_MODEL), D_MIX ** -0.5)
    return {"x": x, "c": c, "w_ada": w_ada, "b_ada": b_ada, "g_norm": g_norm,
            "w_ffn_up": w_ffn_up, "w_ffn_down": w_ffn_down, "w_in": w_in,
            "b_fgate": b_fgate, "conv_w": conv_w, "conv_b": conv_b,
            "w_rgate": w_rgate, "b_rgate": b_rgate, "w_igate": w_igate,
            "b_igate": b_igate, "lru_lambda": lru_lambda, "g_qk": g_qk,
            "g_mix_out": g_mix_out, "w_out": w_out}


def reference(x, c, w_ada, b_ada, g_norm, w_ffn_up, w_ffn_down, w_in, b_fgate, conv_w,
              conv_b, w_rgate, b_rgate, w_igate, b_igate, lru_lambda, g_qk, g_mix_out, w_out):
    B = x.shape[0]
    c_act = jax.nn.silu(c)
    for l in range(DEPTH):
        mod = (c_act @ w_ada[l] + b_ada[l]).reshape(B, N_SUB, 3, D_MODEL)

        def norm_mod(h, j):
            shift = mod[:, j, 0][:, None, :]
            scale = mod[:, j, 1][:, None, :]
            return _rms(h) * g_norm[l, j] * (1.0 + scale) + shift

        def gate(j):
            return (1.0 + mod[:, j, 2])[:, None, :]

        x = x + 0.5 * gate(0) * _swiglu(norm_mod(x, 0), w_ffn_up[l, 0], w_ffn_down[l, 0])
        x = x + gate(1) * _mixer(norm_mod(x, 1), w_in[l], b_fgate[l], conv_w[l], conv_b[l],
                                 w_rgate[l], b_rgate[l], w_igate[l], b_igate[l],
                                 lru_lambda[l], g_qk[l], g_mix_out[l], w_out[l])
        x = x + 0.5 * gate(2) * _swiglu(norm_mod(x, 2), w_ffn_up[l, 1], w_ffn_down[l, 1])
    return x
```

```python
import functools

import jax
import jax.numpy as jnp
from jax import lax
from jax.experimental import pallas as pl
from jax.experimental.pallas import tpu as pltpu

F32 = jnp.float32
BF16 = jnp.bfloat16

HEAD_DIM = 64
CONV_WIDTH = 4
LRU_C = 8.0
N_SUB = 3
EPS = 1e-6

V7X_LANES = 128
V7X_SUBLANES = 8
V7X_VMEM_BYTES = 64 * 1024 * 1024
VMEM_LIMIT_BYTES = V7X_VMEM_BYTES - 8 * 1024 * 1024

HEADS_PER_PAIR = V7X_LANES // HEAD_DIM
AUG_PER_HEAD = 6
NEG_BIG = -1e30


def _params(*sem):
    return pltpu.CompilerParams(dimension_semantics=sem, vmem_limit_bytes=VMEM_LIMIT_BYTES)


def _sigmoid(x):
    return 1.0 / (1.0 + jnp.exp(-x))


def _log_sigmoid(x):
    return jnp.minimum(x, 0.0) - jnp.log1p(jnp.exp(-jnp.abs(x)))


def _split2(v):
    hi = v.astype(BF16)
    lo = (v - hi.astype(F32)).astype(BF16)
    return hi, lo


def _split3(v):
    hi = v.astype(BF16).astype(F32)
    r = v - hi
    mid = r.astype(BF16).astype(F32)
    lo = (r - mid).astype(BF16).astype(F32)
    return hi, mid, lo


def _mod_norm(x, g, scale, shift):
    ms = jnp.mean(x * x, axis=-1, keepdims=True)
    return (x * lax.rsqrt(ms + EPS)) * (g * (1.0 + scale)) + shift


def _mod_kernel(c_ref, w_ref, b_ref, o_ref):
    c = c_ref[...]
    ca = c * _sigmoid(c)
    o_ref[0] = jnp.dot(ca.astype(BF16), w_ref[0].astype(BF16), preferred_element_type=F32) + b_ref[0]


def _adaln_mod(c, w_ada, b_ada):
    depth, d, n = w_ada.shape
    b = c.shape[0]
    tn = 1536
    assert n % tn == 0
    return pl.pallas_call(
        _mod_kernel,
        grid=(depth, n // tn),
        in_specs=[
            pl.BlockSpec((b, d), lambda l, j: (0, 0)),
            pl.BlockSpec((1, d, tn), lambda l, j: (l, 0, j)),
            pl.BlockSpec((1, 1, tn), lambda l, j: (l, 0, j)),
        ],
        out_specs=pl.BlockSpec((1, b, tn), lambda l, j: (l, 0, j)),
        out_shape=jax.ShapeDtypeStruct((depth, b, n), F32),
        compiler_params=_params("arbitrary", "arbitrary"),
        name="adaln_mod",
    )(c, w_ada, b_ada.reshape(depth, 1, n))


def _ffn_kernel(x_ref, shift_ref, scale_ref, gate_ref, g_ref, wg_ref, wu_ref, wd_ref, o_ref,
                h_scr, acc_scr):
    x = x_ref[...]
    h_scr[...] = _mod_norm(x, g_ref[...], scale_ref[0], shift_ref[0]).astype(BF16)
    acc_scr[...] = jnp.zeros_like(acc_scr)

    def chunk(c, carry):
        hb = h_scr[...]
        g = jnp.dot(hb, wg_ref[c], preferred_element_type=F32)
        u = jnp.dot(hb, wu_ref[c], preferred_element_type=F32)
        a = (g * _sigmoid(g)) * u
        acc_scr[...] += jnp.dot(a.astype(BF16), wd_ref[c], preferred_element_type=F32)
        return carry

    lax.fori_loop(0, wg_ref.shape[0], chunk, 0)
    o_ref[...] = x_ref[...] + (0.5 * (1.0 + gate_ref[0])) * acc_scr[...]


def _ffn(x2, shift, scale, gate, g, wg, wu, wd, seq):
    m, d = x2.shape
    tm = 1024
    assert m % tm == 0 and seq % tm == 0
    per_seq = seq // tm
    vec = pl.BlockSpec((1, 1, d), lambda i: (i // per_seq, 0, 0))
    full = lambda a: pl.BlockSpec(a.shape, lambda i: (0,) * a.ndim)
    return pl.pallas_call(
        _ffn_kernel,
        grid=(m // tm,),
        in_specs=[pl.BlockSpec((tm, d), lambda i: (i, 0)), vec, vec, vec, full(g), full(wg), full(wu), full(wd)],
        out_specs=pl.BlockSpec((tm, d), lambda i: (i, 0)),
        out_shape=jax.ShapeDtypeStruct((m, d), F32),
        scratch_shapes=[pltpu.VMEM((tm, d), BF16), pltpu.VMEM((tm, d), F32)],
        compiler_params=_params("arbitrary"),
        name="ffn",
    )(x2, shift, scale, gate, g, wg, wu, wd)


def _mixin_kernel(x_ref, shift_ref, scale_ref, g_ref, wlru_ref, wsb_ref, wfox_ref, wf_ref, bf_ref,
                  cw_ref, cb_ref, wgate_ref, bgate_ref, lam_ref, gq_ref, gk_ref, gmix_ref, hsum_ref,
                  ylru_ref, sbq_ref, sbk_ref, sbv_ref, fxq_ref, fxk_ref, fxv_ref, fxqa_ref, fxka_ref,
                  xp_scr, a_scr, b_scr, hs_scr, hc_scr, fc_scr):
    ts = x_ref.shape[1]
    w = cw_ref.shape[1]
    n_heads = fxq_ref.shape[2] // HEAD_DIM
    pad = V7X_SUBLANES

    @pl.when(pl.program_id(1) == 0)
    def _():
        xp_scr[0:pad, :] = jnp.zeros((pad, w), F32)
        hc_scr[...] = jnp.zeros_like(hc_scr)
        fc_scr[...] = jnp.zeros_like(fc_scr)

    h = _mod_norm(x_ref[0], g_ref[...], scale_ref[0], shift_ref[0]).astype(BF16)

    p = jnp.dot(h, wlru_ref[...], preferred_element_type=F32)
    lru_g = p[:, w:]
    xp_scr[pad:pad + ts, :] = p[:, :w]
    u = cb_ref[...]
    for k in range(CONV_WIDTH):
        off = pad - (CONV_WIDTH - 1) + k
        u = u + cw_ref[k:k + 1, :] * xp_scr[off:off + ts, :]
    xp_scr[0:pad, :] = xp_scr[ts:ts + pad, :]

    gates = jnp.dot(u.astype(BF16), wgate_ref[...], preferred_element_type=F32) + bgate_ref[...]
    r = _sigmoid(gates[:, :w])
    ig = _sigmoid(gates[:, w:])
    lam = lam_ref[...]
    softplus_neg_lam = jnp.maximum(-lam, 0.0) + jnp.log1p(jnp.exp(-jnp.abs(lam)))
    a = jnp.exp((-LRU_C * softplus_neg_lam) * r)
    a_scr[...] = a
    b_scr[...] = jnp.sqrt(1.0 - a * a) * (ig * u)

    row8 = lax.broadcasted_iota(jnp.int32, (V7X_SUBLANES, w), 0)

    def scan8(i, hprev):
        r0 = pl.multiple_of(i * V7X_SUBLANES, V7X_SUBLANES)
        aa = a_scr[pl.ds(r0, V7X_SUBLANES), :]
        bb = b_scr[pl.ds(r0, V7X_SUBLANES), :]
        for sh in (1, 2, 4):
            a_s = pltpu.roll(aa, sh, 0)
            b_s = pltpu.roll(bb, sh, 0)
            m = row8 >= sh
            bb = jnp.where(m, aa * b_s + bb, bb)
            aa = jnp.where(m, aa * a_s, aa)
        hh = aa * hprev + bb
        hs_scr[pl.ds(r0, V7X_SUBLANES), :] = hh
        return hh[V7X_SUBLANES - 1:V7X_SUBLANES, :]

    hc_scr[0:1, :] = lax.fori_loop(0, ts // V7X_SUBLANES, scan8, hc_scr[0:1, :], unroll=4)

    y = hs_scr[...] * jax.nn.gelu(lru_g)
    yn = y * lax.rsqrt(jnp.mean(y * y, axis=-1, keepdims=True) + EPS) * gmix_ref[...]
    ylru_ref[0] = yn.astype(BF16)

    scale = HEAD_DIM ** -0.5
    psb = jnp.dot(h, wsb_ref[...], preferred_element_type=F32)
    hw = n_heads * HEAD_DIM
    sbq_ref[0] = (psb[:, :hw] * scale).astype(BF16)
    sbk_ref[0] = psb[:, hw:2 * hw].astype(BF16)
    sbv_ref[0] = psb[:, 2 * hw:].astype(BF16)

    pfx = jnp.dot(h, wfox_ref[...], preferred_element_type=F32)

    def head_rms(t):
        hi, lo = _split2(t * t)
        ssq = jnp.dot(jnp.concatenate([hi, lo], axis=1), hsum_ref[...], preferred_element_type=F32)
        return t * lax.rsqrt(ssq * (1.0 / HEAD_DIM) + EPS)

    fxq_ref[0] = (head_rms(pfx[:, :hw]) * (gq_ref[...] * scale)).astype(BF16)
    fxk_ref[0] = (head_rms(pfx[:, hw:2 * hw]) * gk_ref[...]).astype(BF16)
    fxv_ref[0] = pfx[:, 2 * hw:].astype(BF16)

    lane = lax.broadcasted_iota(jnp.int32, (ts, V7X_LANES), 1)
    logf = _log_sigmoid(jnp.dot(h, wf_ref[...], preferred_element_type=F32) + bf_ref[...])
    logf = jnp.where(lane < 3 * n_heads, logf, 0.0)
    hi, mid, lo = _split3(logf)
    pieces = jnp.where(lane < n_heads, hi, jnp.where(lane < 2 * n_heads, mid, lo)).astype(BF16)
    tri = (lax.broadcasted_iota(jnp.int32, (ts, ts), 0) >= lax.broadcasted_iota(jnp.int32, (ts, ts), 1))
    cs = jnp.dot(tri.astype(BF16), pieces, preferred_element_type=F32)
    cs = cs + pltpu.roll(cs, V7X_LANES - n_heads, 1) + pltpu.roll(cs, V7X_LANES - 2 * n_heads, 1)
    fcum = cs + fc_scr[0:1, :]
    fc_scr[0:1, :] = fcum[ts - 1:ts, :]

    for pr in range(n_heads // HEADS_PER_PAIR):
        qa = jnp.zeros((ts, V7X_LANES), F32)
        ka = jnp.zeros((ts, V7X_LANES), F32)
        for e in range(HEADS_PER_PAIR):
            hd = pr * HEADS_PER_PAIR + e
            f_hi, f_mid, f_lo = _split3(jnp.broadcast_to(fcum[:, hd:hd + 1], (ts, V7X_LANES)))
            base = e * AUG_PER_HEAD
            for j, piece in enumerate((f_hi, f_mid, f_lo)):
                qa = jnp.where(lane == base + j, piece, qa)
                ka = jnp.where(lane == base + 3 + j, -piece, ka)
            qa = jnp.where((lane >= base + 3) & (lane < base + 6), 1.0, qa)
            ka = jnp.where((lane >= base) & (lane < base + 3), 1.0, ka)
        fxqa_ref[0, :, pr * V7X_LANES:(pr + 1) * V7X_LANES] = qa.astype(BF16)
        fxka_ref[0, :, pr * V7X_LANES:(pr + 1) * V7X_LANES] = ka.astype(BF16)


def _mixer_in(x, shift, scale, g, wts, seq_tile=512):
    b, s, d = x.shape
    ts = seq_tile
    assert s % ts == 0
    w = wts["cw"].shape[1]
    hw = wts["wsb"].shape[1] // 3
    n_pairs = hw // V7X_LANES
    vec = pl.BlockSpec((1, 1, d), lambda i, j: (i, 0, 0))
    full = lambda a: pl.BlockSpec(a.shape, lambda i, j: (0,) * a.ndim)
    names = ("wlru", "wsb", "wfox", "wf", "bf", "cw", "cb", "wgate", "bgate", "lam", "gq", "gk", "gmix", "hsum")
    consts = [wts[k] for k in names]
    seq_out = lambda width: pl.BlockSpec((1, ts, width), lambda i, j: (i, j, 0))
    widths = (w, hw, hw, hw, hw, hw, hw, n_pairs * V7X_LANES, n_pairs * V7X_LANES)
    return pl.pallas_call(
        _mixin_kernel,
        grid=(b, s // ts),
        in_specs=[pl.BlockSpec((1, ts, d), lambda i, j: (i, j, 0)), vec, vec, full(g)] + [full(a) for a in consts],
        out_specs=[seq_out(n) for n in widths],
        out_shape=[jax.ShapeDtypeStruct((b, s, n), BF16) for n in widths],
        scratch_shapes=[
            pltpu.VMEM((ts + V7X_SUBLANES, w), F32),
            pltpu.VMEM((ts, w), F32),
            pltpu.VMEM((ts, w), F32),
            pltpu.VMEM((ts, w), F32),
            pltpu.VMEM((V7X_SUBLANES, w), F32),
            pltpu.VMEM((V7X_SUBLANES, V7X_LANES), F32),
        ],
        compiler_params=_params("arbitrary", "arbitrary"),
        name="mixer_in",
    )(x, shift, scale, g, *consts)


def _sb_kernel(q_ref, k_ref, v_ref, o_ref, *, tk):
    tq = q_ref.shape[1]
    n_pairs = q_ref.shape[2] // V7X_LANES
    i = pl.program_id(1)
    n_diag = tq // tk
    lane = lax.broadcasted_iota(jnp.int32, (tq, V7X_LANES), 1)
    row = lax.broadcasted_iota(jnp.int32, (tq, tk), 0)
    col = lax.broadcasted_iota(jnp.int32, (tq, tk), 1)
    strict = (lax.broadcasted_iota(jnp.int32, (tk, tk), 0) > lax.broadcasted_iota(jnp.int32, (tk, tk), 1))
    strict2 = jnp.concatenate([strict.astype(BF16)] * 2, axis=0)

    for pr in range(n_pairs):
        cols = slice(pr * V7X_LANES, (pr + 1) * V7X_LANES)
        q_pair = q_ref[0, :, cols]
        outs = []
        for e in range(HEADS_PER_PAIR):
            in_head = (lane >= e * HEAD_DIM) & (lane < (e + 1) * HEAD_DIM)
            q = jnp.where(in_head, q_pair, jnp.zeros_like(q_pair))

            def tile(j, state, masked):
                carry, acc = state
                k0 = pl.multiple_of(j * tk, tk)
                k = k_ref[0, pl.ds(k0, tk), cols]
                v = v_ref[0, pl.ds(k0, tk), cols]
                z = lax.dot_general(q, k, (((1,), (1,)), ((), ())), preferred_element_type=F32)
                log_beta = jnp.minimum(z, 0.0) - jnp.log1p(jnp.exp(-jnp.abs(z)))
                log_1mb = log_beta - z
                if masked:
                    past = (col + j * tk) < (row + i * tq)
                    log_1mb = jnp.where(past, log_1mb, 0.0)
                hi, lo = _split2(log_1mb)
                after = jnp.dot(jnp.concatenate([hi, lo], axis=1), strict2, preferred_element_type=F32)
                wgt = jnp.exp(log_beta + after + carry)
                if masked:
                    wgt = jnp.where(past, wgt, 0.0)
                acc = acc + jnp.dot(wgt.astype(BF16), v, preferred_element_type=F32)
                carry = carry + after[:, 0:1] + log_1mb[:, 0:1]
                return carry, acc

            state = (jnp.zeros((tq, 1), F32), jnp.zeros((tq, V7X_LANES), F32))
            for dj in range(n_diag):
                state = tile(i * n_diag + (n_diag - 1 - dj), state, True)
            n_full = i * n_diag
            state = lax.fori_loop(0, n_full, lambda t, st: tile(n_full - 1 - t, st, False), state)
            outs.append(state[1])
        o_ref[0, :, cols] = jnp.where(lane < HEAD_DIM, outs[0], outs[1])


def _sb_attention(q, k, v, tq=256, tk=128):
    b, s, hw = q.shape
    assert s % tq == 0 and tq % tk == 0
    return pl.pallas_call(
        functools.partial(_sb_kernel, tk=tk),
        grid=(b, s // tq),
        in_specs=[
            pl.BlockSpec((1, tq, hw), lambda i, j: (i, j, 0)),
            pl.BlockSpec((1, s, hw), lambda i, j: (i, 0, 0)),
            pl.BlockSpec((1, s, hw), lambda i, j: (i, 0, 0)),
        ],
        out_specs=pl.BlockSpec((1, tq, hw), lambda i, j: (i, j, 0)),
        out_shape=jax.ShapeDtypeStruct((b, s, hw), F32),
        compiler_params=_params("arbitrary", "arbitrary"),
        name="sb_attention",
    )(q, k, v)


def _fox_kernel(q_ref, qa_ref, k_ref, ka_ref, v_ref, o_ref, *, tk):
    tq = q_ref.shape[1]
    n_pairs = q_ref.shape[2] // V7X_LANES
    i = pl.program_id(1)
    n_diag = tq // tk
    lane = lax.broadcasted_iota(jnp.int32, (tq, V7X_LANES), 1)
    row = lax.broadcasted_iota(jnp.int32, (tq, tk), 0)
    col = lax.broadcasted_iota(jnp.int32, (tq, tk), 1)

    for pr in range(n_pairs):
        cols = slice(pr * V7X_LANES, (pr + 1) * V7X_LANES)
        q_pair = q_ref[0, :, cols]
        qa_pair = qa_ref[0, :, cols]
        outs = []
        for e in range(HEADS_PER_PAIR):
            in_head = (lane >= e * HEAD_DIM) & (lane < (e + 1) * HEAD_DIM)
            in_aug = (lane >= e * AUG_PER_HEAD) & (lane < (e + 1) * AUG_PER_HEAD)
            q = jnp.concatenate([jnp.where(in_head, q_pair, jnp.zeros_like(q_pair)),
                                 jnp.where(in_aug, qa_pair, jnp.zeros_like(qa_pair))], axis=1)

            def tile(j, state, masked):
                m, l, acc = state
                k0 = pl.multiple_of(j * tk, tk)
                k = jnp.concatenate([k_ref[0, pl.ds(k0, tk), cols], ka_ref[0, pl.ds(k0, tk), cols]], axis=1)
                v = v_ref[0, pl.ds(k0, tk), cols]
                z = lax.dot_general(q, k, (((1,), (1,)), ((), ())), preferred_element_type=F32)
                if masked:
                    z = jnp.where((col + j * tk) <= (row + i * tq), z, -jnp.inf)
                m_new = jnp.maximum(m, jnp.max(z, axis=-1, keepdims=True))
                alpha = jnp.exp(m - m_new)
                p = jnp.exp(z - m_new)
                l = alpha * l + jnp.sum(p, axis=-1, keepdims=True)
                acc = alpha * acc + jnp.dot(p.astype(BF16), v, preferred_element_type=F32)
                return m_new, l, acc

            state = (jnp.full((tq, 1), NEG_BIG, F32), jnp.zeros((tq, 1), F32), jnp.zeros((tq, V7X_LANES), F32))
            n_full = i * n_diag
            state = lax.fori_loop(0, n_full, lambda t, st: tile(t, st, False), state)
            for dj in range(n_diag):
                state = tile(n_full + dj, state, True)
            outs.append(state[2] / state[1])
        o_ref[0, :, cols] = jnp.where(lane < HEAD_DIM, outs[0], outs[1])


def _fox_attention(q, qa, k, ka, v, tq=256, tk=256):
    b, s, hw = q.shape
    assert s % tq == 0 and tq % tk == 0
    qspec = pl.BlockSpec((1, tq, hw), lambda i, j: (i, j, 0))
    kspec = pl.BlockSpec((1, s, hw), lambda i, j: (i, 0, 0))
    return pl.pallas_call(
        functools.partial(_fox_kernel, tk=tk),
        grid=(b, s // tq),
        in_specs=[qspec, qspec, kspec, kspec, kspec],
        out_specs=pl.BlockSpec((1, tq, hw), lambda i, j: (i, j, 0)),
        out_shape=jax.ShapeDtypeStruct((b, s, hw), F32),
        compiler_params=_params("arbitrary", "arbitrary"),
        name="fox_attention",
    )(q, qa, k, ka, v)


def _mixout_kernel(x_ref, gate_ref, ylru_ref, ysb_ref, yfox_ref, gsb_ref, gfox_ref, wout_ref, o_ref):
    def group_norm(y, g):
        return (y * lax.rsqrt(jnp.mean(y * y, axis=-1, keepdims=True) + EPS) * g).astype(BF16)

    y = jnp.concatenate([ylru_ref[0], group_norm(ysb_ref[0], gsb_ref[...]), group_norm(yfox_ref[0], gfox_ref[...])],
                        axis=1)
    o_ref[0] = x_ref[0] + (1.0 + gate_ref[0]) * jnp.dot(y, wout_ref[...], preferred_element_type=F32)


def _mixer_out(x, gate, ylru, ysb, yfox, gsb, gfox, wout, seq_tile=1024):
    b, s, d = x.shape
    ts = seq_tile
    assert s % ts == 0
    seq = lambda a: pl.BlockSpec((1, ts, a.shape[2]), lambda i, j: (i, j, 0))
    full = lambda a: pl.BlockSpec(a.shape, lambda i, j: (0,) * a.ndim)
    return pl.pallas_call(
        _mixout_kernel,
        grid=(b, s // ts),
        in_specs=[seq(x), pl.BlockSpec((1, 1, d), lambda i, j: (i, 0, 0)), seq(ylru), seq(ysb), seq(yfox),
                  full(gsb), full(gfox), full(wout)],
        out_specs=seq(x),
        out_shape=jax.ShapeDtypeStruct((b, s, d), F32),
        compiler_params=_params("arbitrary", "arbitrary"),
        name="mixer_out",
    )(x, gate, ylru, ysb, yfox, gsb, gfox, wout)


def _block_diag(w):
    h, di, dj = w.shape
    return jnp.einsum("hij,hg->higj", w, jnp.eye(h, dtype=w.dtype)).reshape(h * di, h * dj)


def _ffn_weights(w_up, w_down, tf=256):
    d, two_ff = w_up.shape
    ff = two_ff // 2
    assert ff % tf == 0
    nc = ff // tf
    wg = w_up[:, :ff].reshape(d, nc, tf).transpose(1, 0, 2).astype(BF16)
    wu = w_up[:, ff:].reshape(d, nc, tf).transpose(1, 0, 2).astype(BF16)
    wd = w_down.reshape(nc, tf, d).astype(BF16)
    return wg, wu, wd


def _mixer_weights(w_in, b_fgate, conv_w, conv_b, w_rgate, b_rgate, w_igate, b_igate, lru_lambda, g_qk, g_mix_out):
    w = conv_w.shape[1]
    n_heads = b_fgate.shape[0]
    hw = n_heads * HEAD_DIM
    o_sb = 2 * w
    o_fx = o_sb + 3 * hw
    o_f = o_fx + 3 * hw
    wf = jnp.zeros((w_in.shape[0], V7X_LANES), F32).at[:, :3 * n_heads].set(jnp.tile(w_in[:, o_f:o_f + n_heads], (1, 3)))
    bf = jnp.zeros((1, V7X_LANES), F32).at[0, :3 * n_heads].set(jnp.tile(b_fgate, 3))
    head_of = jnp.arange(hw) // HEAD_DIM
    hsum = (head_of[:, None] == head_of[None, :]).astype(BF16)
    return {
        "wlru": w_in[:, :o_sb].astype(BF16),
        "wsb": w_in[:, o_sb:o_fx].astype(BF16),
        "wfox": w_in[:, o_fx:o_f].astype(BF16),
        "wf": wf.astype(BF16),
        "bf": bf,
        "cw": conv_w,
        "cb": conv_b.reshape(1, w),
        "wgate": jnp.concatenate([_block_diag(w_rgate), _block_diag(w_igate)], axis=1).astype(BF16),
        "bgate": jnp.concatenate([b_rgate, b_igate]).reshape(1, 2 * w),
        "lam": lru_lambda.reshape(1, w),
        "gq": jnp.tile(g_qk[0], n_heads).reshape(1, hw),
        "gk": jnp.tile(g_qk[1], n_heads).reshape(1, hw),
        "gmix": g_mix_out[:w].reshape(1, w),
        "hsum": jnp.concatenate([hsum, hsum], axis=0),
    }


def kernel(x, c, w_ada, b_ada, g_norm, w_ffn_up, w_ffn_down, w_in, b_fgate, conv_w, conv_b, w_rgate, b_rgate,
           w_igate, b_igate, lru_lambda, g_qk, g_mix_out, w_out):
    b, s, d = x.shape
    depth = w_ada.shape[0]
    w = conv_w.shape[-1]
    hw = b_fgate.shape[-1] * HEAD_DIM

    mod = _adaln_mod(c, w_ada, b_ada).reshape(depth, b, N_SUB, 3, 1, d)

    for l in range(depth):
        shift = lambda j: mod[l, :, j, 0]
        scale = lambda j: mod[l, :, j, 1]
        gate = lambda j: mod[l, :, j, 2]
        gn = lambda j: g_norm[l, j].reshape(1, d)

        def ffn(xx, j, i):
            wg, wu, wd = _ffn_weights(w_ffn_up[l, i], w_ffn_down[l, i])
            return _ffn(xx.reshape(b * s, d), shift(j), scale(j), gate(j), gn(j), wg, wu, wd, s).reshape(b, s, d)

        x = ffn(x, 0, 0)

        wts = _mixer_weights(w_in[l], b_fgate[l], conv_w[l], conv_b[l], w_rgate[l], b_rgate[l], w_igate[l],
                             b_igate[l], lru_lambda[l], g_qk[l], g_mix_out[l])
        ylru, sbq, sbk, sbv, fxq, fxk, fxv, fxqa, fxka = _mixer_in(x, shift(1), scale(1), gn(1), wts)
        ysb = _sb_attention(sbq, sbk, sbv)
        yfox = _fox_attention(fxq, fxqa, fxk, fxka, fxv)
        x = _mixer_out(x, gate(1), ylru, ysb, yfox, g_mix_out[l, w:w + hw].reshape(1, hw),
                       g_mix_out[l, w + hw:].reshape(1, hw), w_out[l].astype(BF16))

        x = ffn(x, 2, 1)
    return x
```

```python
import functools

import jax
import jax.numpy as jnp
from jax import lax
from jax.experimental import pallas as pl
from jax.experimental.pallas import tpu as pltpu

F32 = jnp.float32
BF16 = jnp.bfloat16

HEAD_DIM = 64
CONV_WIDTH = 4
LRU_C = 8.0
N_SUB = 3
EPS = 1e-6

V7X_LANES = 128
V7X_SUBLANES = 8
V7X_VMEM_BYTES = 64 * 1024 * 1024
VMEM_LIMIT_BYTES = V7X_VMEM_BYTES - 8 * 1024 * 1024

HEADS_PER_PAIR = V7X_LANES // HEAD_DIM
AUG_PER_HEAD = 6
NEG_BIG = -1e30


def _params(*sem):
    return pltpu.CompilerParams(dimension_semantics=sem, vmem_limit_bytes=VMEM_LIMIT_BYTES)


def _sigmoid(x):
    return 1.0 / (1.0 + jnp.exp(-x))


def _log_sigmoid(x):
    return jnp.minimum(x, 0.0) - jnp.log1p(jnp.exp(-jnp.abs(x)))


def _split2(v):
    hi = v.astype(BF16)
    lo = (v - hi.astype(F32)).astype(BF16)
    return hi, lo


def _split3(v):
    hi = v.astype(BF16).astype(F32)
    r = v - hi
    mid = r.astype(BF16).astype(F32)
    lo = (r - mid).astype(BF16).astype(F32)
    return hi, mid, lo


def _mod_norm(x, g, scale, shift):
    ms = jnp.mean(x * x, axis=-1, keepdims=True)
    return (x * lax.rsqrt(ms + EPS)) * (g * (1.0 + scale)) + shift


def _mod_kernel(c_ref, w_ref, b_ref, o_ref):
    c = c_ref[...]
    ca = c * _sigmoid(c)
    o_ref[0] = jnp.dot(ca.astype(BF16), w_ref[0].astype(BF16), preferred_element_type=F32) + b_ref[0]


def _adaln_mod(c, w_ada, b_ada):
    depth, d, n = w_ada.shape
    b = c.shape[0]
    tn = 1536
    assert n % tn == 0
    return pl.pallas_call(
        _mod_kernel,
        grid=(depth, n // tn),
        in_specs=[
            pl.BlockSpec((b, d), lambda l, j: (0, 0)),
            pl.BlockSpec((1, d, tn), lambda l, j: (l, 0, j)),
            pl.BlockSpec((1, 1, tn), lambda l, j: (l, 0, j)),
        ],
        out_specs=pl.BlockSpec((1, b, tn), lambda l, j: (l, 0, j)),
        out_shape=jax.ShapeDtypeStruct((depth, b, n), F32),
        compiler_params=_params("arbitrary", "arbitrary"),
        name="adaln_mod",
    )(c, w_ada, b_ada.reshape(depth, 1, n))


def _ffn_kernel(x_ref, shift_ref, scale_ref, gate_ref, g_ref, wg_ref, wu_ref, wd_ref, o_ref,
                h_scr, acc_scr):
    x = x_ref[...]
    h_scr[...] = _mod_norm(x, g_ref[...], scale_ref[0], shift_ref[0]).astype(BF16)
    acc_scr[...] = jnp.zeros_like(acc_scr)

    def chunk(c, carry):
        hb = h_scr[...]
        g = jnp.dot(hb, wg_ref[c], preferred_element_type=F32)
        u = jnp.dot(hb, wu_ref[c], preferred_element_type=F32)
        a = (g * _sigmoid(g)) * u
        acc_scr[...] += jnp.dot(a.astype(BF16), wd_ref[c], preferred_element_type=F32)
        return carry

    lax.fori_loop(0, wg_ref.shape[0], chunk, 0)
    o_ref[...] = x_ref[...] + (0.5 * (1.0 + gate_ref[0])) * acc_scr[...]


def _ffn(x2, shift, scale, gate, g, wg, wu, wd, seq):
    m, d = x2.shape
    tm = 1024
    assert m % tm == 0 and seq % tm == 0
    per_seq = seq // tm
    vec = pl.BlockSpec((1, 1, d), lambda i: (i // per_seq, 0, 0))
    full = lambda a: pl.BlockSpec(a.shape, lambda i: (0,) * a.ndim)
    return pl.pallas_call(
        _ffn_kernel,
        grid=(m // tm,),
        in_specs=[pl.BlockSpec((tm, d), lambda i: (i, 0)), vec, vec, vec, full(g), full(wg), full(wu), full(wd)],
        out_specs=pl.BlockSpec((tm, d), lambda i: (i, 0)),
        out_shape=jax.ShapeDtypeStruct((m, d), F32),
        scratch_shapes=[pltpu.VMEM((tm, d), BF16), pltpu.VMEM((tm, d), F32)],
        compiler_params=_params("arbitrary"),
        name="ffn",
    )(x2, shift, scale, gate, g, wg, wu, wd)


def _mixin_kernel(x_ref, shift_ref, scale_ref, g_ref, wlru_ref, wsb_ref, wfox_ref, wf_ref, bf_ref,
                  cw_ref, cb_ref, wgate_ref, bgate_ref, lam_ref, gq_ref, gk_ref, gmix_ref, hsum_ref,
                  ylru_ref, sbq_ref, sbk_ref, sbv_ref, fxq_ref, fxk_ref, fxv_ref, fxqa_ref, fxka_ref,
                  xp_scr, a_scr, b_scr, hs_scr, hc_scr, fc_scr):
    ts = x_ref.shape[1]
    w = cw_ref.shape[1]
    n_heads = fxq_ref.shape[2] // HEAD_DIM
    pad = V7X_SUBLANES

    @pl.when(pl.program_id(1) == 0)
    def _():
        xp_scr[0:pad, :] = jnp.zeros((pad, w), F32)
        hc_scr[...] = jnp.zeros_like(hc_scr)
        fc_scr[...] = jnp.zeros_like(fc_scr)

    h = _mod_norm(x_ref[0], g_ref[...], scale_ref[0], shift_ref[0]).astype(BF16)

    p = jnp.dot(h, wlru_ref[...], preferred_element_type=F32)
    lru_g = p[:, w:]
    xp_scr[pad:pad + ts, :] = p[:, :w]
    u = cb_ref[...]
    for k in range(CONV_WIDTH):
        off = pad - (CONV_WIDTH - 1) + k
        u = u + cw_ref[k:k + 1, :] * xp_scr[off:off + ts, :]
    xp_scr[0:pad, :] = xp_scr[ts:ts + pad, :]

    gates = jnp.dot(u.astype(BF16), wgate_ref[...], preferred_element_type=F32) + bgate_ref[...]
    r = _sigmoid(gates[:, :w])
    ig = _sigmoid(gates[:, w:])
    lam = lam_ref[...]
    softplus_neg_lam = jnp.maximum(-lam, 0.0) + jnp.log1p(jnp.exp(-jnp.abs(lam)))
    a = jnp.exp((-LRU_C * softplus_neg_lam) * r)
    a_scr[...] = a
    b_scr[...] = jnp.sqrt(1.0 - a * a) * (ig * u)

    row8 = lax.broadcasted_iota(jnp.int32, (V7X_SUBLANES, w), 0)

    def scan8(i, hprev):
        r0 = pl.multiple_of(i * V7X_SUBLANES, V7X_SUBLANES)
        aa = a_scr[pl.ds(r0, V7X_SUBLANES), :]
        bb = b_scr[pl.ds(r0, V7X_SUBLANES), :]
        for sh in (1, 2, 4):
            a_s = pltpu.roll(aa, sh, 0)
            b_s = pltpu.roll(bb, sh, 0)
            m = row8 >= sh
            bb = jnp.where(m, aa * b_s + bb, bb)
            aa = jnp.where(m, aa * a_s, aa)
        hh = aa * hprev + bb
        hs_scr[pl.ds(r0, V7X_SUBLANES), :] = hh
        return hh[V7X_SUBLANES - 1:V7X_SUBLANES, :]

    hc_scr[0:1, :] = lax.fori_loop(0, ts // V7X_SUBLANES, scan8, hc_scr[0:1, :], unroll=4)

    y = hs_scr[...] * jax.nn.gelu(lru_g)
    yn = y * lax.rsqrt(jnp.mean(y * y, axis=-1, keepdims=True) + EPS) * gmix_ref[...]
    ylru_ref[0] = yn.astype(BF16)

    scale = HEAD_DIM ** -0.5
    psb = jnp.dot(h, wsb_ref[...], preferred_element_type=F32)
    hw = n_heads * HEAD_DIM
    sbq_ref[0] = (psb[:, :hw] * scale).astype(BF16)
    sbk_ref[0] = psb[:, hw:2 * hw].astype(BF16)
    sbv_ref[0] = psb[:, 2 * hw:].astype(BF16)

    pfx = jnp.dot(h, wfox_ref[...], preferred_element_type=F32)

    def head_rms(t):
        hi, lo = _split2(t * t)
        ssq = jnp.dot(jnp.concatenate([hi, lo], axis=1), hsum_ref[...], preferred_element_type=F32)
        return t * lax.rsqrt(ssq * (1.0 / HEAD_DIM) + EPS)

    fxq_ref[0] = (head_rms(pfx[:, :hw]) * (gq_ref[...] * scale)).astype(BF16)
    fxk_ref[0] = (head_rms(pfx[:, hw:2 * hw]) * gk_ref[...]).astype(BF16)
    fxv_ref[0] = pfx[:, 2 * hw:].astype(BF16)

    lane = lax.broadcasted_iota(jnp.int32, (ts, V7X_LANES), 1)
    logf = _log_sigmoid(jnp.dot(h, wf_ref[...], preferred_element_type=F32) + bf_ref[...])
    logf = jnp.where(lane < 3 * n_heads, logf, 0.0)
    hi, mid, lo = _split3(logf)
    pieces = jnp.where(lane < n_heads, hi, jnp.where(lane < 2 * n_heads, mid, lo)).astype(BF16)
    tri = (lax.broadcasted_iota(jnp.int32, (ts, ts), 0) >= lax.broadcasted_iota(jnp.int32, (ts, ts), 1))
    cs = jnp.dot(tri.astype(BF16), pieces, preferred_element_type=F32)
    cs = cs + pltpu.roll(cs, V7X_LANES - n_heads, 1) + pltpu.roll(cs, V7X_LANES - 2 * n_heads, 1)
    fcum = cs + fc_scr[0:1, :]
    fc_scr[0:1, :] = fcum[ts - 1:ts, :]

    for pr in range(n_heads // HEADS_PER_PAIR):
        qa = jnp.zeros((ts, V7X_LANES), F32)
        ka = jnp.zeros((ts, V7X_LANES), F32)
        for e in range(HEADS_PER_PAIR):
            hd = pr * HEADS_PER_PAIR + e
            f_hi, f_mid, f_lo = _split3(jnp.broadcast_to(fcum[:, hd:hd + 1], (ts, V7X_LANES)))
            base = e * AUG_PER_HEAD
            for j, piece in enumerate((f_hi, f_mid, f_lo)):
                qa = jnp.where(lane == base + j, piece, qa)
                ka = jnp.where(lane == base + 3 + j, -piece, ka)
            qa = jnp.where((lane >= base + 3) & (lane < base + 6), 1.0, qa)
            ka = jnp.where((lane >= base) & (lane < base + 3), 1.0, ka)
        fxqa_ref[0, :, pr * V7X_LANES:(pr + 1) * V7X_LANES] = qa.astype(BF16)
        fxka_ref[0, :, pr * V7X_LANES:(pr + 1) * V7X_LANES] = ka.astype(BF16)


def _mixer_in(x, shift, scale, g, wts, seq_tile=512):
    b, s, d = x.shape
    ts = seq_tile
    assert s % ts == 0
    w = wts["cw"].shape[1]
    hw = wts["wsb"].shape[1] // 3
    n_pairs = hw // V7X_LANES
    vec = pl.BlockSpec((1, 1, d), lambda i, j: (i, 0, 0))
    full = lambda a: pl.BlockSpec(a.shape, lambda i, j: (0,) * a.ndim)
    names = ("wlru", "wsb", "wfox", "wf", "bf", "cw", "cb", "wgate", "bgate", "lam", "gq", "gk", "gmix", "hsum")
    consts = [wts[k] for k in names]
    seq_out = lambda width: pl.BlockSpec((1, ts, width), lambda i, j: (i, j, 0))
    widths = (w, hw, hw, hw, hw, hw, hw, n_pairs * V7X_LANES, n_pairs * V7X_LANES)
    return pl.pallas_call(
        _mixin_kernel,
        grid=(b, s // ts),
        in_specs=[pl.BlockSpec((1, ts, d), lambda i, j: (i, j, 0)), vec, vec, full(g)] + [full(a) for a in consts],
        out_specs=[seq_out(n) for n in widths],
        out_shape=[jax.ShapeDtypeStruct((b, s, n), BF16) for n in widths],
        scratch_shapes=[
            pltpu.VMEM((ts + V7X_SUBLANES, w), F32),
            pltpu.VMEM((ts, w), F32),
            pltpu.VMEM((ts, w), F32),
            pltpu.VMEM((ts, w), F32),
            pltpu.VMEM((V7X_SUBLANES, w), F32),
            pltpu.VMEM((V7X_SUBLANES, V7X_LANES), F32),
        ],
        compiler_params=_params("arbitrary", "arbitrary"),
        name="mixer_in",
    )(x, shift, scale, g, *consts)


def _head_lanes(h):
    pr, e = divmod(h, HEADS_PER_PAIR)
    return slice(pr * V7X_LANES, (pr + 1) * V7X_LANES), e * HEAD_DIM


def _sb_kernel(q_ref, k_ref, v_ref, o_ref, qm_scr, carry_scr, acc_scr, *, tk):
    tq = q_ref.shape[1]
    n_heads = q_ref.shape[2] // HEAD_DIM
    i = pl.program_id(1)
    n_diag = tq // tk
    lane = lax.broadcasted_iota(jnp.int32, (tq, V7X_LANES), 1)
    row = lax.broadcasted_iota(jnp.int32, (tq, tk), 0)
    col = lax.broadcasted_iota(jnp.int32, (tq, tk), 1)
    strict = (lax.broadcasted_iota(jnp.int32, (tk, tk), 0) > lax.broadcasted_iota(jnp.int32, (tk, tk), 1))
    strict2 = jnp.concatenate([strict.astype(BF16)] * 2, axis=0)

    for h in range(n_heads):
        cols, lo_lane = _head_lanes(h)
        q_pair = q_ref[0, :, cols]
        qm_scr[h] = jnp.where((lane >= lo_lane) & (lane < lo_lane + HEAD_DIM), q_pair, jnp.zeros_like(q_pair))
    carry_scr[...] = jnp.zeros_like(carry_scr)
    acc_scr[...] = jnp.zeros_like(acc_scr)

    def tile(j, masked):
        k0 = pl.multiple_of(j * tk, tk)
        heads = range(n_heads)
        if masked:
            past = (col + j * tk) < (row + i * tq)
        z = [lax.dot_general(qm_scr[h], k_ref[0, pl.ds(k0, tk), _head_lanes(h)[0]], (((1,), (1,)), ((), ())),
                             preferred_element_type=F32) for h in heads]
        log_beta = [jnp.minimum(z[h], 0.0) - jnp.log(1.0 + jnp.exp(-jnp.abs(z[h]))) for h in heads]
        log_1mb = [log_beta[h] - z[h] for h in heads]
        if masked:
            log_1mb = [jnp.where(past, log_1mb[h], 0.0) for h in heads]
        after = [jnp.dot(jnp.concatenate(_split2(log_1mb[h]), axis=1), strict2, preferred_element_type=F32)
                 for h in heads]
        wgt = []
        for h in heads:
            carry = carry_scr[h]
            w_h = jnp.exp(log_beta[h] + after[h] + jnp.concatenate([carry] * (tk // V7X_LANES), axis=1))
            wgt.append(jnp.where(past, w_h, 0.0) if masked else w_h)
            carry_scr[h] = carry + jnp.broadcast_to(after[h][:, 0:1] + log_1mb[h][:, 0:1], (tq, V7X_LANES))
        for h in heads:
            v = v_ref[0, pl.ds(k0, tk), _head_lanes(h)[0]]
            acc_scr[h] += jnp.dot(wgt[h].astype(BF16), v, preferred_element_type=F32)

    for dj in range(n_diag):
        tile(i * n_diag + (n_diag - 1 - dj), True)
    n_full = i * n_diag

    def full_tile(t, c):
        tile(n_full - 1 - t, False)
        return c

    lax.fori_loop(0, n_full, full_tile, 0)
    for h in range(0, n_heads, HEADS_PER_PAIR):
        cols, _ = _head_lanes(h)
        o_ref[0, :, cols] = jnp.where(lane < HEAD_DIM, acc_scr[h], acc_scr[h + 1])


def _sb_attention(q, k, v, tq=256, tk=256):
    b, s, hw = q.shape
    n_heads = hw // HEAD_DIM
    assert s % tq == 0 and tq % tk == 0
    return pl.pallas_call(
        functools.partial(_sb_kernel, tk=tk),
        grid=(b, s // tq),
        in_specs=[
            pl.BlockSpec((1, tq, hw), lambda i, j: (i, j, 0)),
            pl.BlockSpec((1, s, hw), lambda i, j: (i, 0, 0)),
            pl.BlockSpec((1, s, hw), lambda i, j: (i, 0, 0)),
        ],
        out_specs=pl.BlockSpec((1, tq, hw), lambda i, j: (i, j, 0)),
        out_shape=jax.ShapeDtypeStruct((b, s, hw), F32),
        scratch_shapes=[
            pltpu.VMEM((n_heads, tq, V7X_LANES), BF16),
            pltpu.VMEM((n_heads, tq, V7X_LANES), F32),
            pltpu.VMEM((n_heads, tq, V7X_LANES), F32),
        ],
        compiler_params=_params("arbitrary", "arbitrary"),
        name="sb_attention",
    )(q, k, v)


def _fox_kernel(q_ref, qa_ref, k_ref, ka_ref, v_ref, o_ref, qm_scr, m_scr, l_scr, acc_scr, *, tk):
    tq = q_ref.shape[1]
    n_heads = q_ref.shape[2] // HEAD_DIM
    i = pl.program_id(1)
    n_diag = tq // tk
    lane = lax.broadcasted_iota(jnp.int32, (tq, V7X_LANES), 1)
    row = lax.broadcasted_iota(jnp.int32, (tq, tk), 0)
    col = lax.broadcasted_iota(jnp.int32, (tq, tk), 1)
    ones = jnp.ones((tk, V7X_LANES), BF16)

    for h in range(n_heads):
        cols, lo_lane = _head_lanes(h)
        e = h % HEADS_PER_PAIR
        q_pair = q_ref[0, :, cols]
        qa_pair = qa_ref[0, :, cols]
        in_head = (lane >= lo_lane) & (lane < lo_lane + HEAD_DIM)
        in_aug = (lane >= e * AUG_PER_HEAD) & (lane < (e + 1) * AUG_PER_HEAD)
        qm_scr[h] = jnp.concatenate([jnp.where(in_head, q_pair, jnp.zeros_like(q_pair)),
                                     jnp.where(in_aug, qa_pair, jnp.zeros_like(qa_pair))], axis=1)
    m_scr[...] = jnp.full_like(m_scr, NEG_BIG)
    l_scr[...] = jnp.zeros_like(l_scr)
    acc_scr[...] = jnp.zeros_like(acc_scr)

    def tile(j, masked):
        k0 = pl.multiple_of(j * tk, tk)
        heads = range(n_heads)
        if masked:
            visible = (col + j * tk) <= (row + i * tq)
        z = []
        for h in heads:
            cols, _ = _head_lanes(h)
            k = jnp.concatenate([k_ref[0, pl.ds(k0, tk), cols], ka_ref[0, pl.ds(k0, tk), cols]], axis=1)
            z_h = lax.dot_general(qm_scr[h], k, (((1,), (1,)), ((), ())), preferred_element_type=F32)
            z.append(jnp.where(visible, z_h, -jnp.inf) if masked else z_h)
        p, alpha = [], []
        for h in heads:
            m_prev = m_scr[h]
            m_new = jnp.maximum(m_prev, jnp.max(z[h], axis=-1, keepdims=True))
            alpha.append(jnp.exp(m_prev - m_new))
            p.append(jnp.exp(z[h] - jnp.concatenate([m_new] * (tk // V7X_LANES), axis=1)).astype(BF16))
            m_scr[h] = m_new
        for h in heads:
            v = jnp.concatenate([v_ref[0, pl.ds(k0, tk), _head_lanes(h)[0]], ones], axis=1)
            pv = jnp.dot(p[h], v, preferred_element_type=F32)
            l_scr[h] = alpha[h] * l_scr[h] + pv[:, V7X_LANES:]
            acc_scr[h] = alpha[h] * acc_scr[h] + pv[:, :V7X_LANES]

    n_full = i * n_diag

    def full_tile(t, c):
        tile(t, False)
        return c

    lax.fori_loop(0, n_full, full_tile, 0)
    for dj in range(n_diag):
        tile(n_full + dj, True)
    for h in range(0, n_heads, HEADS_PER_PAIR):
        cols, _ = _head_lanes(h)
        o_ref[0, :, cols] = jnp.where(lane < HEAD_DIM, acc_scr[h] / l_scr[h], acc_scr[h + 1] / l_scr[h + 1])


def _fox_attention(q, qa, k, ka, v, tq=256, tk=256):
    b, s, hw = q.shape
    n_heads = hw // HEAD_DIM
    assert s % tq == 0 and tq % tk == 0
    qspec = pl.BlockSpec((1, tq, hw), lambda i, j: (i, j, 0))
    kspec = pl.BlockSpec((1, s, hw), lambda i, j: (i, 0, 0))
    return pl.pallas_call(
        functools.partial(_fox_kernel, tk=tk),
        grid=(b, s // tq),
        in_specs=[qspec, qspec, kspec, kspec, kspec],
        out_specs=pl.BlockSpec((1, tq, hw), lambda i, j: (i, j, 0)),
        out_shape=jax.ShapeDtypeStruct((b, s, hw), F32),
        scratch_shapes=[
            pltpu.VMEM((n_heads, tq, 2 * V7X_LANES), BF16),
            pltpu.VMEM((n_heads, tq, V7X_LANES), F32),
            pltpu.VMEM((n_heads, tq, V7X_LANES), F32),
            pltpu.VMEM((n_heads, tq, V7X_LANES), F32),
        ],
        compiler_params=_params("arbitrary", "arbitrary"),
        name="fox_attention",
    )(q, qa, k, ka, v)


def _mixout_kernel(x_ref, gate_ref, ylru_ref, ysb_ref, yfox_ref, gsb_ref, gfox_ref, wout_ref, o_ref):
    def group_norm(y, g):
        return (y * lax.rsqrt(jnp.mean(y * y, axis=-1, keepdims=True) + EPS) * g).astype(BF16)

    y = jnp.concatenate([ylru_ref[0], group_norm(ysb_ref[0], gsb_ref[...]), group_norm(yfox_ref[0], gfox_ref[...])],
                        axis=1)
    o_ref[0] = x_ref[0] + (1.0 + gate_ref[0]) * jnp.dot(y, wout_ref[...], preferred_element_type=F32)


def _mixer_out(x, gate, ylru, ysb, yfox, gsb, gfox, wout, seq_tile=1024):
    b, s, d = x.shape
    ts = seq_tile
    assert s % ts == 0
    seq = lambda a: pl.BlockSpec((1, ts, a.shape[2]), lambda i, j: (i, j, 0))
    full = lambda a: pl.BlockSpec(a.shape, lambda i, j: (0,) * a.ndim)
    return pl.pallas_call(
        _mixout_kernel,
        grid=(b, s // ts),
        in_specs=[seq(x), pl.BlockSpec((1, 1, d), lambda i, j: (i, 0, 0)), seq(ylru), seq(ysb), seq(yfox),
                  full(gsb), full(gfox), full(wout)],
        out_specs=seq(x),
        out_shape=jax.ShapeDtypeStruct((b, s, d), F32),
        compiler_params=_params("arbitrary", "arbitrary"),
        name="mixer_out",
    )(x, gate, ylru, ysb, yfox, gsb, gfox, wout)


def _block_diag(w):
    h, di, dj = w.shape
    return jnp.einsum("hij,hg->higj", w, jnp.eye(h, dtype=w.dtype)).reshape(h * di, h * dj)


def _ffn_weights(w_up, w_down, tf=256):
    d, two_ff = w_up.shape
    ff = two_ff // 2
    assert ff % tf == 0
    nc = ff // tf
    wg = w_up[:, :ff].reshape(d, nc, tf).transpose(1, 0, 2).astype(BF16)
    wu = w_up[:, ff:].reshape(d, nc, tf).transpose(1, 0, 2).astype(BF16)
    wd = w_down.reshape(nc, tf, d).astype(BF16)
    return wg, wu, wd


def _mixer_weights(w_in, b_fgate, conv_w, conv_b, w_rgate, b_rgate, w_igate, b_igate, lru_lambda, g_qk, g_mix_out):
    w = conv_w.shape[1]
    n_heads = b_fgate.shape[0]
    hw = n_heads * HEAD_DIM
    o_sb = 2 * w
    o_fx = o_sb + 3 * hw
    o_f = o_fx + 3 * hw
    wf = jnp.zeros((w_in.shape[0], V7X_LANES), F32).at[:, :3 * n_heads].set(jnp.tile(w_in[:, o_f:o_f + n_heads], (1, 3)))
    bf = jnp.zeros((1, V7X_LANES), F32).at[0, :3 * n_heads].set(jnp.tile(b_fgate, 3))
    head_of = jnp.arange(hw) // HEAD_DIM
    hsum = (head_of[:, None] == head_of[None, :]).astype(BF16)
    return {
        "wlru": w_in[:, :o_sb].astype(BF16),
        "wsb": w_in[:, o_sb:o_fx].astype(BF16),
        "wfox": w_in[:, o_fx:o_f].astype(BF16),
        "wf": wf.astype(BF16),
        "bf": bf,
        "cw": conv_w,
        "cb": conv_b.reshape(1, w),
        "wgate": jnp.concatenate([_block_diag(w_rgate), _block_diag(w_igate)], axis=1).astype(BF16),
        "bgate": jnp.concatenate([b_rgate, b_igate]).reshape(1, 2 * w),
        "lam": lru_lambda.reshape(1, w),
        "gq": jnp.tile(g_qk[0], n_heads).reshape(1, hw),
        "gk": jnp.tile(g_qk[1], n_heads).reshape(1, hw),
        "gmix": g_mix_out[:w].reshape(1, w),
        "hsum": jnp.concatenate([hsum, hsum], axis=0),
    }


def kernel(x, c, w_ada, b_ada, g_norm, w_ffn_up, w_ffn_down, w_in, b_fgate, conv_w, conv_b, w_rgate, b_rgate,
           w_igate, b_igate, lru_lambda, g_qk, g_mix_out, w_out):
    b, s, d = x.shape
    depth = w_ada.shape[0]
    w = conv_w.shape[-1]
    hw = b_fgate.shape[-1] * HEAD_DIM

    mod = _adaln_mod(c, w_ada, b_ada).reshape(depth, b, N_SUB, 3, 1, d)

    for l in range(depth):
        shift = lambda j: mod[l, :, j, 0]
        scale = lambda j: mod[l, :, j, 1]
        gate = lambda j: mod[l, :, j, 2]
        gn = lambda j: g_norm[l, j].reshape(1, d)

        def ffn(xx, j, i):
            wg, wu, wd = _ffn_weights(w_ffn_up[l, i], w_ffn_down[l, i])
            return _ffn(xx.reshape(b * s, d), shift(j), scale(j), gate(j), gn(j), wg, wu, wd, s).reshape(b, s, d)

        x = ffn(x, 0, 0)

        wts = _mixer_weights(w_in[l], b_fgate[l], conv_w[l], conv_b[l], w_rgate[l], b_rgate[l], w_igate[l],
                             b_igate[l], lru_lambda[l], g_qk[l], g_mix_out[l])
        ylru, sbq, sbk, sbv, fxq, fxk, fxv, fxqa, fxka = _mixer_in(x, shift(1), scale(1), gn(1), wts)
        ysb = _sb_attention(sbq, sbk, sbv)
        yfox = _fox_attention(fxq, fxqa, fxk, fxka, fxv)
        x = _mixer_out(x, gate(1), ylru, ysb, yfox, g_mix_out[l, w:w + hw].reshape(1, hw),
                       g_mix_out[l, w + hw:].reshape(1, hw), w_out[l].astype(BF16))

        x = ffn(x, 2, 1)
    return x
```

```python
import functools

import jax
import jax.numpy as jnp
from jax import lax
from jax.experimental import pallas as pl
from jax.experimental.pallas import tpu as pltpu

F32 = jnp.float32
BF16 = jnp.bfloat16

HEAD_DIM = 64
CONV_WIDTH = 4
LRU_C = 8.0
N_SUB = 3
EPS = 1e-6

V7X_LANES = 128
V7X_SUBLANES = 8
V7X_VMEM_BYTES = 64 * 1024 * 1024
VMEM_LIMIT_BYTES = V7X_VMEM_BYTES - 8 * 1024 * 1024

HEADS_PER_PAIR = V7X_LANES // HEAD_DIM
AUG_PER_HEAD = 6
NEG_BIG = -1e30


def _params(*sem):
    return pltpu.CompilerParams(dimension_semantics=sem, vmem_limit_bytes=VMEM_LIMIT_BYTES)


def _sigmoid(x):
    return 1.0 / (1.0 + jnp.exp(-x))


def _log_sigmoid(x):
    return jnp.minimum(x, 0.0) - jnp.log1p(jnp.exp(-jnp.abs(x)))


def _split2(v):
    hi = v.astype(BF16)
    lo = (v - hi.astype(F32)).astype(BF16)
    return hi, lo


def _split3(v):
    hi = v.astype(BF16).astype(F32)
    r = v - hi
    mid = r.astype(BF16).astype(F32)
    lo = (r - mid).astype(BF16).astype(F32)
    return hi, mid, lo


def _mod_norm(x, g, scale, shift):
    ms = jnp.mean(x * x, axis=-1, keepdims=True)
    return (x * lax.rsqrt(ms + EPS)) * (g * (1.0 + scale)) + shift


def _mod_kernel(c_ref, w_ref, b_ref, o_ref):
    c = c_ref[...]
    ca = c * _sigmoid(c)
    o_ref[0] = jnp.dot(ca.astype(BF16), w_ref[0].astype(BF16), preferred_element_type=F32) + b_ref[0]


def _adaln_mod(c, w_ada, b_ada):
    depth, d, n = w_ada.shape
    b = c.shape[0]
    tn = 1536
    assert n % tn == 0
    return pl.pallas_call(
        _mod_kernel,
        grid=(depth, n // tn),
        in_specs=[
            pl.BlockSpec((b, d), lambda l, j: (0, 0)),
            pl.BlockSpec((1, d, tn), lambda l, j: (l, 0, j)),
            pl.BlockSpec((1, 1, tn), lambda l, j: (l, 0, j)),
        ],
        out_specs=pl.BlockSpec((1, b, tn), lambda l, j: (l, 0, j)),
        out_shape=jax.ShapeDtypeStruct((depth, b, n), F32),
        compiler_params=_params("arbitrary", "arbitrary"),
        name="adaln_mod",
    )(c, w_ada, b_ada.reshape(depth, 1, n))


def _ffn_kernel(x_ref, shift_ref, scale_ref, gate_ref, g_ref, wg_ref, wu_ref, wd_ref, o_ref,
                h_scr, acc_scr):
    x = x_ref[...]
    h_scr[...] = _mod_norm(x, g_ref[...], scale_ref[0], shift_ref[0]).astype(BF16)
    acc_scr[...] = jnp.zeros_like(acc_scr)

    def chunk(c, carry):
        hb = h_scr[...]
        g = jnp.dot(hb, wg_ref[c], preferred_element_type=F32)
        u = jnp.dot(hb, wu_ref[c], preferred_element_type=F32)
        a = (g * _sigmoid(g)) * u
        acc_scr[...] += jnp.dot(a.astype(BF16), wd_ref[c], preferred_element_type=F32)
        return carry

    lax.fori_loop(0, wg_ref.shape[0], chunk, 0)
    o_ref[...] = x_ref[...] + (0.5 * (1.0 + gate_ref[0])) * acc_scr[...]


def _ffn(x2, shift, scale, gate, g, wg, wu, wd, seq):
    m, d = x2.shape
    tm = 1024
    assert m % tm == 0 and seq % tm == 0
    per_seq = seq // tm
    vec = pl.BlockSpec((1, 1, d), lambda i: (i // per_seq, 0, 0))
    full = lambda a: pl.BlockSpec(a.shape, lambda i: (0,) * a.ndim)
    return pl.pallas_call(
        _ffn_kernel,
        grid=(m // tm,),
        in_specs=[pl.BlockSpec((tm, d), lambda i: (i, 0)), vec, vec, vec, full(g), full(wg), full(wu), full(wd)],
        out_specs=pl.BlockSpec((tm, d), lambda i: (i, 0)),
        out_shape=jax.ShapeDtypeStruct((m, d), F32),
        scratch_shapes=[pltpu.VMEM((tm, d), BF16), pltpu.VMEM((tm, d), F32)],
        compiler_params=_params("arbitrary"),
        name="ffn",
    )(x2, shift, scale, gate, g, wg, wu, wd)


def _mixin_kernel(x_ref, shift_ref, scale_ref, g_ref, wlru_ref, wsb_ref, wfox_ref, wf_ref, bf_ref,
                  cw_ref, cb_ref, wgate_ref, bgate_ref, lam_ref, gq_ref, gk_ref, gmix_ref, hsum_ref,
                  ylru_ref, sbq_ref, sbk_ref, sbv_ref, fxq_ref, fxk_ref, fxv_ref, fxqa_ref, fxka_ref,
                  xp_scr, a_scr, b_scr, hs_scr, hc_scr, fc_scr):
    ts = x_ref.shape[1]
    w = cw_ref.shape[1]
    n_heads = fxq_ref.shape[2] // HEAD_DIM
    pad = V7X_SUBLANES

    @pl.when(pl.program_id(1) == 0)
    def _():
        xp_scr[0:pad, :] = jnp.zeros((pad, w), F32)
        hc_scr[...] = jnp.zeros_like(hc_scr)
        fc_scr[...] = jnp.zeros_like(fc_scr)

    h = _mod_norm(x_ref[0], g_ref[...], scale_ref[0], shift_ref[0]).astype(BF16)

    p = jnp.dot(h, wlru_ref[...], preferred_element_type=F32)
    lru_g = p[:, w:]
    xp_scr[pad:pad + ts, :] = p[:, :w]
    u = cb_ref[...]
    for k in range(CONV_WIDTH):
        off = pad - (CONV_WIDTH - 1) + k
        u = u + cw_ref[k:k + 1, :] * xp_scr[off:off + ts, :]
    xp_scr[0:pad, :] = xp_scr[ts:ts + pad, :]

    gates = jnp.dot(u.astype(BF16), wgate_ref[...], preferred_element_type=F32) + bgate_ref[...]
    r = _sigmoid(gates[:, :w])
    ig = _sigmoid(gates[:, w:])
    lam = lam_ref[...]
    softplus_neg_lam = jnp.maximum(-lam, 0.0) + jnp.log1p(jnp.exp(-jnp.abs(lam)))
    a = jnp.exp((-LRU_C * softplus_neg_lam) * r)
    a_scr[...] = a
    b_scr[...] = jnp.sqrt(1.0 - a * a) * (ig * u)

    row8 = lax.broadcasted_iota(jnp.int32, (V7X_SUBLANES, w), 0)

    def scan8(i, hprev):
        r0 = pl.multiple_of(i * V7X_SUBLANES, V7X_SUBLANES)
        aa = a_scr[pl.ds(r0, V7X_SUBLANES), :]
        bb = b_scr[pl.ds(r0, V7X_SUBLANES), :]
        for sh in (1, 2, 4):
            a_s = pltpu.roll(aa, sh, 0)
            b_s = pltpu.roll(bb, sh, 0)
            m = row8 >= sh
            bb = jnp.where(m, aa * b_s + bb, bb)
            aa = jnp.where(m, aa * a_s, aa)
        hh = aa * hprev + bb
        hs_scr[pl.ds(r0, V7X_SUBLANES), :] = hh
        return hh[V7X_SUBLANES - 1:V7X_SUBLANES, :]

    hc_scr[0:1, :] = lax.fori_loop(0, ts // V7X_SUBLANES, scan8, hc_scr[0:1, :], unroll=4)

    y = hs_scr[...] * jax.nn.gelu(lru_g)
    yn = y * lax.rsqrt(jnp.mean(y * y, axis=-1, keepdims=True) + EPS) * gmix_ref[...]
    ylru_ref[0] = yn.astype(BF16)

    scale = HEAD_DIM ** -0.5
    psb = jnp.dot(h, wsb_ref[...], preferred_element_type=F32)
    hw = n_heads * HEAD_DIM
    sbq_ref[0] = (psb[:, :hw] * scale).astype(BF16)
    sbk_ref[0] = psb[:, hw:2 * hw].astype(BF16)
    sbv_ref[0] = psb[:, 2 * hw:].astype(BF16)

    pfx = jnp.dot(h, wfox_ref[...], preferred_element_type=F32)

    def head_rms(t):
        hi, lo = _split2(t * t)
        ssq = jnp.dot(jnp.concatenate([hi, lo], axis=1), hsum_ref[...], preferred_element_type=F32)
        return t * lax.rsqrt(ssq * (1.0 / HEAD_DIM) + EPS)

    fxq_ref[0] = (head_rms(pfx[:, :hw]) * (gq_ref[...] * scale)).astype(BF16)
    fxk_ref[0] = (head_rms(pfx[:, hw:2 * hw]) * gk_ref[...]).astype(BF16)
    fxv_ref[0] = pfx[:, 2 * hw:].astype(BF16)

    lane = lax.broadcasted_iota(jnp.int32, (ts, V7X_LANES), 1)
    logf = _log_sigmoid(jnp.dot(h, wf_ref[...], preferred_element_type=F32) + bf_ref[...])
    logf = jnp.where(lane < 3 * n_heads, logf, 0.0)
    hi, mid, lo = _split3(logf)
    pieces = jnp.where(lane < n_heads, hi, jnp.where(lane < 2 * n_heads, mid, lo)).astype(BF16)
    tri = (lax.broadcasted_iota(jnp.int32, (ts, ts), 0) >= lax.broadcasted_iota(jnp.int32, (ts, ts), 1))
    cs = jnp.dot(tri.astype(BF16), pieces, preferred_element_type=F32)
    cs = cs + pltpu.roll(cs, V7X_LANES - n_heads, 1) + pltpu.roll(cs, V7X_LANES - 2 * n_heads, 1)
    fcum = cs + fc_scr[0:1, :]
    fc_scr[0:1, :] = fcum[ts - 1:ts, :]

    for pr in range(n_heads // HEADS_PER_PAIR):
        qa = jnp.zeros((ts, V7X_LANES), F32)
        ka = jnp.zeros((ts, V7X_LANES), F32)
        for e in range(HEADS_PER_PAIR):
            hd = pr * HEADS_PER_PAIR + e
            f_hi, f_mid, f_lo = _split3(jnp.broadcast_to(fcum[:, hd:hd + 1], (ts, V7X_LANES)))
            base = e * AUG_PER_HEAD
            for j, piece in enumerate((f_hi, f_mid, f_lo)):
                qa = jnp.where(lane == base + j, piece, qa)
                ka = jnp.where(lane == base + 3 + j, -piece, ka)
            qa = jnp.where((lane >= base + 3) & (lane < base + 6), 1.0, qa)
            ka = jnp.where((lane >= base) & (lane < base + 3), 1.0, ka)
        fxqa_ref[0, :, pr * V7X_LANES:(pr + 1) * V7X_LANES] = qa.astype(BF16)
        fxka_ref[0, :, pr * V7X_LANES:(pr + 1) * V7X_LANES] = ka.astype(BF16)


def _mixer_in(x, shift, scale, g, wts, seq_tile=512):
    b, s, d = x.shape
    ts = seq_tile
    assert s % ts == 0
    w = wts["cw"].shape[1]
    hw = wts["wsb"].shape[1] // 3
    n_pairs = hw // V7X_LANES
    vec = pl.BlockSpec((1, 1, d), lambda i, j: (i, 0, 0))
    full = lambda a: pl.BlockSpec(a.shape, lambda i, j: (0,) * a.ndim)
    names = ("wlru", "wsb", "wfox", "wf", "bf", "cw", "cb", "wgate", "bgate", "lam", "gq", "gk", "gmix", "hsum")
    consts = [wts[k] for k in names]
    seq_out = lambda width: pl.BlockSpec((1, ts, width), lambda i, j: (i, j, 0))
    widths = (w, hw, hw, hw, hw, hw, hw, n_pairs * V7X_LANES, n_pairs * V7X_LANES)
    return pl.pallas_call(
        _mixin_kernel,
        grid=(b, s // ts),
        in_specs=[pl.BlockSpec((1, ts, d), lambda i, j: (i, j, 0)), vec, vec, full(g)] + [full(a) for a in consts],
        out_specs=[seq_out(n) for n in widths],
        out_shape=[jax.ShapeDtypeStruct((b, s, n), BF16) for n in widths],
        scratch_shapes=[
            pltpu.VMEM((ts + V7X_SUBLANES, w), F32),
            pltpu.VMEM((ts, w), F32),
            pltpu.VMEM((ts, w), F32),
            pltpu.VMEM((ts, w), F32),
            pltpu.VMEM((V7X_SUBLANES, w), F32),
            pltpu.VMEM((V7X_SUBLANES, V7X_LANES), F32),
        ],
        compiler_params=_params("arbitrary", "arbitrary"),
        name="mixer_in",
    )(x, shift, scale, g, *consts)


def _head_lanes(h):
    pr, e = divmod(h, HEADS_PER_PAIR)
    return slice(pr * V7X_LANES, (pr + 1) * V7X_LANES), e * HEAD_DIM


def _neg_abs(x):
    bits = lax.bitcast_convert_type(x, jnp.uint32) | jnp.uint32(0x80000000)
    return lax.bitcast_convert_type(bits, F32)


def _sb_kernel(q_ref, k_ref, v_ref, o_ref, qm_scr, z_scr, carry_scr, acc_scr):
    tq = q_ref.shape[1]
    tk = tq
    n_heads = q_ref.shape[2] // HEAD_DIM
    heads = range(n_heads)
    i = pl.program_id(1)
    lane = lax.broadcasted_iota(jnp.int32, (tq, V7X_LANES), 1)
    past = lax.broadcasted_iota(jnp.int32, (tq, tk), 1) < lax.broadcasted_iota(jnp.int32, (tq, tk), 0)
    strict = (lax.broadcasted_iota(jnp.int32, (tk, tk), 0) > lax.broadcasted_iota(jnp.int32, (tk, tk), 1)).astype(BF16)

    for h in heads:
        cols, lo_lane = _head_lanes(h)
        q_pair = q_ref[0, :, cols]
        qm_scr[h] = jnp.where((lane >= lo_lane) & (lane < lo_lane + HEAD_DIM), q_pair, jnp.zeros_like(q_pair))
    carry_scr[...] = jnp.zeros_like(carry_scr)
    acc_scr[...] = jnp.zeros_like(acc_scr)

    def score(h, j):
        k0 = pl.multiple_of(j * tk, tk)
        return lax.dot_general(qm_scr[h], k_ref[0, pl.ds(k0, tk), _head_lanes(h)[0]], (((1,), (1,)), ((), ())),
                               preferred_element_type=F32)

    def consume(j, masked, ahead):
        k0 = pl.multiple_of(j * tk, tk)
        log_beta, log_1mb = [], []
        for h in heads:
            z = z_scr[h]
            lb = jnp.minimum(z, 0.0) - jnp.log(1.0 + jnp.exp(_neg_abs(z)))
            log_beta.append(lb)
            log_1mb.append(jnp.where(past, lb - z, 0.0) if masked else lb - z)
            if ahead is not None:
                z_scr[h] = score(h, ahead)
        after = [jnp.dot(log_1mb[h].astype(BF16), strict, preferred_element_type=F32) for h in heads]
        wgt = []
        for h in heads:
            carry = carry_scr[h]
            w_h = jnp.exp(log_beta[h] + after[h] + jnp.concatenate([carry] * (tk // V7X_LANES), axis=1))
            wgt.append(jnp.where(past, w_h, 0.0) if masked else w_h)
            carry_scr[h] = carry + jnp.broadcast_to(after[h][:, 0:1] + log_1mb[h][:, 0:1], (tq, V7X_LANES))
        for h in heads:
            v = v_ref[0, pl.ds(k0, tk), _head_lanes(h)[0]]
            acc_scr[h] += jnp.dot(wgt[h].astype(BF16), v, preferred_element_type=F32)

    for h in heads:
        z_scr[h] = score(h, i)
    consume(i, True, jnp.maximum(i - 1, 0))

    def step(u, c):
        consume(i - u, False, i - u - 1)
        return c

    lax.fori_loop(1, i, step, 0)

    @pl.when(i > 0)
    def _():
        consume(0, False, None)

    for h in range(0, n_heads, HEADS_PER_PAIR):
        cols, _ = _head_lanes(h)
        o_ref[0, :, cols] = jnp.where(lane < HEAD_DIM, acc_scr[h], acc_scr[h + 1])


def _sb_attention(q, k, v, tq=256):
    b, s, hw = q.shape
    n_heads = hw // HEAD_DIM
    assert s % tq == 0
    return pl.pallas_call(
        _sb_kernel,
        grid=(b, s // tq),
        in_specs=[
            pl.BlockSpec((1, tq, hw), lambda i, j: (i, j, 0)),
            pl.BlockSpec((1, s, hw), lambda i, j: (i, 0, 0)),
            pl.BlockSpec((1, s, hw), lambda i, j: (i, 0, 0)),
        ],
        out_specs=pl.BlockSpec((1, tq, hw), lambda i, j: (i, j, 0)),
        out_shape=jax.ShapeDtypeStruct((b, s, hw), F32),
        scratch_shapes=[
            pltpu.VMEM((n_heads, tq, V7X_LANES), BF16),
            pltpu.VMEM((n_heads, tq, tq), F32),
            pltpu.VMEM((n_heads, tq, V7X_LANES), F32),
            pltpu.VMEM((n_heads, tq, V7X_LANES), F32),
        ],
        compiler_params=_params("arbitrary", "arbitrary"),
        name="sb_attention",
    )(q, k, v)


def _fox_kernel(q_ref, qa_ref, k_ref, ka_ref, v_ref, o_ref, qm_scr, z_scr, m_scr, l_scr, acc_scr):
    tq = q_ref.shape[1]
    tk = tq
    n_heads = q_ref.shape[2] // HEAD_DIM
    heads = range(n_heads)
    i = pl.program_id(1)
    lane = lax.broadcasted_iota(jnp.int32, (tq, V7X_LANES), 1)
    visible = lax.broadcasted_iota(jnp.int32, (tq, tk), 1) <= lax.broadcasted_iota(jnp.int32, (tq, tk), 0)
    ones = jnp.ones((tk, V7X_LANES), BF16)

    for h in heads:
        cols, lo_lane = _head_lanes(h)
        e = h % HEADS_PER_PAIR
        q_pair = q_ref[0, :, cols]
        qa_pair = qa_ref[0, :, cols]
        in_head = (lane >= lo_lane) & (lane < lo_lane + HEAD_DIM)
        in_aug = (lane >= e * AUG_PER_HEAD) & (lane < (e + 1) * AUG_PER_HEAD)
        qm_scr[h] = jnp.concatenate([jnp.where(in_head, q_pair, jnp.zeros_like(q_pair)),
                                     jnp.where(in_aug, qa_pair, jnp.zeros_like(qa_pair))], axis=1)
    m_scr[...] = jnp.full_like(m_scr, NEG_BIG)
    l_scr[...] = jnp.zeros_like(l_scr)
    acc_scr[...] = jnp.zeros_like(acc_scr)

    def score(h, j):
        k0 = pl.multiple_of(j * tk, tk)
        cols, _ = _head_lanes(h)
        k = jnp.concatenate([k_ref[0, pl.ds(k0, tk), cols], ka_ref[0, pl.ds(k0, tk), cols]], axis=1)
        return lax.dot_general(qm_scr[h], k, (((1,), (1,)), ((), ())), preferred_element_type=F32)

    def consume(j, masked, ahead):
        k0 = pl.multiple_of(j * tk, tk)
        p, alpha = [], []
        for h in heads:
            z = jnp.where(visible, z_scr[h], -jnp.inf) if masked else z_scr[h]
            m_prev = m_scr[h]
            m_new = jnp.maximum(m_prev, jnp.max(z, axis=-1, keepdims=True))
            alpha.append(jnp.exp(m_prev - m_new))
            p.append(jnp.exp(z - jnp.concatenate([m_new] * (tk // V7X_LANES), axis=1)).astype(BF16))
            m_scr[h] = m_new
            if ahead is not None:
                z_scr[h] = score(h, ahead)
        for h in heads:
            v = jnp.concatenate([v_ref[0, pl.ds(k0, tk), _head_lanes(h)[0]], ones], axis=1)
            pv = jnp.dot(p[h], v, preferred_element_type=F32)
            l_scr[h] = alpha[h] * l_scr[h] + pv[:, V7X_LANES:]
            acc_scr[h] = alpha[h] * acc_scr[h] + pv[:, :V7X_LANES]

    for h in heads:
        z_scr[h] = score(h, 0)

    def step(t, c):
        consume(t, False, t + 1)
        return c

    lax.fori_loop(0, i, step, 0)
    consume(i, True, None)
    for h in range(0, n_heads, HEADS_PER_PAIR):
        cols, _ = _head_lanes(h)
        o_ref[0, :, cols] = jnp.where(lane < HEAD_DIM, acc_scr[h] / l_scr[h], acc_scr[h + 1] / l_scr[h + 1])


def _fox_attention(q, qa, k, ka, v, tq=256):
    b, s, hw = q.shape
    n_heads = hw // HEAD_DIM
    assert s % tq == 0
    qspec = pl.BlockSpec((1, tq, hw), lambda i, j: (i, j, 0))
    kspec = pl.BlockSpec((1, s, hw), lambda i, j: (i, 0, 0))
    return pl.pallas_call(
        _fox_kernel,
        grid=(b, s // tq),
        in_specs=[qspec, qspec, kspec, kspec, kspec],
        out_specs=pl.BlockSpec((1, tq, hw), lambda i, j: (i, j, 0)),
        out_shape=jax.ShapeDtypeStruct((b, s, hw), F32),
        scratch_shapes=[
            pltpu.VMEM((n_heads, tq, 2 * V7X_LANES), BF16),
            pltpu.VMEM((n_heads, tq, tq), F32),
            pltpu.VMEM((n_heads, tq, V7X_LANES), F32),
            pltpu.VMEM((n_heads, tq, V7X_LANES), F32),
            pltpu.VMEM((n_heads, tq, V7X_LANES), F32),
        ],
        compiler_params=_params("arbitrary", "arbitrary"),
        name="fox_attention",
    )(q, qa, k, ka, v)


def _mixout_kernel(x_ref, gate_ref, ylru_ref, ysb_ref, yfox_ref, gsb_ref, gfox_ref, wout_ref, o_ref):
    def group_norm(y, g):
        return (y * lax.rsqrt(jnp.mean(y * y, axis=-1, keepdims=True) + EPS) * g).astype(BF16)

    y = jnp.concatenate([ylru_ref[0], group_norm(ysb_ref[0], gsb_ref[...]), group_norm(yfox_ref[0], gfox_ref[...])],
                        axis=1)
    o_ref[0] = x_ref[0] + (1.0 + gate_ref[0]) * jnp.dot(y, wout_ref[...], preferred_element_type=F32)


def _mixer_out(x, gate, ylru, ysb, yfox, gsb, gfox, wout, seq_tile=1024):
    b, s, d = x.shape
    ts = seq_tile
    assert s % ts == 0
    seq = lambda a: pl.BlockSpec((1, ts, a.shape[2]), lambda i, j: (i, j, 0))
    full = lambda a: pl.BlockSpec(a.shape, lambda i, j: (0,) * a.ndim)
    return pl.pallas_call(
        _mixout_kernel,
        grid=(b, s // ts),
        in_specs=[seq(x), pl.BlockSpec((1, 1, d), lambda i, j: (i, 0, 0)), seq(ylru), seq(ysb), seq(yfox),
                  full(gsb), full(gfox), full(wout)],
        out_specs=seq(x),
        out_shape=jax.ShapeDtypeStruct((b, s, d), F32),
        compiler_params=_params("arbitrary", "arbitrary"),
        name="mixer_out",
    )(x, gate, ylru, ysb, yfox, gsb, gfox, wout)


def _block_diag(w):
    h, di, dj = w.shape
    return jnp.einsum("hij,hg->higj", w, jnp.eye(h, dtype=w.dtype)).reshape(h * di, h * dj)


def _ffn_weights(w_up, w_down, tf=256):
    d, two_ff = w_up.shape
    ff = two_ff // 2
    assert ff % tf == 0
    nc = ff // tf
    wg = w_up[:, :ff].reshape(d, nc, tf).transpose(1, 0, 2).astype(BF16)
    wu = w_up[:, ff:].reshape(d, nc, tf).transpose(1, 0, 2).astype(BF16)
    wd = w_down.reshape(nc, tf, d).astype(BF16)
    return wg, wu, wd


def _mixer_weights(w_in, b_fgate, conv_w, conv_b, w_rgate, b_rgate, w_igate, b_igate, lru_lambda, g_qk, g_mix_out):
    w = conv_w.shape[1]
    n_heads = b_fgate.shape[0]
    hw = n_heads * HEAD_DIM
    o_sb = 2 * w
    o_fx = o_sb + 3 * hw
    o_f = o_fx + 3 * hw
    wf = jnp.zeros((w_in.shape[0], V7X_LANES), F32).at[:, :3 * n_heads].set(jnp.tile(w_in[:, o_f:o_f + n_heads], (1, 3)))
    bf = jnp.zeros((1, V7X_LANES), F32).at[0, :3 * n_heads].set(jnp.tile(b_fgate, 3))
    head_of = jnp.arange(hw) // HEAD_DIM
    hsum = (head_of[:, None] == head_of[None, :]).astype(BF16)
    return {
        "wlru": w_in[:, :o_sb].astype(BF16),
        "wsb": w_in[:, o_sb:o_fx].astype(BF16),
        "wfox": w_in[:, o_fx:o_f].astype(BF16),
        "wf": wf.astype(BF16),
        "bf": bf,
        "cw": conv_w,
        "cb": conv_b.reshape(1, w),
        "wgate": jnp.concatenate([_block_diag(w_rgate), _block_diag(w_igate)], axis=1).astype(BF16),
        "bgate": jnp.concatenate([b_rgate, b_igate]).reshape(1, 2 * w),
        "lam": lru_lambda.reshape(1, w),
        "gq": jnp.tile(g_qk[0], n_heads).reshape(1, hw),
        "gk": jnp.tile(g_qk[1], n_heads).reshape(1, hw),
        "gmix": g_mix_out[:w].reshape(1, w),
        "hsum": jnp.concatenate([hsum, hsum], axis=0),
    }


def kernel(x, c, w_ada, b_ada, g_norm, w_ffn_up, w_ffn_down, w_in, b_fgate, conv_w, conv_b, w_rgate, b_rgate,
           w_igate, b_igate, lru_lambda, g_qk, g_mix_out, w_out):
    b, s, d = x.shape
    depth = w_ada.shape[0]
    w = conv_w.shape[-1]
    hw = b_fgate.shape[-1] * HEAD_DIM

    mod = _adaln_mod(c, w_ada, b_ada).reshape(depth, b, N_SUB, 3, 1, d)

    for l in range(depth):
        shift = lambda j: mod[l, :, j, 0]
        scale = lambda j: mod[l, :, j, 1]
        gate = lambda j: mod[l, :, j, 2]
        gn = lambda j: g_norm[l, j].reshape(1, d)

        def ffn(xx, j, i):
            wg, wu, wd = _ffn_weights(w_ffn_up[l, i], w_ffn_down[l, i])
            return _ffn(xx.reshape(b * s, d), shift(j), scale(j), gate(j), gn(j), wg, wu, wd, s).reshape(b, s, d)

        x = ffn(x, 0, 0)

        wts = _mixer_weights(w_in[l], b_fgate[l], conv_w[l], conv_b[l], w_rgate[l], b_rgate[l], w_igate[l],
                             b_igate[l], lru_lambda[l], g_qk[l], g_mix_out[l])
        ylru, sbq, sbk, sbv, fxq, fxk, fxv, fxqa, fxka = _mixer_in(x, shift(1), scale(1), gn(1), wts)
        ysb = _sb_attention(sbq, sbk, sbv)
        yfox = _fox_attention(fxq, fxqa, fxk, fxka, fxv)
        x = _mixer_out(x, gate(1), ylru, ysb, yfox, g_mix_out[l, w:w + hw].reshape(1, hw),
                       g_mix_out[l, w + hw:].reshape(1, hw), w_out[l].astype(BF16))

        x = ffn(x, 2, 1)
    return x
```

```python
import functools

import jax
import jax.numpy as jnp
from jax import lax
from jax.experimental import pallas as pl
from jax.experimental.pallas import tpu as pltpu

F32 = jnp.float32
BF16 = jnp.bfloat16

HEAD_DIM = 64
CONV_WIDTH = 4
LRU_C = 8.0
N_SUB = 3
EPS = 1e-6

V7X_LANES = 128
V7X_SUBLANES = 8
V7X_VMEM_BYTES = 64 * 1024 * 1024
VMEM_LIMIT_BYTES = V7X_VMEM_BYTES - 8 * 1024 * 1024

HEADS_PER_PAIR = V7X_LANES // HEAD_DIM
AUG_PER_HEAD = 6
NEG_BIG = -1e30


def _params(*sem):
    return pltpu.CompilerParams(dimension_semantics=sem, vmem_limit_bytes=VMEM_LIMIT_BYTES)


def _sigmoid(x):
    return 1.0 / (1.0 + jnp.exp(-x))


def _log_sigmoid(x):
    return jnp.minimum(x, 0.0) - jnp.log1p(jnp.exp(-jnp.abs(x)))


def _split2(v):
    hi = v.astype(BF16)
    lo = (v - hi.astype(F32)).astype(BF16)
    return hi, lo


def _split3(v):
    hi = v.astype(BF16).astype(F32)
    r = v - hi
    mid = r.astype(BF16).astype(F32)
    lo = (r - mid).astype(BF16).astype(F32)
    return hi, mid, lo


def _mod_norm(x, g, scale, shift):
    ms = jnp.mean(x * x, axis=-1, keepdims=True)
    return (x * lax.rsqrt(ms + EPS)) * (g * (1.0 + scale)) + shift


def _mod_kernel(c_ref, w_ref, b_ref, o_ref):
    c = c_ref[...]
    ca = c * _sigmoid(c)
    o_ref[0] = jnp.dot(ca.astype(BF16), w_ref[0].astype(BF16), preferred_element_type=F32) + b_ref[0]


def _adaln_mod(c, w_ada, b_ada):
    depth, d, n = w_ada.shape
    b = c.shape[0]
    tn = 1536
    assert n % tn == 0
    return pl.pallas_call(
        _mod_kernel,
        grid=(depth, n // tn),
        in_specs=[
            pl.BlockSpec((b, d), lambda l, j: (0, 0)),
            pl.BlockSpec((1, d, tn), lambda l, j: (l, 0, j)),
            pl.BlockSpec((1, 1, tn), lambda l, j: (l, 0, j)),
        ],
        out_specs=pl.BlockSpec((1, b, tn), lambda l, j: (l, 0, j)),
        out_shape=jax.ShapeDtypeStruct((depth, b, n), F32),
        compiler_params=_params("arbitrary", "arbitrary"),
        name="adaln_mod",
    )(c, w_ada, b_ada.reshape(depth, 1, n))


def _ffn_kernel(x_ref, shift_ref, scale_ref, gate_ref, g_ref, wg_ref, wu_ref, wd_ref, o_ref,
                h_scr, acc_scr):
    x = x_ref[...]
    h_scr[...] = _mod_norm(x, g_ref[...], scale_ref[0], shift_ref[0]).astype(BF16)
    acc_scr[...] = jnp.zeros_like(acc_scr)

    def chunk(c, carry):
        hb = h_scr[...]
        g = jnp.dot(hb, wg_ref[c], preferred_element_type=F32)
        u = jnp.dot(hb, wu_ref[c], preferred_element_type=F32)
        a = (g * _sigmoid(g)) * u
        acc_scr[...] += jnp.dot(a.astype(BF16), wd_ref[c], preferred_element_type=F32)
        return carry

    lax.fori_loop(0, wg_ref.shape[0], chunk, 0)
    o_ref[...] = x_ref[...] + (0.5 * (1.0 + gate_ref[0])) * acc_scr[...]


def _ffn(x2, shift, scale, gate, g, wg, wu, wd, seq):
    m, d = x2.shape
    tm = 1024
    assert m % tm == 0 and seq % tm == 0
    per_seq = seq // tm
    vec = pl.BlockSpec((1, 1, d), lambda i: (i // per_seq, 0, 0))
    full = lambda a: pl.BlockSpec(a.shape, lambda i: (0,) * a.ndim)
    return pl.pallas_call(
        _ffn_kernel,
        grid=(m // tm,),
        in_specs=[pl.BlockSpec((tm, d), lambda i: (i, 0)), vec, vec, vec, full(g), full(wg), full(wu), full(wd)],
        out_specs=pl.BlockSpec((tm, d), lambda i: (i, 0)),
        out_shape=jax.ShapeDtypeStruct((m, d), F32),
        scratch_shapes=[pltpu.VMEM((tm, d), BF16), pltpu.VMEM((tm, d), F32)],
        compiler_params=_params("arbitrary"),
        name="ffn",
    )(x2, shift, scale, gate, g, wg, wu, wd)


def _mixin_kernel(x_ref, shift_ref, scale_ref, g_ref, wlru_ref, wsb_ref, wfox_ref, wf_ref, bf_ref,
                  cw_ref, cb_ref, wgate_ref, bgate_ref, lam_ref, gq_ref, gk_ref, gmix_ref, hsum_ref,
                  ylru_ref, sbq_ref, sbk_ref, sbv_ref, fxq_ref, fxk_ref, fxv_ref, fxqa_ref, fxka_ref,
                  xp_scr, a_scr, b_scr, hs_scr, hc_scr, fc_scr):
    ts = x_ref.shape[1]
    w = cw_ref.shape[1]
    n_heads = fxq_ref.shape[2] // HEAD_DIM
    pad = V7X_SUBLANES

    @pl.when(pl.program_id(1) == 0)
    def _():
        xp_scr[0:pad, :] = jnp.zeros((pad, w), F32)
        hc_scr[...] = jnp.zeros_like(hc_scr)
        fc_scr[...] = jnp.zeros_like(fc_scr)

    h = _mod_norm(x_ref[0], g_ref[...], scale_ref[0], shift_ref[0]).astype(BF16)

    p = jnp.dot(h, wlru_ref[...], preferred_element_type=F32)
    lru_g = p[:, w:]
    xp_scr[pad:pad + ts, :] = p[:, :w]
    u = cb_ref[...]
    for k in range(CONV_WIDTH):
        off = pad - (CONV_WIDTH - 1) + k
        u = u + cw_ref[k:k + 1, :] * xp_scr[off:off + ts, :]
    xp_scr[0:pad, :] = xp_scr[ts:ts + pad, :]

    gates = jnp.dot(u.astype(BF16), wgate_ref[...], preferred_element_type=F32) + bgate_ref[...]
    r = _sigmoid(gates[:, :w])
    ig = _sigmoid(gates[:, w:])
    lam = lam_ref[...]
    softplus_neg_lam = jnp.maximum(-lam, 0.0) + jnp.log1p(jnp.exp(-jnp.abs(lam)))
    a = jnp.exp((-LRU_C * softplus_neg_lam) * r)
    a_scr[...] = a
    b_scr[...] = jnp.sqrt(1.0 - a * a) * (ig * u)

    row8 = lax.broadcasted_iota(jnp.int32, (V7X_SUBLANES, w), 0)

    def scan8(i, hprev):
        r0 = pl.multiple_of(i * V7X_SUBLANES, V7X_SUBLANES)
        aa = a_scr[pl.ds(r0, V7X_SUBLANES), :]
        bb = b_scr[pl.ds(r0, V7X_SUBLANES), :]
        for sh in (1, 2, 4):
            a_s = pltpu.roll(aa, sh, 0)
            b_s = pltpu.roll(bb, sh, 0)
            m = row8 >= sh
            bb = jnp.where(m, aa * b_s + bb, bb)
            aa = jnp.where(m, aa * a_s, aa)
        hh = aa * hprev + bb
        hs_scr[pl.ds(r0, V7X_SUBLANES), :] = hh
        return hh[V7X_SUBLANES - 1:V7X_SUBLANES, :]

    hc_scr[0:1, :] = lax.fori_loop(0, ts // V7X_SUBLANES, scan8, hc_scr[0:1, :], unroll=4)

    y = hs_scr[...] * jax.nn.gelu(lru_g)
    yn = y * lax.rsqrt(jnp.mean(y * y, axis=-1, keepdims=True) + EPS) * gmix_ref[...]
    ylru_ref[0] = yn.astype(BF16)

    scale = HEAD_DIM ** -0.5
    psb = jnp.dot(h, wsb_ref[...], preferred_element_type=F32)
    hw = n_heads * HEAD_DIM
    sbq_ref[0] = (psb[:, :hw] * scale).astype(BF16)
    sbk_ref[0] = psb[:, hw:2 * hw].astype(BF16)
    sbv_ref[0] = psb[:, 2 * hw:].astype(BF16)

    pfx = jnp.dot(h, wfox_ref[...], preferred_element_type=F32)

    def head_rms(t):
        hi, lo = _split2(t * t)
        ssq = jnp.dot(jnp.concatenate([hi, lo], axis=1), hsum_ref[...], preferred_element_type=F32)
        return t * lax.rsqrt(ssq * (1.0 / HEAD_DIM) + EPS)

    fxq_ref[0] = (head_rms(pfx[:, :hw]) * (gq_ref[...] * scale)).astype(BF16)
    fxk_ref[0] = (head_rms(pfx[:, hw:2 * hw]) * gk_ref[...]).astype(BF16)
    fxv_ref[0] = pfx[:, 2 * hw:].astype(BF16)

    lane = lax.broadcasted_iota(jnp.int32, (ts, V7X_LANES), 1)
    logf = _log_sigmoid(jnp.dot(h, wf_ref[...], preferred_element_type=F32) + bf_ref[...])
    logf = jnp.where(lane < 3 * n_heads, logf, 0.0)
    hi, mid, lo = _split3(logf)
    pieces = jnp.where(lane < n_heads, hi, jnp.where(lane < 2 * n_heads, mid, lo)).astype(BF16)
    tri = (lax.broadcasted_iota(jnp.int32, (ts, ts), 0) >= lax.broadcasted_iota(jnp.int32, (ts, ts), 1))
    cs = jnp.dot(tri.astype(BF16), pieces, preferred_element_type=F32)
    cs = cs + pltpu.roll(cs, V7X_LANES - n_heads, 1) + pltpu.roll(cs, V7X_LANES - 2 * n_heads, 1)
    fcum = cs + fc_scr[0:1, :]
    fc_scr[0:1, :] = fcum[ts - 1:ts, :]

    for pr in range(n_heads // HEADS_PER_PAIR):
        qa = jnp.zeros((ts, V7X_LANES), F32)
        ka = jnp.zeros((ts, V7X_LANES), F32)
        for e in range(HEADS_PER_PAIR):
            hd = pr * HEADS_PER_PAIR + e
            f_hi, f_mid, f_lo = _split3(jnp.broadcast_to(fcum[:, hd:hd + 1], (ts, V7X_LANES)))
            base = e * AUG_PER_HEAD
            for j, piece in enumerate((f_hi, f_mid, f_lo)):
                qa = jnp.where(lane == base + j, piece, qa)
                ka = jnp.where(lane == base + 3 + j, -piece, ka)
            qa = jnp.where((lane >= base + 3) & (lane < base + 6), 1.0, qa)
            ka = jnp.where((lane >= base) & (lane < base + 3), 1.0, ka)
        fxqa_ref[0, :, pr * V7X_LANES:(pr + 1) * V7X_LANES] = qa.astype(BF16)
        fxka_ref[0, :, pr * V7X_LANES:(pr + 1) * V7X_LANES] = ka.astype(BF16)


def _mixer_in(x, shift, scale, g, wts, seq_tile=512):
    b, s, d = x.shape
    ts = seq_tile
    assert s % ts == 0
    w = wts["cw"].shape[1]
    hw = wts["wsb"].shape[1] // 3
    n_pairs = hw // V7X_LANES
    vec = pl.BlockSpec((1, 1, d), lambda i, j: (i, 0, 0))
    full = lambda a: pl.BlockSpec(a.shape, lambda i, j: (0,) * a.ndim)
    names = ("wlru", "wsb", "wfox", "wf", "bf", "cw", "cb", "wgate", "bgate", "lam", "gq", "gk", "gmix", "hsum")
    consts = [wts[k] for k in names]
    seq_out = lambda width: pl.BlockSpec((1, ts, width), lambda i, j: (i, j, 0))
    widths = (w, hw, hw, hw, hw, hw, hw, n_pairs * V7X_LANES, n_pairs * V7X_LANES)
    return pl.pallas_call(
        _mixin_kernel,
        grid=(b, s // ts),
        in_specs=[pl.BlockSpec((1, ts, d), lambda i, j: (i, j, 0)), vec, vec, full(g)] + [full(a) for a in consts],
        out_specs=[seq_out(n) for n in widths],
        out_shape=[jax.ShapeDtypeStruct((b, s, n), BF16) for n in widths],
        scratch_shapes=[
            pltpu.VMEM((ts + V7X_SUBLANES, w), F32),
            pltpu.VMEM((ts, w), F32),
            pltpu.VMEM((ts, w), F32),
            pltpu.VMEM((ts, w), F32),
            pltpu.VMEM((V7X_SUBLANES, w), F32),
            pltpu.VMEM((V7X_SUBLANES, V7X_LANES), F32),
        ],
        compiler_params=_params("arbitrary", "arbitrary"),
        name="mixer_in",
    )(x, shift, scale, g, *consts)


def _head_lanes(h):
    pr, e = divmod(h, HEADS_PER_PAIR)
    return slice(pr * V7X_LANES, (pr + 1) * V7X_LANES), e * HEAD_DIM


def _neg_abs(x):
    bits = lax.bitcast_convert_type(x, jnp.uint32) | jnp.uint32(0x80000000)
    return lax.bitcast_convert_type(bits, F32)


def _sb_kernel(q_ref, k_ref, v_ref, o_ref, qm_scr, z_scr, w_scr, carry_scr, acc_scr):
    tq = q_ref.shape[1]
    tk = tq
    n_heads = q_ref.shape[2] // HEAD_DIM
    heads = range(n_heads)
    i = pl.program_id(1)
    lane = lax.broadcasted_iota(jnp.int32, (tq, V7X_LANES), 1)
    past = lax.broadcasted_iota(jnp.int32, (tq, tk), 1) < lax.broadcasted_iota(jnp.int32, (tq, tk), 0)
    strict = (lax.broadcasted_iota(jnp.int32, (tk, tk), 0) > lax.broadcasted_iota(jnp.int32, (tk, tk), 1)).astype(BF16)

    for h in heads:
        cols, lo_lane = _head_lanes(h)
        q_pair = q_ref[0, :, cols]
        qm_scr[h] = jnp.where((lane >= lo_lane) & (lane < lo_lane + HEAD_DIM), q_pair, jnp.zeros_like(q_pair))
    carry_scr[...] = jnp.zeros_like(carry_scr)
    acc_scr[...] = jnp.zeros_like(acc_scr)

    def score(h, j):
        k0 = pl.multiple_of(j * tk, tk)
        return lax.dot_general(qm_scr[h], k_ref[0, pl.ds(k0, tk), _head_lanes(h)[0]], (((1,), (1,)), ((), ())),
                               preferred_element_type=F32)

    def weighted_values(h, j):
        k0 = pl.multiple_of(j * tk, tk)
        v = v_ref[0, pl.ds(k0, tk), _head_lanes(h)[0]]
        acc_scr[h] += jnp.dot(w_scr[h], v, preferred_element_type=F32)

    def consume(masked, ahead, behind):
        log_beta, log_1mb, after = [], [], []
        for h in heads:
            z = z_scr[h]
            lb = jnp.minimum(z, 0.0) - jnp.log(1.0 + jnp.exp(_neg_abs(z)))
            log_beta.append(lb)
            log_1mb.append(jnp.where(past, lb - z, 0.0) if masked else lb - z)
            after.append(jnp.dot(log_1mb[h].astype(BF16), strict, preferred_element_type=F32))
            if behind is not None:
                weighted_values(h, behind)
            if ahead is not None:
                z_scr[h] = score(h, ahead)
        for h in heads:
            carry = carry_scr[h]
            w_h = jnp.exp(log_beta[h] + after[h] + jnp.concatenate([carry] * (tk // V7X_LANES), axis=1))
            w_scr[h] = (jnp.where(past, w_h, 0.0) if masked else w_h).astype(BF16)
            carry_scr[h] = carry + jnp.broadcast_to(after[h][:, 0:1] + log_1mb[h][:, 0:1], (tq, V7X_LANES))

    for h in heads:
        z_scr[h] = score(h, i)
    consume(True, jnp.maximum(i - 1, 0), None)

    def step(u, c):
        consume(False, i - u - 1, i - u + 1)
        return c

    lax.fori_loop(1, i, step, 0)

    @pl.when(i > 0)
    def _():
        consume(False, None, 1)

    for h in heads:
        weighted_values(h, 0)

    for h in range(0, n_heads, HEADS_PER_PAIR):
        cols, _ = _head_lanes(h)
        o_ref[0, :, cols] = jnp.where(lane < HEAD_DIM, acc_scr[h], acc_scr[h + 1])


def _sb_attention(q, k, v, tq=256):
    b, s, hw = q.shape
    n_heads = hw // HEAD_DIM
    assert s % tq == 0
    return pl.pallas_call(
        _sb_kernel,
        grid=(b, s // tq),
        in_specs=[
            pl.BlockSpec((1, tq, hw), lambda i, j: (i, j, 0)),
            pl.BlockSpec((1, s, hw), lambda i, j: (i, 0, 0)),
            pl.BlockSpec((1, s, hw), lambda i, j: (i, 0, 0)),
        ],
        out_specs=pl.BlockSpec((1, tq, hw), lambda i, j: (i, j, 0)),
        out_shape=jax.ShapeDtypeStruct((b, s, hw), F32),
        scratch_shapes=[
            pltpu.VMEM((n_heads, tq, V7X_LANES), BF16),
            pltpu.VMEM((n_heads, tq, tq), F32),
            pltpu.VMEM((n_heads, tq, tq), BF16),
            pltpu.VMEM((n_heads, tq, V7X_LANES), F32),
            pltpu.VMEM((n_heads, tq, V7X_LANES), F32),
        ],
        compiler_params=_params("arbitrary", "arbitrary"),
        name="sb_attention",
    )(q, k, v)


def _fox_kernel(q_ref, qa_ref, k_ref, ka_ref, v_ref, o_ref, qm_scr, z_scr, p_scr, mx_scr, alpha_scr, m_scr, l_scr,
                acc_scr, *, tk):
    tq = q_ref.shape[1]
    n_heads = q_ref.shape[2] // HEAD_DIM
    heads = range(n_heads)
    i = pl.program_id(1)
    lane = lax.broadcasted_iota(jnp.int32, (tq, V7X_LANES), 1)
    col_minus_row = lax.broadcasted_iota(jnp.int32, (tq, tk), 1) - lax.broadcasted_iota(jnp.int32, (tq, tk), 0)
    ones = jnp.ones((tk, V7X_LANES), BF16)

    for h in heads:
        cols, lo_lane = _head_lanes(h)
        e = h % HEADS_PER_PAIR
        q_pair = q_ref[0, :, cols]
        qa_pair = qa_ref[0, :, cols]
        in_head = (lane >= lo_lane) & (lane < lo_lane + HEAD_DIM)
        in_aug = (lane >= e * AUG_PER_HEAD) & (lane < (e + 1) * AUG_PER_HEAD)
        qm_scr[h] = jnp.concatenate([jnp.where(in_head, q_pair, jnp.zeros_like(q_pair)),
                                     jnp.where(in_aug, qa_pair, jnp.zeros_like(qa_pair))], axis=1)
    m_scr[...] = jnp.full_like(m_scr, NEG_BIG)
    l_scr[...] = jnp.zeros_like(l_scr)
    acc_scr[...] = jnp.zeros_like(acc_scr)

    def score(h, j):
        k0 = pl.multiple_of(j * tk, tk)
        cols, _ = _head_lanes(h)
        k = jnp.concatenate([k_ref[0, pl.ds(k0, tk), cols], ka_ref[0, pl.ds(k0, tk), cols]], axis=1)
        return lax.dot_general(qm_scr[h], k, (((1,), (1,)), ((), ())), preferred_element_type=F32)

    def stage_scores(h, j, limit):
        z = score(h, j)
        if limit is not None:
            z = jnp.where(col_minus_row <= limit, z, -jnp.inf)
        z_scr[h] = z
        mx_scr[h] = jnp.broadcast_to(jnp.max(z, axis=-1, keepdims=True), (tq, V7X_LANES))

    def weighted_values(h, j):
        k0 = pl.multiple_of(j * tk, tk)
        v = jnp.concatenate([v_ref[0, pl.ds(k0, tk), _head_lanes(h)[0]], ones], axis=1)
        pv = jnp.dot(p_scr[h], v, preferred_element_type=F32)
        alpha = alpha_scr[h]
        l_scr[h] = alpha * l_scr[h] + pv[:, V7X_LANES:]
        acc_scr[h] = alpha * acc_scr[h] + pv[:, :V7X_LANES]

    def step(ahead, ahead_limit, behind):
        for h in heads:
            if behind is not None:
                weighted_values(h, behind)
            m_prev = m_scr[h]
            m_new = jnp.maximum(m_prev, mx_scr[h])
            alpha_scr[h] = jnp.exp(m_prev - m_new)
            p_scr[h] = jnp.exp(z_scr[h] - jnp.concatenate([m_new] * (tk // V7X_LANES), axis=1)).astype(BF16)
            m_scr[h] = m_new
            if ahead is not None:
                stage_scores(h, ahead, ahead_limit)

    last = (i * tq) // tk
    diag_limit = i * tq - last * tk
    everything = tk

    for h in heads:
        stage_scores(h, 0, jnp.where(last == 0, diag_limit, everything))

    @pl.when(last > 0)
    def _():
        step(1, jnp.where(last == 1, diag_limit, everything), None)

    def middle(t, c):
        step(t + 1, None, t - 1)
        return c

    lax.fori_loop(1, last - 1, middle, 0)

    @pl.when(last > 1)
    def _():
        step(last, diag_limit, last - 2)

    @pl.when(last > 0)
    def _():
        step(None, None, last - 1)

    @pl.when(last == 0)
    def _():
        step(None, None, None)

    for h in heads:
        weighted_values(h, last)
    for h in range(0, n_heads, HEADS_PER_PAIR):
        cols, _ = _head_lanes(h)
        o_ref[0, :, cols] = jnp.where(lane < HEAD_DIM, acc_scr[h] / l_scr[h], acc_scr[h + 1] / l_scr[h + 1])


def _fox_attention(q, qa, k, ka, v, tq=256, tk=512):
    b, s, hw = q.shape
    n_heads = hw // HEAD_DIM
    assert s % tq == 0 and s % tk == 0 and tk % tq == 0
    qspec = pl.BlockSpec((1, tq, hw), lambda i, j: (i, j, 0))
    kspec = pl.BlockSpec((1, s, hw), lambda i, j: (i, 0, 0))
    return pl.pallas_call(
        functools.partial(_fox_kernel, tk=tk),
        grid=(b, s // tq),
        in_specs=[qspec, qspec, kspec, kspec, kspec],
        out_specs=pl.BlockSpec((1, tq, hw), lambda i, j: (i, j, 0)),
        out_shape=jax.ShapeDtypeStruct((b, s, hw), F32),
        scratch_shapes=[
            pltpu.VMEM((n_heads, tq, 2 * V7X_LANES), BF16),
            pltpu.VMEM((n_heads, tq, tk), F32),
            pltpu.VMEM((n_heads, tq, tk), BF16),
            pltpu.VMEM((n_heads, tq, V7X_LANES), F32),
            pltpu.VMEM((n_heads, tq, V7X_LANES), F32),
            pltpu.VMEM((n_heads, tq, V7X_LANES), F32),
            pltpu.VMEM((n_heads, tq, V7X_LANES), F32),
            pltpu.VMEM((n_heads, tq, V7X_LANES), F32),
        ],
        compiler_params=_params("arbitrary", "arbitrary"),
        name="fox_attention",
    )(q, qa, k, ka, v)


def _fox_kernel_v3(q_ref, qa_ref, k_ref, ka_ref, v_ref, o_ref, qm_scr, z_scr, m_scr, l_scr, acc_scr):
    tq = q_ref.shape[1]
    tk = tq
    n_heads = q_ref.shape[2] // HEAD_DIM
    heads = range(n_heads)
    i = pl.program_id(1)
    lane = lax.broadcasted_iota(jnp.int32, (tq, V7X_LANES), 1)
    visible = lax.broadcasted_iota(jnp.int32, (tq, tk), 1) <= lax.broadcasted_iota(jnp.int32, (tq, tk), 0)
    ones = jnp.ones((tk, V7X_LANES), BF16)

    for h in heads:
        cols, lo_lane = _head_lanes(h)
        e = h % HEADS_PER_PAIR
        q_pair = q_ref[0, :, cols]
        qa_pair = qa_ref[0, :, cols]
        in_head = (lane >= lo_lane) & (lane < lo_lane + HEAD_DIM)
        in_aug = (lane >= e * AUG_PER_HEAD) & (lane < (e + 1) * AUG_PER_HEAD)
        qm_scr[h] = jnp.concatenate([jnp.where(in_head, q_pair, jnp.zeros_like(q_pair)),
                                     jnp.where(in_aug, qa_pair, jnp.zeros_like(qa_pair))], axis=1)
    m_scr[...] = jnp.full_like(m_scr, NEG_BIG)
    l_scr[...] = jnp.zeros_like(l_scr)
    acc_scr[...] = jnp.zeros_like(acc_scr)

    def score(h, j):
        k0 = pl.multiple_of(j * tk, tk)
        cols, _ = _head_lanes(h)
        k = jnp.concatenate([k_ref[0, pl.ds(k0, tk), cols], ka_ref[0, pl.ds(k0, tk), cols]], axis=1)
        return lax.dot_general(qm_scr[h], k, (((1,), (1,)), ((), ())), preferred_element_type=F32)

    def consume(j, masked, ahead):
        k0 = pl.multiple_of(j * tk, tk)
        p, alpha = [], []
        for h in heads:
            z = jnp.where(visible, z_scr[h], -jnp.inf) if masked else z_scr[h]
            m_prev = m_scr[h]
            m_new = jnp.maximum(m_prev, jnp.max(z, axis=-1, keepdims=True))
            alpha.append(jnp.exp(m_prev - m_new))
            p.append(jnp.exp(z - jnp.concatenate([m_new] * (tk // V7X_LANES), axis=1)).astype(BF16))
            m_scr[h] = m_new
            if ahead is not None:
                z_scr[h] = score(h, ahead)
        for h in heads:
            v = jnp.concatenate([v_ref[0, pl.ds(k0, tk), _head_lanes(h)[0]], ones], axis=1)
            pv = jnp.dot(p[h], v, preferred_element_type=F32)
            l_scr[h] = alpha[h] * l_scr[h] + pv[:, V7X_LANES:]
            acc_scr[h] = alpha[h] * acc_scr[h] + pv[:, :V7X_LANES]

    for h in heads:
        z_scr[h] = score(h, 0)

    def step(t, c):
        consume(t, False, t + 1)
        return c

    lax.fori_loop(0, i, step, 0)
    consume(i, True, None)
    for h in range(0, n_heads, HEADS_PER_PAIR):
        cols, _ = _head_lanes(h)
        o_ref[0, :, cols] = jnp.where(lane < HEAD_DIM, acc_scr[h] / l_scr[h], acc_scr[h + 1] / l_scr[h + 1])


def _fox_attention_v3(q, qa, k, ka, v, tq=256):
    b, s, hw = q.shape
    n_heads = hw // HEAD_DIM
    assert s % tq == 0
    qspec = pl.BlockSpec((1, tq, hw), lambda i, j: (i, j, 0))
    kspec = pl.BlockSpec((1, s, hw), lambda i, j: (i, 0, 0))
    return pl.pallas_call(
        _fox_kernel_v3,
        grid=(b, s // tq),
        in_specs=[qspec, qspec, kspec, kspec, kspec],
        out_specs=pl.BlockSpec((1, tq, hw), lambda i, j: (i, j, 0)),
        out_shape=jax.ShapeDtypeStruct((b, s, hw), F32),
        scratch_shapes=[
            pltpu.VMEM((n_heads, tq, 2 * V7X_LANES), BF16),
            pltpu.VMEM((n_heads, tq, tq), F32),
            pltpu.VMEM((n_heads, tq, V7X_LANES), F32),
            pltpu.VMEM((n_heads, tq, V7X_LANES), F32),
            pltpu.VMEM((n_heads, tq, V7X_LANES), F32),
        ],
        compiler_params=_params("arbitrary", "arbitrary"),
        name="fox_attention",
    )(q, qa, k, ka, v)


def _mixout_kernel(x_ref, gate_ref, ylru_ref, ysb_ref, yfox_ref, gsb_ref, gfox_ref, wout_ref, o_ref):
    def group_norm(y, g):
        return (y * lax.rsqrt(jnp.mean(y * y, axis=-1, keepdims=True) + EPS) * g).astype(BF16)

    y = jnp.concatenate([ylru_ref[0], group_norm(ysb_ref[0], gsb_ref[...]), group_norm(yfox_ref[0], gfox_ref[...])],
                        axis=1)
    o_ref[0] = x_ref[0] + (1.0 + gate_ref[0]) * jnp.dot(y, wout_ref[...], preferred_element_type=F32)


def _mixer_out(x, gate, ylru, ysb, yfox, gsb, gfox, wout, seq_tile=1024):
    b, s, d = x.shape
    ts = seq_tile
    assert s % ts == 0
    seq = lambda a: pl.BlockSpec((1, ts, a.shape[2]), lambda i, j: (i, j, 0))
    full = lambda a: pl.BlockSpec(a.shape, lambda i, j: (0,) * a.ndim)
    return pl.pallas_call(
        _mixout_kernel,
        grid=(b, s // ts),
        in_specs=[seq(x), pl.BlockSpec((1, 1, d), lambda i, j: (i, 0, 0)), seq(ylru), seq(ysb), seq(yfox),
                  full(gsb), full(gfox), full(wout)],
        out_specs=seq(x),
        out_shape=jax.ShapeDtypeStruct((b, s, d), F32),
        compiler_params=_params("arbitrary", "arbitrary"),
        name="mixer_out",
    )(x, gate, ylru, ysb, yfox, gsb, gfox, wout)


def _block_diag(w):
    h, di, dj = w.shape
    return jnp.einsum("hij,hg->higj", w, jnp.eye(h, dtype=w.dtype)).reshape(h * di, h * dj)


def _ffn_weights(w_up, w_down, tf=256):
    d, two_ff = w_up.shape
    ff = two_ff // 2
    assert ff % tf == 0
    nc = ff // tf
    wg = w_up[:, :ff].reshape(d, nc, tf).transpose(1, 0, 2).astype(BF16)
    wu = w_up[:, ff:].reshape(d, nc, tf).transpose(1, 0, 2).astype(BF16)
    wd = w_down.reshape(nc, tf, d).astype(BF16)
    return wg, wu, wd


def _mixer_weights(w_in, b_fgate, conv_w, conv_b, w_rgate, b_rgate, w_igate, b_igate, lru_lambda, g_qk, g_mix_out):
    w = conv_w.shape[1]
    n_heads = b_fgate.shape[0]
    hw = n_heads * HEAD_DIM
    o_sb = 2 * w
    o_fx = o_sb + 3 * hw
    o_f = o_fx + 3 * hw
    wf = jnp.zeros((w_in.shape[0], V7X_LANES), F32).at[:, :3 * n_heads].set(jnp.tile(w_in[:, o_f:o_f + n_heads], (1, 3)))
    bf = jnp.zeros((1, V7X_LANES), F32).at[0, :3 * n_heads].set(jnp.tile(b_fgate, 3))
    head_of = jnp.arange(hw) // HEAD_DIM
    hsum = (head_of[:, None] == head_of[None, :]).astype(BF16)
    return {
        "wlru": w_in[:, :o_sb].astype(BF16),
        "wsb": w_in[:, o_sb:o_fx].astype(BF16),
        "wfox": w_in[:, o_fx:o_f].astype(BF16),
        "wf": wf.astype(BF16),
        "bf": bf,
        "cw": conv_w,
        "cb": conv_b.reshape(1, w),
        "wgate": jnp.concatenate([_block_diag(w_rgate), _block_diag(w_igate)], axis=1).astype(BF16),
        "bgate": jnp.concatenate([b_rgate, b_igate]).reshape(1, 2 * w),
        "lam": lru_lambda.reshape(1, w),
        "gq": jnp.tile(g_qk[0], n_heads).reshape(1, hw),
        "gk": jnp.tile(g_qk[1], n_heads).reshape(1, hw),
        "gmix": g_mix_out[:w].reshape(1, w),
        "hsum": jnp.concatenate([hsum, hsum], axis=0),
    }


def kernel(x, c, w_ada, b_ada, g_norm, w_ffn_up, w_ffn_down, w_in, b_fgate, conv_w, conv_b, w_rgate, b_rgate,
           w_igate, b_igate, lru_lambda, g_qk, g_mix_out, w_out):
    b, s, d = x.shape
    depth = w_ada.shape[0]
    w = conv_w.shape[-1]
    hw = b_fgate.shape[-1] * HEAD_DIM

    mod = _adaln_mod(c, w_ada, b_ada).reshape(depth, b, N_SUB, 3, 1, d)

    for l in range(depth):
        shift = lambda j: mod[l, :, j, 0]
        scale = lambda j: mod[l, :, j, 1]
        gate = lambda j: mod[l, :, j, 2]
        gn = lambda j: g_norm[l, j].reshape(1, d)

        def ffn(xx, j, i):
            wg, wu, wd = _ffn_weights(w_ffn_up[l, i], w_ffn_down[l, i])
            return _ffn(xx.reshape(b * s, d), shift(j), scale(j), gate(j), gn(j), wg, wu, wd, s).reshape(b, s, d)

        x = ffn(x, 0, 0)

        wts = _mixer_weights(w_in[l], b_fgate[l], conv_w[l], conv_b[l], w_rgate[l], b_rgate[l], w_igate[l],
                             b_igate[l], lru_lambda[l], g_qk[l], g_mix_out[l])
        ylru, sbq, sbk, sbv, fxq, fxk, fxv, fxqa, fxka = _mixer_in(x, shift(1), scale(1), gn(1), wts)
        ysb = _sb_attention(sbq, sbk, sbv)
        yfox = _fox_attention_v3(fxq, fxqa, fxk, fxka, fxv)
        x = _mixer_out(x, gate(1), ylru, ysb, yfox, g_mix_out[l, w:w + hw].reshape(1, hw),
                       g_mix_out[l, w + hw:].reshape(1, hw), w_out[l].astype(BF16))

        x = ffn(x, 2, 1)
    return x
```

```python
import functools

import jax
import jax.numpy as jnp
import numpy as np
from jax import lax
from jax.experimental import pallas as pl
from jax.experimental.pallas import tpu as pltpu

F32 = jnp.float32
BF16 = jnp.bfloat16

HEAD_DIM = 64
CONV_WIDTH = 4
LRU_C = 8.0
N_SUB = 3
EPS = 1e-6

V7X_LANES = 128
V7X_SUBLANES = 8
V7X_MXU_DIM = 256
V7X_VMEM_BYTES = 64 * 1024 * 1024
VMEM_LIMIT_BYTES = V7X_VMEM_BYTES - 8 * 1024 * 1024

HEADS_PER_PAIR = V7X_LANES // HEAD_DIM
AUG_PER_HEAD = 6
NEG_BIG = -1e30


def _params(*sem):
    return pltpu.CompilerParams(dimension_semantics=sem, vmem_limit_bytes=VMEM_LIMIT_BYTES)


def _sigmoid(x):
    return 1.0 / (1.0 + jnp.exp(-x))


def _log_sigmoid(x):
    return jnp.minimum(x, 0.0) - jnp.log(1.0 + jnp.exp(-jnp.abs(x)))


def _split2(v):
    hi = v.astype(BF16)
    lo = (v - hi.astype(F32)).astype(BF16)
    return hi, lo


def _split3(v):
    hi = v.astype(BF16).astype(F32)
    r = v - hi
    mid = r.astype(BF16).astype(F32)
    lo = (r - mid).astype(BF16).astype(F32)
    return hi, mid, lo


def _mod_norm(x, g, scale, shift):
    ms = jnp.mean(x * x, axis=-1, keepdims=True)
    return (x * lax.rsqrt(ms + EPS)) * (g * (1.0 + scale)) + shift


def _mod_kernel(c_ref, w_ref, b_ref, o_ref):
    c = c_ref[...]
    ca = c * _sigmoid(c)
    o_ref[0] = jnp.dot(ca.astype(BF16), w_ref[0].astype(BF16), preferred_element_type=F32) + b_ref[0]


def _adaln_mod(c, w_ada, b_ada):
    depth, d, n = w_ada.shape
    b = c.shape[0]
    tn = 1536
    assert n % tn == 0
    return pl.pallas_call(
        _mod_kernel,
        grid=(depth, n // tn),
        in_specs=[
            pl.BlockSpec((b, d), lambda l, j: (0, 0)),
            pl.BlockSpec((1, d, tn), lambda l, j: (l, 0, j)),
            pl.BlockSpec((1, 1, tn), lambda l, j: (l, 0, j)),
        ],
        out_specs=pl.BlockSpec((1, b, tn), lambda l, j: (l, 0, j)),
        out_shape=jax.ShapeDtypeStruct((depth, b, n), F32),
        compiler_params=_params("arbitrary", "arbitrary"),
        name="adaln_mod",
    )(c, w_ada, b_ada.reshape(depth, 1, n))


def _ffn_kernel(x_ref, shift_ref, scale_ref, gate_ref, g_ref, wg_ref, wu_ref, wd_ref, o_ref,
                h_scr, a_scr, acc_scr):
    n_chunks = wg_ref.shape[0]
    h_scr[...] = _mod_norm(x_ref[...], g_ref[...], scale_ref[0], shift_ref[0]).astype(BF16)

    def up(c):
        hb = h_scr[...]
        g = jnp.dot(hb, wg_ref[c], preferred_element_type=F32)
        u = jnp.dot(hb, wu_ref[c], preferred_element_type=F32)
        return ((g * _sigmoid(g)) * u).astype(BF16)

    def down(c):
        return jnp.dot(a_scr[...], wd_ref[c], preferred_element_type=F32)

    a_scr[...] = up(0)
    d = down(0)
    a_next = up(1)
    acc_scr[...] = d
    a_scr[...] = a_next

    def chunk(c, carry):
        d = down(c)
        a_next = up(c + 1)
        acc_scr[...] += d
        a_scr[...] = a_next
        return carry

    lax.fori_loop(1, n_chunks - 1, chunk, 0, unroll=3)
    o_ref[...] = x_ref[...] + (0.5 * (1.0 + gate_ref[0])) * (acc_scr[...] + down(n_chunks - 1))


def _ffn(x2, shift, scale, gate, g, wg, wu, wd, seq):
    m, d = x2.shape
    tm = 1024
    assert m % tm == 0 and seq % tm == 0
    per_seq = seq // tm
    vec = pl.BlockSpec((1, 1, d), lambda i: (i // per_seq, 0, 0))
    full = lambda a: pl.BlockSpec(a.shape, lambda i: (0,) * a.ndim)
    return pl.pallas_call(
        _ffn_kernel,
        grid=(m // tm,),
        in_specs=[pl.BlockSpec((tm, d), lambda i: (i, 0)), vec, vec, vec, full(g), full(wg), full(wu), full(wd)],
        out_specs=pl.BlockSpec((tm, d), lambda i: (i, 0)),
        out_shape=jax.ShapeDtypeStruct((m, d), F32),
        scratch_shapes=[
            pltpu.VMEM((tm, d), BF16),
            pltpu.VMEM((tm, wd.shape[1]), BF16),
            pltpu.VMEM((tm, d), F32),
        ],
        compiler_params=_params("arbitrary"),
        name="ffn",
    )(x2, shift, scale, gate, g, wg, wu, wd)


def _mixin_kernel(x_ref, shift_ref, scale_ref, g_ref, wlru_ref, wsb_ref, wfox_ref, wf_ref, bf_ref,
                  cw_ref, cb_ref, wgate_ref, bgate_ref, lam_ref, gq_ref, gk_ref, gmix_ref, hsum_ref,
                  pq_ref, cq_ref, pk_ref, ck_ref,
                  ylru_ref, sbq_ref, sbk_ref, sbv_ref, fxq_ref, fxk_ref, fxv_ref, fxqa_ref, fxka_ref,
                  xp_scr, a_scr, b_scr, hs_scr, hc_scr, fc_scr):
    ts = x_ref.shape[1]
    w = cw_ref.shape[1]
    n_heads = fxq_ref.shape[2] // HEAD_DIM
    pad = V7X_SUBLANES

    @pl.when(pl.program_id(1) == 0)
    def _():
        xp_scr[0:pad, :] = jnp.zeros((pad, w), F32)
        hc_scr[...] = jnp.zeros_like(hc_scr)
        fc_scr[...] = jnp.zeros_like(fc_scr)

    h = _mod_norm(x_ref[0], g_ref[...], scale_ref[0], shift_ref[0]).astype(BF16)

    p = jnp.dot(h, wlru_ref[...], preferred_element_type=F32)
    lru_g = p[:, w:]
    xp_scr[pad:pad + ts, :] = p[:, :w]
    u = cb_ref[...]
    for k in range(CONV_WIDTH):
        off = pad - (CONV_WIDTH - 1) + k
        u = u + cw_ref[k:k + 1, :] * xp_scr[off:off + ts, :]
    xp_scr[0:pad, :] = xp_scr[ts:ts + pad, :]

    ub = u.astype(BF16)

    def gate(which):
        n_grp, grp = wgate_ref.shape[1], wgate_ref.shape[2]
        pre = [jnp.dot(ub[:, g * grp:(g + 1) * grp], wgate_ref[which, g], preferred_element_type=F32)
               for g in range(n_grp)]
        return _sigmoid(jnp.concatenate(pre, axis=1) + bgate_ref[:, which * w:(which + 1) * w])

    r = gate(0)
    ig = gate(1)
    lam = lam_ref[...]
    softplus_neg_lam = jnp.maximum(-lam, 0.0) + jnp.log1p(jnp.exp(-jnp.abs(lam)))
    a = jnp.exp((-LRU_C * softplus_neg_lam) * r)
    a_scr[...] = a
    s = 1.0 - a * a
    b_scr[...] = jnp.where(s > 0.0, s * lax.rsqrt(s), 0.0) * (ig * u)

    row8 = lax.broadcasted_iota(jnp.int32, (V7X_SUBLANES, w), 0)

    def scan8(i, hprev):
        r0 = pl.multiple_of(i * V7X_SUBLANES, V7X_SUBLANES)
        aa = a_scr[pl.ds(r0, V7X_SUBLANES), :]
        bb = b_scr[pl.ds(r0, V7X_SUBLANES), :]
        for sh in (1, 2, 4):
            a_s = pltpu.roll(aa, sh, 0)
            b_s = pltpu.roll(bb, sh, 0)
            m = row8 >= sh
            bb = jnp.where(m, aa * b_s + bb, bb)
            aa = jnp.where(m, aa * a_s, aa)
        hh = aa * hprev + bb
        hs_scr[pl.ds(r0, V7X_SUBLANES), :] = hh
        return hh[V7X_SUBLANES - 1:V7X_SUBLANES, :]

    hc_scr[0:1, :] = lax.fori_loop(0, ts // V7X_SUBLANES, scan8, hc_scr[0:1, :], unroll=4)

    y = hs_scr[...] * jax.nn.gelu(lru_g)
    yn = y * lax.rsqrt(jnp.mean(y * y, axis=-1, keepdims=True) + EPS) * gmix_ref[...]
    ylru_ref[0] = yn.astype(BF16)

    scale = HEAD_DIM ** -0.5
    psb = jnp.dot(h, wsb_ref[...], preferred_element_type=F32)
    hw = n_heads * HEAD_DIM
    sbq_ref[0] = (psb[:, :hw] * scale).astype(BF16)
    sbk_ref[0] = psb[:, hw:2 * hw].astype(BF16)
    sbv_ref[0] = psb[:, 2 * hw:].astype(BF16)

    pfx = jnp.dot(h, wfox_ref[...], preferred_element_type=F32)

    def head_rms(t):
        ssq = jnp.dot((t * t).astype(BF16), hsum_ref[...], preferred_element_type=F32)
        return t * lax.rsqrt(ssq * (1.0 / HEAD_DIM) + EPS)

    fxq_ref[0] = (head_rms(pfx[:, :hw]) * (gq_ref[...] * scale)).astype(BF16)
    fxk_ref[0] = (head_rms(pfx[:, hw:2 * hw]) * gk_ref[...]).astype(BF16)
    fxv_ref[0] = pfx[:, 2 * hw:].astype(BF16)

    lane = lax.broadcasted_iota(jnp.int32, (ts, V7X_LANES), 1)
    logf = _log_sigmoid(jnp.dot(h, wf_ref[...], preferred_element_type=F32) + bf_ref[...])
    logf = jnp.where(lane < 3 * n_heads, logf, 0.0)
    hi, mid, lo = _split3(logf)
    pieces = jnp.where(lane < n_heads, hi, jnp.where(lane < 2 * n_heads, mid, lo)).astype(BF16)
    tri = (lax.broadcasted_iota(jnp.int32, (ts, ts), 0) >= lax.broadcasted_iota(jnp.int32, (ts, ts), 1))
    cs = jnp.dot(tri.astype(BF16), pieces, preferred_element_type=F32)
    cs = cs + pltpu.roll(cs, V7X_LANES - n_heads, 1) + pltpu.roll(cs, V7X_LANES - 2 * n_heads, 1)
    fcum = cs + fc_scr[0:1, :]
    fc_scr[0:1, :] = fcum[ts - 1:ts, :]

    f_pieces = jnp.concatenate(_split3(fcum), axis=1).astype(BF16)
    fxqa_ref[0] = (jnp.dot(f_pieces, pq_ref[...], preferred_element_type=F32) + cq_ref[...]).astype(BF16)
    fxka_ref[0] = (jnp.dot(f_pieces, pk_ref[...], preferred_element_type=F32) + ck_ref[...]).astype(BF16)


def _mixer_in(x, shift, scale, g, wts, seq_tile=512):
    b, s, d = x.shape
    ts = seq_tile
    assert s % ts == 0
    w = wts["cw"].shape[1]
    hw = wts["wsb"].shape[1] // 3
    n_pairs = hw // V7X_LANES
    vec = pl.BlockSpec((1, 1, d), lambda i, j: (i, 0, 0))
    full = lambda a: pl.BlockSpec(a.shape, lambda i, j: (0,) * a.ndim)
    names = ("wlru", "wsb", "wfox", "wf", "bf", "cw", "cb", "wgate", "bgate", "lam", "gq", "gk", "gmix", "hsum",
             "pq", "cq", "pk", "ck")
    consts = [wts[k] for k in names]
    seq_out = lambda width: pl.BlockSpec((1, ts, width), lambda i, j: (i, j, 0))
    widths = (w, hw, hw, hw, hw, hw, hw, n_pairs * V7X_LANES, n_pairs * V7X_LANES)
    return pl.pallas_call(
        _mixin_kernel,
        grid=(b, s // ts),
        in_specs=[pl.BlockSpec((1, ts, d), lambda i, j: (i, j, 0)), vec, vec, full(g)] + [full(a) for a in consts],
        out_specs=[seq_out(n) for n in widths],
        out_shape=[jax.ShapeDtypeStruct((b, s, n), BF16) for n in widths],
        scratch_shapes=[
            pltpu.VMEM((ts + V7X_SUBLANES, w), F32),
            pltpu.VMEM((ts, w), F32),
            pltpu.VMEM((ts, w), F32),
            pltpu.VMEM((ts, w), F32),
            pltpu.VMEM((V7X_SUBLANES, w), F32),
            pltpu.VMEM((V7X_SUBLANES, V7X_LANES), F32),
        ],
        compiler_params=_params("arbitrary", "arbitrary"),
        name="mixer_in",
    )(x, shift, scale, g, *consts)


def _head_lanes(h):
    pr, e = divmod(h, HEADS_PER_PAIR)
    return slice(pr * V7X_LANES, (pr + 1) * V7X_LANES), e * HEAD_DIM


def _neg_abs(x):
    bits = lax.bitcast_convert_type(x, jnp.uint32) | jnp.uint32(0x80000000)
    return lax.bitcast_convert_type(bits, F32)


def _sb_kernel(q_ref, k_ref, v_ref, o_ref, qm_scr, z_scr, w_scr, carry_scr, acc_scr):
    tq = q_ref.shape[1]
    tk = tq
    n_heads = q_ref.shape[2] // HEAD_DIM
    heads = range(n_heads)
    i = pl.program_id(1)
    lane = lax.broadcasted_iota(jnp.int32, (tq, V7X_LANES), 1)
    past = lax.broadcasted_iota(jnp.int32, (tq, tk), 1) < lax.broadcasted_iota(jnp.int32, (tq, tk), 0)
    strict = (lax.broadcasted_iota(jnp.int32, (tk, tk), 0) > lax.broadcasted_iota(jnp.int32, (tk, tk), 1)).astype(BF16)

    for h in heads:
        cols, lo_lane = _head_lanes(h)
        q_pair = q_ref[0, :, cols]
        qm_scr[h] = jnp.where((lane >= lo_lane) & (lane < lo_lane + HEAD_DIM), q_pair, jnp.zeros_like(q_pair))
    carry_scr[...] = jnp.zeros_like(carry_scr)
    acc_scr[...] = jnp.zeros_like(acc_scr)

    def score(h, j):
        k0 = pl.multiple_of(j * tk, tk)
        return lax.dot_general(qm_scr[h], k_ref[0, pl.ds(k0, tk), _head_lanes(h)[0]], (((1,), (1,)), ((), ())),
                               preferred_element_type=F32)

    def weighted_values(h, j):
        k0 = pl.multiple_of(j * tk, tk)
        v = v_ref[0, pl.ds(k0, tk), _head_lanes(h)[0]]
        acc_scr[h] += jnp.dot(w_scr[h], v, preferred_element_type=F32)

    def consume(masked, ahead, behind):
        log_beta, log_1mb, after = [], [], []
        for h in heads:
            z = z_scr[h]
            lb = jnp.minimum(z, 0.0) - jnp.log(1.0 + jnp.exp(_neg_abs(z)))
            log_beta.append(lb)
            log_1mb.append(jnp.where(past, lb - z, 0.0) if masked else lb - z)
            after.append(jnp.dot(log_1mb[h].astype(BF16), strict, preferred_element_type=F32))
            if behind is not None:
                weighted_values(h, behind)
            if ahead is not None:
                z_scr[h] = score(h, ahead)
        for h in heads:
            carry = carry_scr[h]
            w_h = jnp.exp(log_beta[h] + after[h] + jnp.concatenate([carry] * (tk // V7X_LANES), axis=1))
            w_scr[h] = (jnp.where(past, w_h, 0.0) if masked else w_h).astype(BF16)
            carry_scr[h] = carry + jnp.broadcast_to(after[h][:, 0:1] + log_1mb[h][:, 0:1], (tq, V7X_LANES))

    for h in heads:
        z_scr[h] = score(h, i)
    consume(True, jnp.maximum(i - 1, 0), None)

    def step(u, c):
        consume(False, i - u - 1, i - u + 1)
        return c

    lax.fori_loop(1, i, step, 0)

    @pl.when(i > 0)
    def _():
        consume(False, None, 1)

    for h in heads:
        weighted_values(h, 0)

    for h in range(0, n_heads, HEADS_PER_PAIR):
        cols, _ = _head_lanes(h)
        o_ref[0, :, cols] = jnp.where(lane < HEAD_DIM, acc_scr[h], acc_scr[h + 1])


def _sb_attention(q, k, v, tq=256):
    b, s, hw = q.shape
    n_heads = hw // HEAD_DIM
    assert s % tq == 0
    return pl.pallas_call(
        _sb_kernel,
        grid=(b, s // tq),
        in_specs=[
            pl.BlockSpec((1, tq, hw), lambda i, j: (i, j, 0)),
            pl.BlockSpec((1, s, hw), lambda i, j: (i, 0, 0)),
            pl.BlockSpec((1, s, hw), lambda i, j: (i, 0, 0)),
        ],
        out_specs=pl.BlockSpec((1, tq, hw), lambda i, j: (i, j, 0)),
        out_shape=jax.ShapeDtypeStruct((b, s, hw), F32),
        scratch_shapes=[
            pltpu.VMEM((n_heads, tq, V7X_LANES), BF16),
            pltpu.VMEM((n_heads, tq, tq), F32),
            pltpu.VMEM((n_heads, tq, tq), BF16),
            pltpu.VMEM((n_heads, tq, V7X_LANES), F32),
            pltpu.VMEM((n_heads, tq, V7X_LANES), F32),
        ],
        compiler_params=_params("arbitrary", "arbitrary"),
        name="sb_attention",
    )(q, k, v)


def _fox_kernel(q_ref, qa_ref, k_ref, ka_ref, v_ref, o_ref, qm_scr, z_scr, p_scr, mx_scr, alpha_scr, m_scr, l_scr,
                acc_scr, *, tk):
    tq = q_ref.shape[1]
    n_heads = q_ref.shape[2] // HEAD_DIM
    heads = range(n_heads)
    i = pl.program_id(1)
    lane = lax.broadcasted_iota(jnp.int32, (tq, V7X_LANES), 1)
    col_minus_row = lax.broadcasted_iota(jnp.int32, (tq, tk), 1) - lax.broadcasted_iota(jnp.int32, (tq, tk), 0)
    ones = jnp.ones((tk, V7X_LANES), BF16)

    for h in heads:
        cols, lo_lane = _head_lanes(h)
        e = h % HEADS_PER_PAIR
        q_pair = q_ref[0, :, cols]
        qa_pair = qa_ref[0, :, cols]
        in_head = (lane >= lo_lane) & (lane < lo_lane + HEAD_DIM)
        in_aug = (lane >= e * AUG_PER_HEAD) & (lane < (e + 1) * AUG_PER_HEAD)
        qm_scr[h] = jnp.concatenate([jnp.where(in_head, q_pair, jnp.zeros_like(q_pair)),
                                     jnp.where(in_aug, qa_pair, jnp.zeros_like(qa_pair))], axis=1)
    m_scr[...] = jnp.full_like(m_scr, NEG_BIG)
    l_scr[...] = jnp.zeros_like(l_scr)
    acc_scr[...] = jnp.zeros_like(acc_scr)

    def score(h, j):
        k0 = pl.multiple_of(j * tk, tk)
        cols, _ = _head_lanes(h)
        k = jnp.concatenate([k_ref[0, pl.ds(k0, tk), cols], ka_ref[0, pl.ds(k0, tk), cols]], axis=1)
        return lax.dot_general(qm_scr[h], k, (((1,), (1,)), ((), ())), preferred_element_type=F32)

    def stage_scores(h, j, limit):
        z = score(h, j)
        if limit is not None:
            z = jnp.where(col_minus_row <= limit, z, -jnp.inf)
        z_scr[h] = z
        mx_scr[h] = jnp.broadcast_to(jnp.max(z, axis=-1, keepdims=True), (tq, V7X_LANES))

    def weighted_values(h, j):
        k0 = pl.multiple_of(j * tk, tk)
        v = jnp.concatenate([v_ref[0, pl.ds(k0, tk), _head_lanes(h)[0]], ones], axis=1)
        pv = jnp.dot(p_scr[h], v, preferred_element_type=F32)
        alpha = alpha_scr[h]
        l_scr[h] = alpha * l_scr[h] + pv[:, V7X_LANES:]
        acc_scr[h] = alpha * acc_scr[h] + pv[:, :V7X_LANES]

    def step(ahead, ahead_limit, behind):
        for h in heads:
            if behind is not None:
                weighted_values(h, behind)
            m_prev = m_scr[h]
            m_new = jnp.maximum(m_prev, mx_scr[h])
            alpha_scr[h] = jnp.exp(m_prev - m_new)
            p_scr[h] = jnp.exp(z_scr[h] - jnp.concatenate([m_new] * (tk // V7X_LANES), axis=1)).astype(BF16)
            m_scr[h] = m_new
            if ahead is not None:
                stage_scores(h, ahead, ahead_limit)

    last = (i * tq) // tk
    diag_limit = i * tq - last * tk
    everything = tk

    for h in heads:
        stage_scores(h, 0, jnp.where(last == 0, diag_limit, everything))

    @pl.when(last > 0)
    def _():
        step(1, jnp.where(last == 1, diag_limit, everything), None)

    def middle(t, c):
        step(t + 1, None, t - 1)
        return c

    lax.fori_loop(1, last - 1, middle, 0)

    @pl.when(last > 1)
    def _():
        step(last, diag_limit, last - 2)

    @pl.when(last > 0)
    def _():
        step(None, None, last - 1)

    @pl.when(last == 0)
    def _():
        step(None, None, None)

    for h in heads:
        weighted_values(h, last)
    for h in range(0, n_heads, HEADS_PER_PAIR):
        cols, _ = _head_lanes(h)
        o_ref[0, :, cols] = jnp.where(lane < HEAD_DIM, acc_scr[h] / l_scr[h], acc_scr[h + 1] / l_scr[h + 1])


def _fox_attention(q, qa, k, ka, v, tq=256, tk=512):
    b, s, hw = q.shape
    n_heads = hw // HEAD_DIM
    assert s % tq == 0 and s % tk == 0 and tk % tq == 0
    qspec = pl.BlockSpec((1, tq, hw), lambda i, j: (i, j, 0))
    kspec = pl.BlockSpec((1, s, hw), lambda i, j: (i, 0, 0))
    return pl.pallas_call(
        functools.partial(_fox_kernel, tk=tk),
        grid=(b, s // tq),
        in_specs=[qspec, qspec, kspec, kspec, kspec],
        out_specs=pl.BlockSpec((1, tq, hw), lambda i, j: (i, j, 0)),
        out_shape=jax.ShapeDtypeStruct((b, s, hw), F32),
        scratch_shapes=[
            pltpu.VMEM((n_heads, tq, 2 * V7X_LANES), BF16),
            pltpu.VMEM((n_heads, tq, tk), F32),
            pltpu.VMEM((n_heads, tq, tk), BF16),
            pltpu.VMEM((n_heads, tq, V7X_LANES), F32),
            pltpu.VMEM((n_heads, tq, V7X_LANES), F32),
            pltpu.VMEM((n_heads, tq, V7X_LANES), F32),
            pltpu.VMEM((n_heads, tq, V7X_LANES), F32),
            pltpu.VMEM((n_heads, tq, V7X_LANES), F32),
        ],
        compiler_params=_params("arbitrary", "arbitrary"),
        name="fox_attention",
    )(q, qa, k, ka, v)


def _fox_kernel_v3(q_ref, qa_ref, k_ref, ka_ref, v_ref, o_ref, qm_scr, z_scr, m_scr, l_scr, acc_scr, *, tk):
    tq = q_ref.shape[1]
    n_heads = q_ref.shape[2] // HEAD_DIM
    heads = range(n_heads)
    i = pl.program_id(1)
    last = (i * tq) // tk
    lane = lax.broadcasted_iota(jnp.int32, (tq, V7X_LANES), 1)
    col_minus_row = lax.broadcasted_iota(jnp.int32, (tq, tk), 1) - lax.broadcasted_iota(jnp.int32, (tq, tk), 0)
    visible = col_minus_row <= i * tq - last * tk
    ones = jnp.ones((tk, V7X_LANES), BF16)

    for h in heads:
        cols, lo_lane = _head_lanes(h)
        e = h % HEADS_PER_PAIR
        q_pair = q_ref[0, :, cols]
        qa_pair = qa_ref[0, :, cols]
        in_head = (lane >= lo_lane) & (lane < lo_lane + HEAD_DIM)
        in_aug = (lane >= e * AUG_PER_HEAD) & (lane < (e + 1) * AUG_PER_HEAD)
        qm_scr[h] = jnp.concatenate([jnp.where(in_head, q_pair, jnp.zeros_like(q_pair)),
                                     jnp.where(in_aug, qa_pair, jnp.zeros_like(qa_pair))], axis=1)
    m_scr[...] = jnp.full_like(m_scr, NEG_BIG)
    l_scr[...] = jnp.zeros_like(l_scr)
    acc_scr[...] = jnp.zeros_like(acc_scr)

    def score(h, j):
        k0 = pl.multiple_of(j * tk, tk)
        cols, _ = _head_lanes(h)
        k = jnp.concatenate([k_ref[0, pl.ds(k0, tk), cols], ka_ref[0, pl.ds(k0, tk), cols]], axis=1)
        return lax.dot_general(qm_scr[h], k, (((1,), (1,)), ((), ())), preferred_element_type=F32)

    def consume(j, masked, ahead):
        k0 = pl.multiple_of(j * tk, tk)
        p, alpha = [], []
        for h in heads:
            z = jnp.where(visible, z_scr[h], -jnp.inf) if masked else z_scr[h]
            m_prev = m_scr[h]
            m_new = jnp.maximum(m_prev, jnp.max(z, axis=-1, keepdims=True))
            alpha.append(jnp.exp(m_prev - m_new))
            p.append(jnp.exp(z - jnp.concatenate([m_new] * (tk // V7X_LANES), axis=1)).astype(BF16))
            m_scr[h] = m_new
            if ahead is not None:
                z_scr[h] = score(h, ahead)
        for h in heads:
            v = jnp.concatenate([v_ref[0, pl.ds(k0, tk), _head_lanes(h)[0]], ones], axis=1)
            pv = jnp.dot(p[h], v, preferred_element_type=F32)
            l_scr[h] = alpha[h] * l_scr[h] + pv[:, V7X_LANES:]
            acc_scr[h] = alpha[h] * acc_scr[h] + pv[:, :V7X_LANES]

    for h in heads:
        z_scr[h] = score(h, 0)

    def step(t, c):
        consume(t, False, t + 1)
        return c

    lax.fori_loop(0, last, step, 0)
    consume(last, True, None)
    for h in range(0, n_heads, HEADS_PER_PAIR):
        cols, _ = _head_lanes(h)
        o_ref[0, :, cols] = jnp.where(lane < HEAD_DIM, acc_scr[h] / l_scr[h], acc_scr[h + 1] / l_scr[h + 1])


def _fox_attention_v3(q, qa, k, ka, v, tq=256, tk=512):
    b, s, hw = q.shape
    n_heads = hw // HEAD_DIM
    assert s % tq == 0 and s % tk == 0 and tk % tq == 0
    qspec = pl.BlockSpec((1, tq, hw), lambda i, j: (i, j, 0))
    kspec = pl.BlockSpec((1, s, hw), lambda i, j: (i, 0, 0))
    return pl.pallas_call(
        functools.partial(_fox_kernel_v3, tk=tk),
        grid=(b, s // tq),
        in_specs=[qspec, qspec, kspec, kspec, kspec],
        out_specs=pl.BlockSpec((1, tq, hw), lambda i, j: (i, j, 0)),
        out_shape=jax.ShapeDtypeStruct((b, s, hw), F32),
        scratch_shapes=[
            pltpu.VMEM((n_heads, tq, 2 * V7X_LANES), BF16),
            pltpu.VMEM((n_heads, tq, tk), F32),
            pltpu.VMEM((n_heads, tq, V7X_LANES), F32),
            pltpu.VMEM((n_heads, tq, V7X_LANES), F32),
            pltpu.VMEM((n_heads, tq, V7X_LANES), F32),
        ],
        compiler_params=_params("arbitrary", "arbitrary"),
        name="fox_attention",
    )(q, qa, k, ka, v)


def _mixout_kernel(x_ref, gate_ref, ylru_ref, ysb_ref, yfox_ref, gsb_ref, gfox_ref, wout_ref, o_ref):
    def group_norm(y, g):
        return (y * lax.rsqrt(jnp.mean(y * y, axis=-1, keepdims=True) + EPS) * g).astype(BF16)

    y = jnp.concatenate([ylru_ref[0], group_norm(ysb_ref[0], gsb_ref[...]), group_norm(yfox_ref[0], gfox_ref[...])],
                        axis=1)
    o_ref[0] = x_ref[0] + (1.0 + gate_ref[0]) * jnp.dot(y, wout_ref[...], preferred_element_type=F32)


def _mixer_out(x, gate, ylru, ysb, yfox, gsb, gfox, wout, seq_tile=1024):
    b, s, d = x.shape
    ts = seq_tile
    assert s % ts == 0
    seq = lambda a: pl.BlockSpec((1, ts, a.shape[2]), lambda i, j: (i, j, 0))
    full = lambda a: pl.BlockSpec(a.shape, lambda i, j: (0,) * a.ndim)
    return pl.pallas_call(
        _mixout_kernel,
        grid=(b, s // ts),
        in_specs=[seq(x), pl.BlockSpec((1, 1, d), lambda i, j: (i, 0, 0)), seq(ylru), seq(ysb), seq(yfox),
                  full(gsb), full(gfox), full(wout)],
        out_specs=seq(x),
        out_shape=jax.ShapeDtypeStruct((b, s, d), F32),
        compiler_params=_params("arbitrary", "arbitrary"),
        name="mixer_out",
    )(x, gate, ylru, ysb, yfox, gsb, gfox, wout)


def _block_diag(w):
    h, di, dj = w.shape
    return jnp.einsum("hij,hg->higj", w, jnp.eye(h, dtype=w.dtype)).reshape(h * di, h * dj)


def _diag_groups(bd):
    n = bd.shape[0]
    assert n % V7X_MXU_DIM == 0 and V7X_MXU_DIM % HEAD_DIM == 0
    n_grp = n // V7X_MXU_DIM
    blocks = bd.reshape(n_grp, V7X_MXU_DIM, n_grp, V7X_MXU_DIM)
    return jnp.stack([blocks[g, :, g, :] for g in range(n_grp)])


def _aug_placement(n_heads):
    n_out = (n_heads // HEADS_PER_PAIR) * V7X_LANES
    pq = np.zeros((3 * V7X_LANES, n_out), np.float32)
    pk = np.zeros((3 * V7X_LANES, n_out), np.float32)
    cq = np.zeros((1, n_out), np.float32)
    ck = np.zeros((1, n_out), np.float32)
    for h in range(n_heads):
        pr, e = divmod(h, HEADS_PER_PAIR)
        base = pr * V7X_LANES + e * AUG_PER_HEAD
        for j in range(3):
            pq[j * V7X_LANES + h, base + j] = 1.0
            cq[0, base + 3 + j] = 1.0
            ck[0, base + j] = 1.0
            pk[j * V7X_LANES + h, base + 3 + j] = -1.0
    return {"pq": jnp.asarray(pq, BF16), "cq": jnp.asarray(cq), "pk": jnp.asarray(pk, BF16), "ck": jnp.asarray(ck)}


def _ffn_weights(w_up, w_down, tf=256):
    d, two_ff = w_up.shape
    ff = two_ff // 2
    assert ff % tf == 0
    nc = ff // tf
    wg = w_up[:, :ff].reshape(d, nc, tf).transpose(1, 0, 2).astype(BF16)
    wu = w_up[:, ff:].reshape(d, nc, tf).transpose(1, 0, 2).astype(BF16)
    wd = w_down.reshape(nc, tf, d).astype(BF16)
    return wg, wu, wd


def _mixer_weights(w_in, b_fgate, conv_w, conv_b, w_rgate, b_rgate, w_igate, b_igate, lru_lambda, g_qk, g_mix_out):
    w = conv_w.shape[1]
    n_heads = b_fgate.shape[0]
    hw = n_heads * HEAD_DIM
    o_sb = 2 * w
    o_fx = o_sb + 3 * hw
    o_f = o_fx + 3 * hw
    wf = jnp.zeros((w_in.shape[0], V7X_LANES), F32).at[:, :3 * n_heads].set(jnp.tile(w_in[:, o_f:o_f + n_heads], (1, 3)))
    bf = jnp.zeros((1, V7X_LANES), F32).at[0, :3 * n_heads].set(jnp.tile(b_fgate, 3))
    head_of = jnp.arange(hw) // HEAD_DIM
    hsum = (head_of[:, None] == head_of[None, :]).astype(BF16)
    return {
        "wlru": w_in[:, :o_sb].astype(BF16),
        "wsb": w_in[:, o_sb:o_fx].astype(BF16),
        "wfox": w_in[:, o_fx:o_f].astype(BF16),
        "wf": wf.astype(BF16),
        "bf": bf,
        "cw": conv_w,
        "cb": conv_b.reshape(1, w),
        "wgate": jnp.stack([_diag_groups(_block_diag(w_rgate)), _diag_groups(_block_diag(w_igate))]).astype(BF16),
        "bgate": jnp.concatenate([b_rgate, b_igate]).reshape(1, 2 * w),
        "lam": lru_lambda.reshape(1, w),
        "gq": jnp.tile(g_qk[0], n_heads).reshape(1, hw),
        "gk": jnp.tile(g_qk[1], n_heads).reshape(1, hw),
        "gmix": g_mix_out[:w].reshape(1, w),
        "hsum": hsum,
        **_aug_placement(n_heads),
    }


def kernel(x, c, w_ada, b_ada, g_norm, w_ffn_up, w_ffn_down, w_in, b_fgate, conv_w, conv_b, w_rgate, b_rgate,
           w_igate, b_igate, lru_lambda, g_qk, g_mix_out, w_out):
    b, s, d = x.shape
    depth = w_ada.shape[0]
    w = conv_w.shape[-1]
    hw = b_fgate.shape[-1] * HEAD_DIM

    mod = _adaln_mod(c, w_ada, b_ada).reshape(depth, b, N_SUB, 3, 1, d)

    for l in range(depth):
        shift = lambda j: mod[l, :, j, 0]
        scale = lambda j: mod[l, :, j, 1]
        gate = lambda j: mod[l, :, j, 2]
        gn = lambda j: g_norm[l, j].reshape(1, d)

        def ffn(xx, j, i):
            wg, wu, wd = _ffn_weights(w_ffn_up[l, i], w_ffn_down[l, i])
            return _ffn(xx.reshape(b * s, d), shift(j), scale(j), gate(j), gn(j), wg, wu, wd, s).reshape(b, s, d)

        x = ffn(x, 0, 0)

        wts = _mixer_weights(w_in[l], b_fgate[l], conv_w[l], conv_b[l], w_rgate[l], b_rgate[l], w_igate[l],
                             b_igate[l], lru_lambda[l], g_qk[l], g_mix_out[l])
        ylru, sbq, sbk, sbv, fxq, fxk, fxv, fxqa, fxka = _mixer_in(x, shift(1), scale(1), gn(1), wts)
        ysb = _sb_attention(sbq, sbk, sbv)
        yfox = _fox_attention_v3(fxq, fxqa, fxk, fxka, fxv)
        x = _mixer_out(x, gate(1), ylru, ysb, yfox, g_mix_out[l, w:w + hw].reshape(1, hw),
                       g_mix_out[l, w + hw:].reshape(1, hw), w_out[l].astype(BF16))

        x = ffn(x, 2, 1)
    return x
```

```python
import functools

import jax
import jax.numpy as jnp
import numpy as np
from jax import lax
from jax.experimental import pallas as pl
from jax.experimental.pallas import tpu as pltpu

F32 = jnp.float32
BF16 = jnp.bfloat16

HEAD_DIM = 64
CONV_WIDTH = 4
LRU_C = 8.0
N_SUB = 3
EPS = 1e-6

V7X_LANES = 128
V7X_SUBLANES = 8
V7X_MXU_DIM = 256
V7X_VMEM_BYTES = 64 * 1024 * 1024
VMEM_LIMIT_BYTES = V7X_VMEM_BYTES - 8 * 1024 * 1024

HEADS_PER_PAIR = V7X_LANES // HEAD_DIM
AUG_PER_HEAD = 6
NEG_BIG = -1e30


def _params(*sem):
    return pltpu.CompilerParams(dimension_semantics=sem, vmem_limit_bytes=VMEM_LIMIT_BYTES)


def _sigmoid(x):
    return 1.0 / (1.0 + jnp.exp(-x))


def _log_sigmoid(x):
    return jnp.minimum(x, 0.0) - jnp.log(1.0 + jnp.exp(-jnp.abs(x)))


def _split2(v):
    hi = v.astype(BF16)
    lo = (v - hi.astype(F32)).astype(BF16)
    return hi, lo


def _split3(v):
    hi = v.astype(BF16).astype(F32)
    r = v - hi
    mid = r.astype(BF16).astype(F32)
    lo = (r - mid).astype(BF16).astype(F32)
    return hi, mid, lo


def _mod_norm(x, g, scale, shift):
    ms = jnp.mean(x * x, axis=-1, keepdims=True)
    return (x * lax.rsqrt(ms + EPS)) * (g * (1.0 + scale)) + shift


def _mod_kernel(c_ref, w_ref, b_ref, o_ref):
    c = c_ref[...]
    ca = c * _sigmoid(c)
    o_ref[0] = jnp.dot(ca.astype(BF16), w_ref[0].astype(BF16), preferred_element_type=F32) + b_ref[0]


def _adaln_mod(c, w_ada, b_ada):
    depth, d, n = w_ada.shape
    b = c.shape[0]
    tn = 1536
    assert n % tn == 0
    return pl.pallas_call(
        _mod_kernel,
        grid=(depth, n // tn),
        in_specs=[
            pl.BlockSpec((b, d), lambda l, j: (0, 0)),
            pl.BlockSpec((1, d, tn), lambda l, j: (l, 0, j)),
            pl.BlockSpec((1, 1, tn), lambda l, j: (l, 0, j)),
        ],
        out_specs=pl.BlockSpec((1, b, tn), lambda l, j: (l, 0, j)),
        out_shape=jax.ShapeDtypeStruct((depth, b, n), F32),
        compiler_params=_params("arbitrary", "arbitrary"),
        name="adaln_mod",
    )(c, w_ada, b_ada.reshape(depth, 1, n))


def _ffn_kernel(x_ref, shift_ref, scale_ref, gate_ref, g_ref, wup_ref, wd_ref, o_ref,
                h_scr, a_scr, acc_scr):
    ff = wd_ref.shape[0]
    tf = a_scr.shape[1]
    n_chunks = ff // tf
    h_scr[...] = _mod_norm(x_ref[...], g_ref[...], scale_ref[0], shift_ref[0]).astype(BF16)

    def up(c):
        hb = h_scr[...]
        off = pl.multiple_of(c * tf, tf)
        g = jnp.dot(hb, wup_ref[:, pl.ds(off, tf)], preferred_element_type=F32)
        u = jnp.dot(hb, wup_ref[:, pl.ds(off + ff, tf)], preferred_element_type=F32)
        return ((g * _sigmoid(g)) * u).astype(BF16)

    def down(c):
        return jnp.dot(a_scr[...], wd_ref[pl.ds(pl.multiple_of(c * tf, tf), tf), :], preferred_element_type=F32)

    a_scr[...] = up(0)
    d = down(0)
    a_next = up(1)
    acc_scr[...] = d
    a_scr[...] = a_next

    def chunk(c, carry):
        d = down(c)
        a_next = up(c + 1)
        acc_scr[...] += d
        a_scr[...] = a_next
        return carry

    lax.fori_loop(1, n_chunks - 1, chunk, 0, unroll=3)
    o_ref[...] = x_ref[...] + (0.5 * (1.0 + gate_ref[0])) * (acc_scr[...] + down(n_chunks - 1))


def _ffn(x2, shift, scale, gate, g, wup, wd, seq, tf=256):
    m, d = x2.shape
    tm = 1024
    assert m % tm == 0 and seq % tm == 0 and wd.shape[0] % tf == 0 and wup.shape[1] == 2 * wd.shape[0]
    per_seq = seq // tm
    vec = pl.BlockSpec((1, 1, d), lambda i: (i // per_seq, 0, 0))
    full = lambda a: pl.BlockSpec(a.shape, lambda i: (0,) * a.ndim)
    return pl.pallas_call(
        _ffn_kernel,
        grid=(m // tm,),
        in_specs=[pl.BlockSpec((tm, d), lambda i: (i, 0)), vec, vec, vec, full(g), full(wup), full(wd)],
        out_specs=pl.BlockSpec((tm, d), lambda i: (i, 0)),
        out_shape=jax.ShapeDtypeStruct((m, d), F32),
        scratch_shapes=[
            pltpu.VMEM((tm, d), BF16),
            pltpu.VMEM((tm, tf), BF16),
            pltpu.VMEM((tm, d), F32),
        ],
        compiler_params=_params("arbitrary"),
        name="ffn",
    )(x2, shift, scale, gate, g, wup, wd)


def _mixin_kernel(x_ref, shift_ref, scale_ref, g_ref, wlru_ref, wsb_ref, wfox_ref, wf_ref, bf_ref,
                  cw_ref, cb_ref, wgate_ref, bgate_ref, lam_ref, gq_ref, gk_ref, gmix_ref, hsum_ref,
                  pq_ref, cq_ref, pk_ref, ck_ref,
                  ylru_ref, sbq_ref, sbk_ref, sbv_ref, fxq_ref, fxk_ref, fxv_ref, fxqa_ref, fxka_ref,
                  xp_scr, a_scr, b_scr, hs_scr, hc_scr, fc_scr):
    ts = x_ref.shape[1]
    w = cw_ref.shape[1]
    n_heads = fxq_ref.shape[2] // HEAD_DIM
    pad = V7X_SUBLANES

    @pl.when(pl.program_id(1) == 0)
    def _():
        xp_scr[0:pad, :] = jnp.zeros((pad, w), F32)
        hc_scr[...] = jnp.zeros_like(hc_scr)
        fc_scr[...] = jnp.zeros_like(fc_scr)

    h = _mod_norm(x_ref[0], g_ref[...], scale_ref[0], shift_ref[0]).astype(BF16)

    p = jnp.dot(h, wlru_ref[...], preferred_element_type=F32)
    lru_g = p[:, w:]
    xp_scr[pad:pad + ts, :] = p[:, :w]
    u = cb_ref[...]
    for k in range(CONV_WIDTH):
        off = pad - (CONV_WIDTH - 1) + k
        u = u + cw_ref[k:k + 1, :] * xp_scr[off:off + ts, :]
    xp_scr[0:pad, :] = xp_scr[ts:ts + pad, :]

    ub = u.astype(BF16)

    def gate(which):
        n_grp, grp = wgate_ref.shape[1], wgate_ref.shape[2]
        pre = [jnp.dot(ub[:, g * grp:(g + 1) * grp], wgate_ref[which, g], preferred_element_type=F32)
               for g in range(n_grp)]
        return _sigmoid(jnp.concatenate(pre, axis=1) + bgate_ref[:, which * w:(which + 1) * w])

    r = gate(0)
    ig = gate(1)
    lam = lam_ref[...]
    softplus_neg_lam = jnp.maximum(-lam, 0.0) + jnp.log1p(jnp.exp(-jnp.abs(lam)))
    a = jnp.exp((-LRU_C * softplus_neg_lam) * r)
    a_scr[...] = a
    s = 1.0 - a * a
    b_scr[...] = jnp.where(s > 0.0, s * lax.rsqrt(s), 0.0) * (ig * u)

    row8 = lax.broadcasted_iota(jnp.int32, (V7X_SUBLANES, w), 0)

    def scan8(i, hprev):
        r0 = pl.multiple_of(i * V7X_SUBLANES, V7X_SUBLANES)
        aa = a_scr[pl.ds(r0, V7X_SUBLANES), :]
        bb = b_scr[pl.ds(r0, V7X_SUBLANES), :]
        for sh in (1, 2, 4):
            a_s = pltpu.roll(aa, sh, 0)
            b_s = pltpu.roll(bb, sh, 0)
            m = row8 >= sh
            bb = jnp.where(m, aa * b_s + bb, bb)
            aa = jnp.where(m, aa * a_s, aa)
        hh = aa * hprev + bb
        hs_scr[pl.ds(r0, V7X_SUBLANES), :] = hh
        return hh[V7X_SUBLANES - 1:V7X_SUBLANES, :]

    hc_scr[0:1, :] = lax.fori_loop(0, ts // V7X_SUBLANES, scan8, hc_scr[0:1, :], unroll=True)

    y = hs_scr[...] * jax.nn.gelu(lru_g)
    yn = y * lax.rsqrt(jnp.mean(y * y, axis=-1, keepdims=True) + EPS) * gmix_ref[...]
    ylru_ref[0] = yn.astype(BF16)

    scale = HEAD_DIM ** -0.5
    psb = jnp.dot(h, wsb_ref[...], preferred_element_type=F32)
    hw = n_heads * HEAD_DIM
    sbq_ref[0] = (psb[:, :hw] * scale).astype(BF16)
    sbk_ref[0] = psb[:, hw:2 * hw].astype(BF16)
    sbv_ref[0] = psb[:, 2 * hw:].astype(BF16)

    pfx = jnp.dot(h, wfox_ref[...], preferred_element_type=F32)

    def head_rms(t):
        ssq = jnp.dot((t * t).astype(BF16), hsum_ref[...], preferred_element_type=F32)
        return t * lax.rsqrt(ssq * (1.0 / HEAD_DIM) + EPS)

    fxq_ref[0] = (head_rms(pfx[:, :hw]) * (gq_ref[...] * scale)).astype(BF16)
    fxk_ref[0] = (head_rms(pfx[:, hw:2 * hw]) * gk_ref[...]).astype(BF16)
    fxv_ref[0] = pfx[:, 2 * hw:].astype(BF16)

    lane = lax.broadcasted_iota(jnp.int32, (ts, V7X_LANES), 1)
    logf = _log_sigmoid(jnp.dot(h, wf_ref[...], preferred_element_type=F32) + bf_ref[...])
    logf = jnp.where(lane < 3 * n_heads, logf, 0.0)
    hi, mid, lo = _split3(logf)
    pieces = jnp.where(lane < n_heads, hi, jnp.where(lane < 2 * n_heads, mid, lo)).astype(BF16)
    tri = (lax.broadcasted_iota(jnp.int32, (ts, ts), 0) >= lax.broadcasted_iota(jnp.int32, (ts, ts), 1))
    cs = jnp.dot(tri.astype(BF16), pieces, preferred_element_type=F32)
    cs = cs + pltpu.roll(cs, V7X_LANES - n_heads, 1) + pltpu.roll(cs, V7X_LANES - 2 * n_heads, 1)
    fcum = cs + fc_scr[0:1, :]
    fc_scr[0:1, :] = fcum[ts - 1:ts, :]

    f_pieces = jnp.concatenate(_split3(fcum), axis=1).astype(BF16)
    fxqa_ref[0] = (jnp.dot(f_pieces, pq_ref[...], preferred_element_type=F32) + cq_ref[...]).astype(BF16)
    fxka_ref[0] = (jnp.dot(f_pieces, pk_ref[...], preferred_element_type=F32) + ck_ref[...]).astype(BF16)


def _mixer_in(x, shift, scale, g, wts, seq_tile=512):
    b, s, d = x.shape
    ts = seq_tile
    assert s % ts == 0
    w = wts["cw"].shape[1]
    hw = wts["wsb"].shape[1] // 3
    n_pairs = hw // V7X_LANES
    vec = pl.BlockSpec((1, 1, d), lambda i, j: (i, 0, 0))
    full = lambda a: pl.BlockSpec(a.shape, lambda i, j: (0,) * a.ndim)
    names = ("wlru", "wsb", "wfox", "wf", "bf", "cw", "cb", "wgate", "bgate", "lam", "gq", "gk", "gmix", "hsum",
             "pq", "cq", "pk", "ck")
    consts = [wts[k] for k in names]
    seq_out = lambda width: pl.BlockSpec((1, ts, width), lambda i, j: (i, j, 0))
    widths = (w, hw, hw, hw, hw, hw, hw, n_pairs * V7X_LANES, n_pairs * V7X_LANES)
    return pl.pallas_call(
        _mixin_kernel,
        grid=(b, s // ts),
        in_specs=[pl.BlockSpec((1, ts, d), lambda i, j: (i, j, 0)), vec, vec, full(g)] + [full(a) for a in consts],
        out_specs=[seq_out(n) for n in widths],
        out_shape=[jax.ShapeDtypeStruct((b, s, n), BF16) for n in widths],
        scratch_shapes=[
            pltpu.VMEM((ts + V7X_SUBLANES, w), F32),
            pltpu.VMEM((ts, w), F32),
            pltpu.VMEM((ts, w), F32),
            pltpu.VMEM((ts, w), F32),
            pltpu.VMEM((V7X_SUBLANES, w), F32),
            pltpu.VMEM((V7X_SUBLANES, V7X_LANES), F32),
        ],
        compiler_params=_params("arbitrary", "arbitrary"),
        name="mixer_in",
    )(x, shift, scale, g, *consts)


def _head_lanes(h):
    pr, e = divmod(h, HEADS_PER_PAIR)
    return slice(pr * V7X_LANES, (pr + 1) * V7X_LANES), e * HEAD_DIM


def _neg_abs(x):
    bits = lax.bitcast_convert_type(x, jnp.uint32) | jnp.uint32(0x80000000)
    return lax.bitcast_convert_type(bits, F32)


def _sb_kernel(q_ref, k_ref, v_ref, o_ref, qm_scr, z_scr, w_scr, carry_scr, acc_scr):
    tq = q_ref.shape[1]
    tk = tq
    n_heads = q_ref.shape[2] // HEAD_DIM
    heads = range(n_heads)
    i = pl.program_id(1)
    lane = lax.broadcasted_iota(jnp.int32, (tq, V7X_LANES), 1)
    past = lax.broadcasted_iota(jnp.int32, (tq, tk), 1) < lax.broadcasted_iota(jnp.int32, (tq, tk), 0)
    strict = (lax.broadcasted_iota(jnp.int32, (tk, tk), 0) > lax.broadcasted_iota(jnp.int32, (tk, tk), 1)).astype(BF16)

    for h in heads:
        cols, lo_lane = _head_lanes(h)
        q_pair = q_ref[0, :, cols]
        qm_scr[h] = jnp.where((lane >= lo_lane) & (lane < lo_lane + HEAD_DIM), q_pair, jnp.zeros_like(q_pair))
    carry_scr[...] = jnp.zeros_like(carry_scr)
    acc_scr[...] = jnp.zeros_like(acc_scr)

    def score(h, j):
        k0 = pl.multiple_of(j * tk, tk)
        return lax.dot_general(qm_scr[h], k_ref[0, pl.ds(k0, tk), _head_lanes(h)[0]], (((1,), (1,)), ((), ())),
                               preferred_element_type=F32)

    def weighted_values(h, j):
        k0 = pl.multiple_of(j * tk, tk)
        v = v_ref[0, pl.ds(k0, tk), _head_lanes(h)[0]]
        acc_scr[h] += jnp.dot(w_scr[h], v, preferred_element_type=F32)

    def consume(masked, ahead, behind):
        log_beta, log_1mb, after = [], [], []
        for h in heads:
            z = z_scr[h]
            lb = jnp.minimum(z, 0.0) - jnp.log(1.0 + jnp.exp(_neg_abs(z)))
            log_beta.append(lb)
            log_1mb.append(jnp.where(past, lb - z, 0.0) if masked else lb - z)
            after.append(jnp.dot(log_1mb[h].astype(BF16), strict, preferred_element_type=F32))
            if behind is not None:
                weighted_values(h, behind)
            if ahead is not None:
                z_scr[h] = score(h, ahead)
        for h in heads:
            carry = carry_scr[h]
            w_h = jnp.exp(log_beta[h] + after[h] + jnp.concatenate([carry] * (tk // V7X_LANES), axis=1))
            w_scr[h] = (jnp.where(past, w_h, 0.0) if masked else w_h).astype(BF16)
            carry_scr[h] = carry + jnp.broadcast_to(after[h][:, 0:1] + log_1mb[h][:, 0:1], (tq, V7X_LANES))

    for h in heads:
        z_scr[h] = score(h, i)
    consume(True, jnp.maximum(i - 1, 0), None)

    def step(u, c):
        consume(False, i - u - 1, i - u + 1)
        return c

    lax.fori_loop(1, i, step, 0)

    @pl.when(i > 0)
    def _():
        consume(False, None, 1)

    for h in heads:
        weighted_values(h, 0)

    for h in range(0, n_heads, HEADS_PER_PAIR):
        cols, _ = _head_lanes(h)
        o_ref[0, :, cols] = jnp.where(lane < HEAD_DIM, acc_scr[h], acc_scr[h + 1])


def _sb_attention(q, k, v, tq=256):
    b, s, hw = q.shape
    n_heads = hw // HEAD_DIM
    assert s % tq == 0
    return pl.pallas_call(
        _sb_kernel,
        grid=(b, s // tq),
        in_specs=[
            pl.BlockSpec((1, tq, hw), lambda i, j: (i, j, 0)),
            pl.BlockSpec((1, s, hw), lambda i, j: (i, 0, 0)),
            pl.BlockSpec((1, s, hw), lambda i, j: (i, 0, 0)),
        ],
        out_specs=pl.BlockSpec((1, tq, hw), lambda i, j: (i, j, 0)),
        out_shape=jax.ShapeDtypeStruct((b, s, hw), F32),
        scratch_shapes=[
            pltpu.VMEM((n_heads, tq, V7X_LANES), BF16),
            pltpu.VMEM((n_heads, tq, tq), F32),
            pltpu.VMEM((n_heads, tq, tq), BF16),
            pltpu.VMEM((n_heads, tq, V7X_LANES), F32),
            pltpu.VMEM((n_heads, tq, V7X_LANES), F32),
        ],
        compiler_params=_params("arbitrary", "arbitrary"),
        name="sb_attention",
    )(q, k, v)


def _fox_kernel(q_ref, qa_ref, k_ref, ka_ref, v_ref, o_ref, qm_scr, z_scr, m_scr, l_scr, acc_scr, *, tk):
    tq = q_ref.shape[1]
    n_heads = q_ref.shape[2] // HEAD_DIM
    heads = range(n_heads)
    i = pl.program_id(1)
    last = (i * tq) // tk
    lane = lax.broadcasted_iota(jnp.int32, (tq, V7X_LANES), 1)
    col_minus_row = lax.broadcasted_iota(jnp.int32, (tq, tk), 1) - lax.broadcasted_iota(jnp.int32, (tq, tk), 0)
    visible = col_minus_row <= i * tq - last * tk
    ones = jnp.ones((tk, V7X_LANES), BF16)

    for h in heads:
        cols, lo_lane = _head_lanes(h)
        e = h % HEADS_PER_PAIR
        q_pair = q_ref[0, :, cols]
        qa_pair = qa_ref[0, :, cols]
        in_head = (lane >= lo_lane) & (lane < lo_lane + HEAD_DIM)
        in_aug = (lane >= e * AUG_PER_HEAD) & (lane < (e + 1) * AUG_PER_HEAD)
        qm_scr[h] = jnp.concatenate([jnp.where(in_head, q_pair, jnp.zeros_like(q_pair)),
                                     jnp.where(in_aug, qa_pair, jnp.zeros_like(qa_pair))], axis=1)
    m_scr[...] = jnp.full_like(m_scr, NEG_BIG)
    l_scr[...] = jnp.zeros_like(l_scr)
    acc_scr[...] = jnp.zeros_like(acc_scr)

    def score(h, j):
        k0 = pl.multiple_of(j * tk, tk)
        cols, _ = _head_lanes(h)
        k = jnp.concatenate([k_ref[0, pl.ds(k0, tk), cols], ka_ref[0, pl.ds(k0, tk), cols]], axis=1)
        return lax.dot_general(qm_scr[h], k, (((1,), (1,)), ((), ())), preferred_element_type=F32)

    def consume(j, masked, ahead):
        k0 = pl.multiple_of(j * tk, tk)
        p, alpha = [], []
        for h in heads:
            z = jnp.where(visible, z_scr[h], -jnp.inf) if masked else z_scr[h]
            m_prev = m_scr[h]
            m_new = jnp.maximum(m_prev, jnp.max(z, axis=-1, keepdims=True))
            alpha.append(jnp.exp(m_prev - m_new))
            p.append(jnp.exp(z - jnp.concatenate([m_new] * (tk // V7X_LANES), axis=1)).astype(BF16))
            m_scr[h] = m_new
            if ahead is not None:
                z_scr[h] = score(h, ahead)
        for h in heads:
            v = jnp.concatenate([v_ref[0, pl.ds(k0, tk), _head_lanes(h)[0]], ones], axis=1)
            pv = jnp.dot(p[h], v, preferred_element_type=F32)
            l_scr[h] = alpha[h] * l_scr[h] + pv[:, V7X_LANES:]
            acc_scr[h] = alpha[h] * acc_scr[h] + pv[:, :V7X_LANES]

    for h in heads:
        z_scr[h] = score(h, 0)

    def step(t, c):
        consume(t, False, t + 1)
        return c

    lax.fori_loop(0, last, step, 0)
    consume(last, True, None)
    for h in range(0, n_heads, HEADS_PER_PAIR):
        cols, _ = _head_lanes(h)
        o_ref[0, :, cols] = jnp.where(lane < HEAD_DIM, acc_scr[h] / l_scr[h], acc_scr[h + 1] / l_scr[h + 1])


def _fox_attention(q, qa, k, ka, v, tq=256, tk=512):
    b, s, hw = q.shape
    n_heads = hw // HEAD_DIM
    assert s % tq == 0 and s % tk == 0 and tk % tq == 0
    qspec = pl.BlockSpec((1, tq, hw), lambda i, j: (i, j, 0))
    kspec = pl.BlockSpec((1, s, hw), lambda i, j: (i, 0, 0))
    return pl.pallas_call(
        functools.partial(_fox_kernel, tk=tk),
        grid=(b, s // tq),
        in_specs=[qspec, qspec, kspec, kspec, kspec],
        out_specs=pl.BlockSpec((1, tq, hw), lambda i, j: (i, j, 0)),
        out_shape=jax.ShapeDtypeStruct((b, s, hw), F32),
        scratch_shapes=[
            pltpu.VMEM((n_heads, tq, 2 * V7X_LANES), BF16),
            pltpu.VMEM((n_heads, tq, tk), F32),
            pltpu.VMEM((n_heads, tq, V7X_LANES), F32),
            pltpu.VMEM((n_heads, tq, V7X_LANES), F32),
            pltpu.VMEM((n_heads, tq, V7X_LANES), F32),
        ],
        compiler_params=_params("arbitrary", "arbitrary"),
        name="fox_attention",
    )(q, qa, k, ka, v)


def _mixout_kernel(x_ref, gate_ref, ylru_ref, ysb_ref, yfox_ref, gsb_ref, gfox_ref, wout_ref, o_ref):
    def group_norm(y, g):
        return (y * lax.rsqrt(jnp.mean(y * y, axis=-1, keepdims=True) + EPS) * g).astype(BF16)

    y = jnp.concatenate([ylru_ref[0], group_norm(ysb_ref[0], gsb_ref[...]), group_norm(yfox_ref[0], gfox_ref[...])],
                        axis=1)
    o_ref[0] = x_ref[0] + (1.0 + gate_ref[0]) * jnp.dot(y, wout_ref[...], preferred_element_type=F32)


def _mixer_out(x, gate, ylru, ysb, yfox, gsb, gfox, wout, seq_tile=1024):
    b, s, d = x.shape
    ts = seq_tile
    assert s % ts == 0
    seq = lambda a: pl.BlockSpec((1, ts, a.shape[2]), lambda i, j: (i, j, 0))
    full = lambda a: pl.BlockSpec(a.shape, lambda i, j: (0,) * a.ndim)
    return pl.pallas_call(
        _mixout_kernel,
        grid=(b, s // ts),
        in_specs=[seq(x), pl.BlockSpec((1, 1, d), lambda i, j: (i, 0, 0)), seq(ylru), seq(ysb), seq(yfox),
                  full(gsb), full(gfox), full(wout)],
        out_specs=seq(x),
        out_shape=jax.ShapeDtypeStruct((b, s, d), F32),
        compiler_params=_params("arbitrary", "arbitrary"),
        name="mixer_out",
    )(x, gate, ylru, ysb, yfox, gsb, gfox, wout)


def _block_diag(w):
    h, di, dj = w.shape
    return jnp.einsum("hij,hg->higj", w, jnp.eye(h, dtype=w.dtype)).reshape(h * di, h * dj)


def _diag_groups(bd):
    n = bd.shape[0]
    assert n % V7X_MXU_DIM == 0 and V7X_MXU_DIM % HEAD_DIM == 0
    n_grp = n // V7X_MXU_DIM
    blocks = bd.reshape(n_grp, V7X_MXU_DIM, n_grp, V7X_MXU_DIM)
    return jnp.stack([blocks[g, :, g, :] for g in range(n_grp)])


def _aug_placement(n_heads):
    n_out = (n_heads // HEADS_PER_PAIR) * V7X_LANES
    pq = np.zeros((3 * V7X_LANES, n_out), np.float32)
    pk = np.zeros((3 * V7X_LANES, n_out), np.float32)
    cq = np.zeros((1, n_out), np.float32)
    ck = np.zeros((1, n_out), np.float32)
    for h in range(n_heads):
        pr, e = divmod(h, HEADS_PER_PAIR)
        base = pr * V7X_LANES + e * AUG_PER_HEAD
        for j in range(3):
            pq[j * V7X_LANES + h, base + j] = 1.0
            cq[0, base + 3 + j] = 1.0
            ck[0, base + j] = 1.0
            pk[j * V7X_LANES + h, base + 3 + j] = -1.0
    return {"pq": jnp.asarray(pq, BF16), "cq": jnp.asarray(cq), "pk": jnp.asarray(pk, BF16), "ck": jnp.asarray(ck)}


def _mixer_weights(w_in, b_fgate, conv_w, conv_b, w_rgate, b_rgate, w_igate, b_igate, lru_lambda, g_qk, g_mix_out):
    w = conv_w.shape[1]
    n_heads = b_fgate.shape[0]
    hw = n_heads * HEAD_DIM
    o_sb = 2 * w
    o_fx = o_sb + 3 * hw
    o_f = o_fx + 3 * hw
    wf = jnp.zeros((w_in.shape[0], V7X_LANES), F32).at[:, :3 * n_heads].set(jnp.tile(w_in[:, o_f:o_f + n_heads], (1, 3)))
    bf = jnp.zeros((1, V7X_LANES), F32).at[0, :3 * n_heads].set(jnp.tile(b_fgate, 3))
    head_of = jnp.arange(hw) // HEAD_DIM
    hsum = (head_of[:, None] == head_of[None, :]).astype(BF16)
    return {
        "wlru": w_in[:, :o_sb].astype(BF16),
        "wsb": w_in[:, o_sb:o_fx].astype(BF16),
        "wfox": w_in[:, o_fx:o_f].astype(BF16),
        "wf": wf.astype(BF16),
        "bf": bf,
        "cw": conv_w,
        "cb": conv_b.reshape(1, w),
        "wgate": jnp.stack([_diag_groups(_block_diag(w_rgate)), _diag_groups(_block_diag(w_igate))]).astype(BF16),
        "bgate": jnp.concatenate([b_rgate, b_igate]).reshape(1, 2 * w),
        "lam": lru_lambda.reshape(1, w),
        "gq": jnp.tile(g_qk[0], n_heads).reshape(1, hw),
        "gk": jnp.tile(g_qk[1], n_heads).reshape(1, hw),
        "gmix": g_mix_out[:w].reshape(1, w),
        "hsum": hsum,
        **_aug_placement(n_heads),
    }


def kernel(x, c, w_ada, b_ada, g_norm, w_ffn_up, w_ffn_down, w_in, b_fgate, conv_w, conv_b, w_rgate, b_rgate,
           w_igate, b_igate, lru_lambda, g_qk, g_mix_out, w_out):
    b, s, d = x.shape
    depth = w_ada.shape[0]
    w = conv_w.shape[-1]
    hw = b_fgate.shape[-1] * HEAD_DIM

    mod = _adaln_mod(c, w_ada, b_ada).reshape(depth, b, N_SUB, 3, 1, d)

    for l in range(depth):
        shift = lambda j: mod[l, :, j, 0]
        scale = lambda j: mod[l, :, j, 1]
        gate = lambda j: mod[l, :, j, 2]
        gn = lambda j: g_norm[l, j].reshape(1, d)

        def ffn(xx, j, i):
            return _ffn(xx.reshape(b * s, d), shift(j), scale(j), gate(j), gn(j), w_ffn_up[l, i].astype(BF16),
                        w_ffn_down[l, i].astype(BF16), s).reshape(b, s, d)

        x = ffn(x, 0, 0)

        wts = _mixer_weights(w_in[l], b_fgate[l], conv_w[l], conv_b[l], w_rgate[l], b_rgate[l], w_igate[l],
                             b_igate[l], lru_lambda[l], g_qk[l], g_mix_out[l])
        ylru, sbq, sbk, sbv, fxq, fxk, fxv, fxqa, fxka = _mixer_in(x, shift(1), scale(1), gn(1), wts)
        ysb = _sb_attention(sbq, sbk, sbv)
        yfox = _fox_attention(fxq, fxqa, fxk, fxka, fxv)
        x = _mixer_out(x, gate(1), ylru, ysb, yfox, g_mix_out[l, w:w + hw].reshape(1, hw),
                       g_mix_out[l, w + hw:].reshape(1, hw), w_out[l].astype(BF16))

        x = ffn(x, 2, 1)
    return x
```

```python
import functools

import jax
import jax.numpy as jnp
import numpy as np
from jax import lax
from jax.experimental import pallas as pl
from jax.experimental.pallas import tpu as pltpu

F32 = jnp.float32
BF16 = jnp.bfloat16

HEAD_DIM = 64
CONV_WIDTH = 4
LRU_C = 8.0
N_SUB = 3
EPS = 1e-6

V7X_LANES = 128
V7X_SUBLANES = 8
V7X_MXU_DIM = 256
V7X_VMEM_BYTES = 64 * 1024 * 1024
VMEM_LIMIT_BYTES = V7X_VMEM_BYTES - 8 * 1024 * 1024

HEADS_PER_PAIR = V7X_LANES // HEAD_DIM
AUG_PER_HEAD = 6
NEG_BIG = -1e30


def _params(*sem):
    return pltpu.CompilerParams(dimension_semantics=sem, vmem_limit_bytes=VMEM_LIMIT_BYTES)


def _sigmoid(x):
    return 1.0 / (1.0 + jnp.exp(-x))


def _log_sigmoid(x):
    return jnp.minimum(x, 0.0) - jnp.log(1.0 + jnp.exp(-jnp.abs(x)))


def _split2(v):
    hi = v.astype(BF16)
    lo = (v - hi.astype(F32)).astype(BF16)
    return hi, lo


def _split3(v):
    hi = v.astype(BF16).astype(F32)
    r = v - hi
    mid = r.astype(BF16).astype(F32)
    lo = (r - mid).astype(BF16).astype(F32)
    return hi, mid, lo


def _mod_norm(x, g, scale, shift):
    ms = jnp.mean(x * x, axis=-1, keepdims=True)
    return (x * lax.rsqrt(ms + EPS)) * (g * (1.0 + scale)) + shift


def _mod_kernel(c_ref, w_ref, b_ref, o_ref):
    c = c_ref[...]
    ca = c * _sigmoid(c)
    o_ref[0] = jnp.dot(ca.astype(BF16), w_ref[0].astype(BF16), preferred_element_type=F32) + b_ref[0]


def _adaln_mod(c, w_ada, b_ada):
    depth, d, n = w_ada.shape
    b = c.shape[0]
    tn = 1536
    assert n % tn == 0
    return pl.pallas_call(
        _mod_kernel,
        grid=(depth, n // tn),
        in_specs=[
            pl.BlockSpec((b, d), lambda l, j: (0, 0)),
            pl.BlockSpec((1, d, tn), lambda l, j: (l, 0, j)),
            pl.BlockSpec((1, 1, tn), lambda l, j: (l, 0, j)),
        ],
        out_specs=pl.BlockSpec((1, b, tn), lambda l, j: (l, 0, j)),
        out_shape=jax.ShapeDtypeStruct((depth, b, n), F32),
        compiler_params=_params("arbitrary", "arbitrary"),
        name="adaln_mod",
    )(c, w_ada, b_ada.reshape(depth, 1, n))


def _ffn_kernel(x_ref, shift_ref, scale_ref, gate_ref, g_ref, wup_ref, wd_ref, o_ref,
                h_scr, a_scr, acc_scr):
    ff = wd_ref.shape[0]
    tf = a_scr.shape[1]
    n_chunks = ff // tf
    h_scr[...] = _mod_norm(x_ref[...], g_ref[...], scale_ref[0], shift_ref[0]).astype(BF16)

    def up(c):
        hb = h_scr[...]
        off = pl.multiple_of(c * tf, tf)
        g = jnp.dot(hb, wup_ref[:, pl.ds(off, tf)], preferred_element_type=F32)
        u = jnp.dot(hb, wup_ref[:, pl.ds(off + ff, tf)], preferred_element_type=F32)
        return ((g * _sigmoid(g)) * u).astype(BF16)

    def down(c):
        return jnp.dot(a_scr[...], wd_ref[pl.ds(pl.multiple_of(c * tf, tf), tf), :], preferred_element_type=F32)

    a_scr[...] = up(0)
    d = down(0)
    a_next = up(1)
    acc_scr[...] = d
    a_scr[...] = a_next

    def chunk(c, carry):
        d = down(c)
        a_next = up(c + 1)
        acc_scr[...] += d
        a_scr[...] = a_next
        return carry

    lax.fori_loop(1, n_chunks - 1, chunk, 0, unroll=3)
    o_ref[...] = x_ref[...] + (0.5 * (1.0 + gate_ref[0])) * (acc_scr[...] + down(n_chunks - 1))


def _ffn(x2, shift, scale, gate, g, wup, wd, seq, tf=256):
    m, d = x2.shape
    tm = 1024
    assert m % tm == 0 and seq % tm == 0 and wd.shape[0] % tf == 0 and wup.shape[1] == 2 * wd.shape[0]
    per_seq = seq // tm
    vec = pl.BlockSpec((1, 1, d), lambda i: (i // per_seq, 0, 0))
    full = lambda a: pl.BlockSpec(a.shape, lambda i: (0,) * a.ndim)
    return pl.pallas_call(
        _ffn_kernel,
        grid=(m // tm,),
        in_specs=[pl.BlockSpec((tm, d), lambda i: (i, 0)), vec, vec, vec, full(g), full(wup), full(wd)],
        out_specs=pl.BlockSpec((tm, d), lambda i: (i, 0)),
        out_shape=jax.ShapeDtypeStruct((m, d), F32),
        scratch_shapes=[
            pltpu.VMEM((tm, d), BF16),
            pltpu.VMEM((tm, tf), BF16),
            pltpu.VMEM((tm, d), F32),
        ],
        compiler_params=_params("arbitrary"),
        name="ffn",
    )(x2, shift, scale, gate, g, wup, wd)


def _mixin_kernel(x_ref, shift_ref, scale_ref, g_ref, wlru_ref, wsb_ref, wfox_ref, wf_ref, bf_ref,
                  cw_ref, cb_ref, wgate_ref, bgate_ref, lam_ref, gq_ref, gk_ref, gmix_ref, hsum_ref,
                  pq_ref, cq_ref, pk_ref, ck_ref,
                  ylru_ref, sbq_ref, sbk_ref, sbv_ref, fxq_ref, fxk_ref, fxv_ref, fxqa_ref, fxka_ref,
                  xp_scr, a_scr, b_scr, hs_scr, hc_scr, fc_scr):
    ts = x_ref.shape[1]
    w = cw_ref.shape[1]
    n_heads = fxq_ref.shape[2] // HEAD_DIM
    pad = V7X_SUBLANES

    @pl.when(pl.program_id(1) == 0)
    def _():
        xp_scr[0:pad, :] = jnp.zeros((pad, w), F32)
        hc_scr[...] = jnp.zeros_like(hc_scr)
        fc_scr[...] = jnp.zeros_like(fc_scr)

    h = _mod_norm(x_ref[0], g_ref[...], scale_ref[0], shift_ref[0]).astype(BF16)

    p = jnp.dot(h, wlru_ref[...], preferred_element_type=F32)
    lru_g = p[:, w:]
    xp_scr[pad:pad + ts, :] = p[:, :w]
    u = cb_ref[...]
    for k in range(CONV_WIDTH):
        off = pad - (CONV_WIDTH - 1) + k
        u = u + cw_ref[k:k + 1, :] * xp_scr[off:off + ts, :]
    xp_scr[0:pad, :] = xp_scr[ts:ts + pad, :]

    ub = u.astype(BF16)

    def gate(which):
        n_grp, grp = wgate_ref.shape[1], wgate_ref.shape[2]
        pre = [jnp.dot(ub[:, g * grp:(g + 1) * grp], wgate_ref[which, g], preferred_element_type=F32)
               for g in range(n_grp)]
        return _sigmoid(jnp.concatenate(pre, axis=1) + bgate_ref[:, which * w:(which + 1) * w])

    r = gate(0)
    ig = gate(1)
    lam = lam_ref[...]
    softplus_neg_lam = jnp.maximum(-lam, 0.0) + jnp.log1p(jnp.exp(-jnp.abs(lam)))
    a = jnp.exp((-LRU_C * softplus_neg_lam) * r)
    a_scr[...] = a
    s = 1.0 - a * a
    b_scr[...] = jnp.where(s > 0.0, s * lax.rsqrt(s), 0.0) * (ig * u)

    row8 = lax.broadcasted_iota(jnp.int32, (V7X_SUBLANES, w), 0)

    def scan8(i, hprev):
        r0 = pl.multiple_of(i * V7X_SUBLANES, V7X_SUBLANES)
        aa = a_scr[pl.ds(r0, V7X_SUBLANES), :]
        bb = b_scr[pl.ds(r0, V7X_SUBLANES), :]
        for sh in (1, 2, 4):
            a_s = pltpu.roll(aa, sh, 0)
            b_s = pltpu.roll(bb, sh, 0)
            m = row8 >= sh
            bb = jnp.where(m, aa * b_s + bb, bb)
            aa = jnp.where(m, aa * a_s, aa)
        hh = aa * hprev + bb
        hs_scr[pl.ds(r0, V7X_SUBLANES), :] = hh
        return hh[V7X_SUBLANES - 1:V7X_SUBLANES, :]

    hc_scr[0:1, :] = lax.fori_loop(0, ts // V7X_SUBLANES, scan8, hc_scr[0:1, :], unroll=True)

    y = hs_scr[...] * jax.nn.gelu(lru_g)
    yn = y * lax.rsqrt(jnp.mean(y * y, axis=-1, keepdims=True) + EPS) * gmix_ref[...]
    ylru_ref[0] = yn.astype(BF16)

    scale = HEAD_DIM ** -0.5
    psb = jnp.dot(h, wsb_ref[...], preferred_element_type=F32)
    hw = n_heads * HEAD_DIM
    sbq_ref[0] = (psb[:, :hw] * scale).astype(BF16)
    sbk_ref[0] = psb[:, hw:2 * hw].astype(BF16)
    sbv_ref[0] = psb[:, 2 * hw:].astype(BF16)

    pfx = jnp.dot(h, wfox_ref[...], preferred_element_type=F32)

    def head_rms(t):
        ssq = jnp.dot((t * t).astype(BF16), hsum_ref[...], preferred_element_type=F32)
        return t * lax.rsqrt(ssq * (1.0 / HEAD_DIM) + EPS)

    fxq_ref[0] = (head_rms(pfx[:, :hw]) * (gq_ref[...] * scale)).astype(BF16)
    fxk_ref[0] = (head_rms(pfx[:, hw:2 * hw]) * gk_ref[...]).astype(BF16)
    fxv_ref[0] = pfx[:, 2 * hw:].astype(BF16)

    lane = lax.broadcasted_iota(jnp.int32, (ts, V7X_LANES), 1)
    logf = _log_sigmoid(jnp.dot(h, wf_ref[...], preferred_element_type=F32) + bf_ref[...])
    logf = jnp.where(lane < 3 * n_heads, logf, 0.0)
    hi, mid, lo = _split3(logf)
    pieces = jnp.where(lane < n_heads, hi, jnp.where(lane < 2 * n_heads, mid, lo)).astype(BF16)
    tri = (lax.broadcasted_iota(jnp.int32, (ts, ts), 0) >= lax.broadcasted_iota(jnp.int32, (ts, ts), 1))
    cs = jnp.dot(tri.astype(BF16), pieces, preferred_element_type=F32)
    cs = cs + pltpu.roll(cs, V7X_LANES - n_heads, 1) + pltpu.roll(cs, V7X_LANES - 2 * n_heads, 1)
    fcum = cs + fc_scr[0:1, :]
    fc_scr[0:1, :] = fcum[ts - 1:ts, :]

    f_pieces = jnp.concatenate(_split3(fcum), axis=1).astype(BF16)
    fxqa_ref[0] = (jnp.dot(f_pieces, pq_ref[...], preferred_element_type=F32) + cq_ref[...]).astype(BF16)
    fxka_ref[0] = (jnp.dot(f_pieces, pk_ref[...], preferred_element_type=F32) + ck_ref[...]).astype(BF16)


def _mixer_in(x, shift, scale, g, wts, seq_tile=512):
    b, s, d = x.shape
    ts = seq_tile
    assert s % ts == 0
    w = wts["cw"].shape[1]
    hw = wts["wsb"].shape[1] // 3
    n_pairs = hw // V7X_LANES
    vec = pl.BlockSpec((1, 1, d), lambda i, j: (i, 0, 0))
    full = lambda a: pl.BlockSpec(a.shape, lambda i, j: (0,) * a.ndim)
    names = ("wlru", "wsb", "wfox", "wf", "bf", "cw", "cb", "wgate", "bgate", "lam", "gq", "gk", "gmix", "hsum",
             "pq", "cq", "pk", "ck")
    consts = [wts[k] for k in names]
    seq_out = lambda width: pl.BlockSpec((1, ts, width), lambda i, j: (i, j, 0))
    widths = (w, hw, hw, hw, hw, hw, hw, n_pairs * V7X_LANES, n_pairs * V7X_LANES)
    return pl.pallas_call(
        _mixin_kernel,
        grid=(b, s // ts),
        in_specs=[pl.BlockSpec((1, ts, d), lambda i, j: (i, j, 0)), vec, vec, full(g)] + [full(a) for a in consts],
        out_specs=[seq_out(n) for n in widths],
        out_shape=[jax.ShapeDtypeStruct((b, s, n), BF16) for n in widths],
        scratch_shapes=[
            pltpu.VMEM((ts + V7X_SUBLANES, w), F32),
            pltpu.VMEM((ts, w), F32),
            pltpu.VMEM((ts, w), F32),
            pltpu.VMEM((ts, w), F32),
            pltpu.VMEM((V7X_SUBLANES, w), F32),
            pltpu.VMEM((V7X_SUBLANES, V7X_LANES), F32),
        ],
        compiler_params=_params("arbitrary", "arbitrary"),
        name="mixer_in",
    )(x, shift, scale, g, *consts)


def _head_lanes(h):
    pr, e = divmod(h, HEADS_PER_PAIR)
    return slice(pr * V7X_LANES, (pr + 1) * V7X_LANES), e * HEAD_DIM


def _neg_abs(x):
    bits = lax.bitcast_convert_type(x, jnp.uint32) | jnp.uint32(0x80000000)
    return lax.bitcast_convert_type(bits, F32)


def _sb_kernel(q_ref, k_ref, v_ref, o_ref, qm_scr, z_scr, w_scr, carry_scr, acc_scr):
    tq = q_ref.shape[1]
    tk = tq
    n_heads = q_ref.shape[2] // HEAD_DIM
    heads = range(n_heads)
    i = pl.program_id(1)
    lane = lax.broadcasted_iota(jnp.int32, (tq, V7X_LANES), 1)
    past = lax.broadcasted_iota(jnp.int32, (tq, tk), 1) < lax.broadcasted_iota(jnp.int32, (tq, tk), 0)
    strict = (lax.broadcasted_iota(jnp.int32, (tk, tk), 0) > lax.broadcasted_iota(jnp.int32, (tk, tk), 1)).astype(BF16)

    for h in heads:
        cols, lo_lane = _head_lanes(h)
        q_pair = q_ref[0, :, cols]
        qm_scr[h] = jnp.where((lane >= lo_lane) & (lane < lo_lane + HEAD_DIM), q_pair, jnp.zeros_like(q_pair))
    carry_scr[...] = jnp.zeros_like(carry_scr)
    acc_scr[...] = jnp.zeros_like(acc_scr)

    def score(h, j):
        k0 = pl.multiple_of(j * tk, tk)
        return lax.dot_general(qm_scr[h], k_ref[0, pl.ds(k0, tk), _head_lanes(h)[0]], (((1,), (1,)), ((), ())),
                               preferred_element_type=F32)

    def weighted_values(h, j):
        k0 = pl.multiple_of(j * tk, tk)
        v = v_ref[0, pl.ds(k0, tk), _head_lanes(h)[0]]
        acc_scr[h] += jnp.dot(w_scr[h], v, preferred_element_type=F32)

    def consume(masked, ahead, behind):
        log_beta, log_1mb, after = [], [], []
        for h in heads:
            z = z_scr[h]
            lb = jnp.minimum(z, 0.0) - jnp.log(1.0 + jnp.exp(_neg_abs(z)))
            log_beta.append(lb)
            log_1mb.append(jnp.where(past, lb - z, 0.0) if masked else lb - z)
            after.append(jnp.dot(log_1mb[h].astype(BF16), strict, preferred_element_type=F32))
            if behind is not None:
                weighted_values(h, behind)
            if ahead is not None:
                z_scr[h] = score(h, ahead)
        for h in heads:
            carry = carry_scr[h]
            w_h = jnp.exp(log_beta[h] + after[h] + jnp.concatenate([carry] * (tk // V7X_LANES), axis=1))
            w_scr[h] = (jnp.where(past, w_h, 0.0) if masked else w_h).astype(BF16)
            carry_scr[h] = carry + jnp.broadcast_to(after[h][:, 0:1] + log_1mb[h][:, 0:1], (tq, V7X_LANES))

    for h in heads:
        z_scr[h] = score(h, i)
    consume(True, jnp.maximum(i - 1, 0), None)

    def step(u, c):
        consume(False, i - u - 1, i - u + 1)
        return c

    lax.fori_loop(1, i, step, 0)

    @pl.when(i > 0)
    def _():
        consume(False, None, 1)

    for h in heads:
        weighted_values(h, 0)

    for h in range(0, n_heads, HEADS_PER_PAIR):
        cols, _ = _head_lanes(h)
        o_ref[0, :, cols] = jnp.where(lane < HEAD_DIM, acc_scr[h], acc_scr[h + 1])


def _sb_attention(q, k, v, tq=256):
    b, s, hw = q.shape
    n_heads = hw // HEAD_DIM
    assert s % tq == 0
    return pl.pallas_call(
        _sb_kernel,
        grid=(b, s // tq),
        in_specs=[
            pl.BlockSpec((1, tq, hw), lambda i, j: (i, j, 0)),
            pl.BlockSpec((1, s, hw), lambda i, j: (i, 0, 0)),
            pl.BlockSpec((1, s, hw), lambda i, j: (i, 0, 0)),
        ],
        out_specs=pl.BlockSpec((1, tq, hw), lambda i, j: (i, j, 0)),
        out_shape=jax.ShapeDtypeStruct((b, s, hw), F32),
        scratch_shapes=[
            pltpu.VMEM((n_heads, tq, V7X_LANES), BF16),
            pltpu.VMEM((n_heads, tq, tq), F32),
            pltpu.VMEM((n_heads, tq, tq), BF16),
            pltpu.VMEM((n_heads, tq, V7X_LANES), F32),
            pltpu.VMEM((n_heads, tq, V7X_LANES), F32),
        ],
        compiler_params=_params("arbitrary", "arbitrary"),
        name="sb_attention",
    )(q, k, v)


def _fox_kernel(q_ref, qa_ref, k_ref, ka_ref, v_ref, o_ref, qm_scr, z_scr, m_scr, l_scr, acc_scr, *, tk):
    tq = q_ref.shape[1]
    n_heads = q_ref.shape[2] // HEAD_DIM
    heads = range(n_heads)
    i = pl.program_id(1)
    last = (i * tq) // tk
    lane = lax.broadcasted_iota(jnp.int32, (tq, V7X_LANES), 1)
    col_minus_row = lax.broadcasted_iota(jnp.int32, (tq, tk), 1) - lax.broadcasted_iota(jnp.int32, (tq, tk), 0)
    visible = col_minus_row <= i * tq - last * tk
    ones = jnp.ones((tk, V7X_LANES), BF16)

    for h in heads:
        cols, lo_lane = _head_lanes(h)
        e = h % HEADS_PER_PAIR
        q_pair = q_ref[0, :, cols]
        qa_pair = qa_ref[0, :, cols]
        in_head = (lane >= lo_lane) & (lane < lo_lane + HEAD_DIM)
        in_aug = (lane >= e * AUG_PER_HEAD) & (lane < (e + 1) * AUG_PER_HEAD)
        qm_scr[h] = jnp.concatenate([jnp.where(in_head, q_pair, jnp.zeros_like(q_pair)),
                                     jnp.where(in_aug, qa_pair, jnp.zeros_like(qa_pair))], axis=1)
    m_scr[...] = jnp.full_like(m_scr, NEG_BIG)
    l_scr[...] = jnp.zeros_like(l_scr)
    acc_scr[...] = jnp.zeros_like(acc_scr)

    def score(h, j):
        k0 = pl.multiple_of(j * tk, tk)
        cols, _ = _head_lanes(h)
        k = jnp.concatenate([k_ref[0, pl.ds(k0, tk), cols], ka_ref[0, pl.ds(k0, tk), cols]], axis=1)
        return lax.dot_general(qm_scr[h], k, (((1,), (1,)), ((), ())), preferred_element_type=F32)

    def consume(j, masked, ahead):
        k0 = pl.multiple_of(j * tk, tk)
        for h in heads:
            z = jnp.where(visible, z_scr[h], -jnp.inf) if masked else z_scr[h]
            m_prev = m_scr[h]
            m_new = jnp.maximum(m_prev, jnp.max(z, axis=-1, keepdims=True))
            alpha = jnp.exp(m_prev - m_new)
            p = jnp.exp(z - jnp.concatenate([m_new] * (tk // V7X_LANES), axis=1)).astype(BF16)
            m_scr[h] = m_new
            v = jnp.concatenate([v_ref[0, pl.ds(k0, tk), _head_lanes(h)[0]], ones], axis=1)
            pv = jnp.dot(p, v, preferred_element_type=F32)
            if ahead is not None:
                z_scr[h] = score(h, ahead)
            l_scr[h] = alpha * l_scr[h] + pv[:, V7X_LANES:]
            acc_scr[h] = alpha * acc_scr[h] + pv[:, :V7X_LANES]

    for h in heads:
        z_scr[h] = score(h, 0)

    def step(t, c):
        consume(t, False, t + 1)
        return c

    lax.fori_loop(0, last, step, 0)
    consume(last, True, None)
    for h in range(0, n_heads, HEADS_PER_PAIR):
        cols, _ = _head_lanes(h)
        o_ref[0, :, cols] = jnp.where(lane < HEAD_DIM, acc_scr[h] / l_scr[h], acc_scr[h + 1] / l_scr[h + 1])


def _fox_attention(q, qa, k, ka, v, tq=512, tk=512):
    b, s, hw = q.shape
    n_heads = hw // HEAD_DIM
    assert s % tq == 0 and s % tk == 0 and tk % tq == 0
    qspec = pl.BlockSpec((1, tq, hw), lambda i, j: (i, j, 0))
    kspec = pl.BlockSpec((1, s, hw), lambda i, j: (i, 0, 0))
    return pl.pallas_call(
        functools.partial(_fox_kernel, tk=tk),
        grid=(b, s // tq),
        in_specs=[qspec, qspec, kspec, kspec, kspec],
        out_specs=pl.BlockSpec((1, tq, hw), lambda i, j: (i, j, 0)),
        out_shape=jax.ShapeDtypeStruct((b, s, hw), F32),
        scratch_shapes=[
            pltpu.VMEM((n_heads, tq, 2 * V7X_LANES), BF16),
            pltpu.VMEM((n_heads, tq, tk), F32),
            pltpu.VMEM((n_heads, tq, V7X_LANES), F32),
            pltpu.VMEM((n_heads, tq, V7X_LANES), F32),
            pltpu.VMEM((n_heads, tq, V7X_LANES), F32),
        ],
        compiler_params=_params("arbitrary", "arbitrary"),
        name="fox_attention",
    )(q, qa, k, ka, v)


def _mixout_kernel(x_ref, gate_ref, ylru_ref, ysb_ref, yfox_ref, gsb_ref, gfox_ref, wout_ref, o_ref):
    def group_norm(y, g):
        return (y * lax.rsqrt(jnp.mean(y * y, axis=-1, keepdims=True) + EPS) * g).astype(BF16)

    y = jnp.concatenate([ylru_ref[0], group_norm(ysb_ref[0], gsb_ref[...]), group_norm(yfox_ref[0], gfox_ref[...])],
                        axis=1)
    o_ref[0] = x_ref[0] + (1.0 + gate_ref[0]) * jnp.dot(y, wout_ref[...], preferred_element_type=F32)


def _mixer_out(x, gate, ylru, ysb, yfox, gsb, gfox, wout, seq_tile=1024):
    b, s, d = x.shape
    ts = seq_tile
    assert s % ts == 0
    seq = lambda a: pl.BlockSpec((1, ts, a.shape[2]), lambda i, j: (i, j, 0))
    full = lambda a: pl.BlockSpec(a.shape, lambda i, j: (0,) * a.ndim)
    return pl.pallas_call(
        _mixout_kernel,
        grid=(b, s // ts),
        in_specs=[seq(x), pl.BlockSpec((1, 1, d), lambda i, j: (i, 0, 0)), seq(ylru), seq(ysb), seq(yfox),
                  full(gsb), full(gfox), full(wout)],
        out_specs=seq(x),
        out_shape=jax.ShapeDtypeStruct((b, s, d), F32),
        compiler_params=_params("arbitrary", "arbitrary"),
        name="mixer_out",
    )(x, gate, ylru, ysb, yfox, gsb, gfox, wout)


def _block_diag(w):
    h, di, dj = w.shape
    return jnp.einsum("hij,hg->higj", w, jnp.eye(h, dtype=w.dtype)).reshape(h * di, h * dj)


def _diag_groups(bd):
    n = bd.shape[0]
    assert n % V7X_MXU_DIM == 0 and V7X_MXU_DIM % HEAD_DIM == 0
    n_grp = n // V7X_MXU_DIM
    blocks = bd.reshape(n_grp, V7X_MXU_DIM, n_grp, V7X_MXU_DIM)
    return jnp.stack([blocks[g, :, g, :] for g in range(n_grp)])


def _aug_placement(n_heads):
    n_out = (n_heads // HEADS_PER_PAIR) * V7X_LANES
    pq = np.zeros((3 * V7X_LANES, n_out), np.float32)
    pk = np.zeros((3 * V7X_LANES, n_out), np.float32)
    cq = np.zeros((1, n_out), np.float32)
    ck = np.zeros((1, n_out), np.float32)
    for h in range(n_heads):
        pr, e = divmod(h, HEADS_PER_PAIR)
        base = pr * V7X_LANES + e * AUG_PER_HEAD
        for j in range(3):
            pq[j * V7X_LANES + h, base + j] = 1.0
            cq[0, base + 3 + j] = 1.0
            ck[0, base + j] = 1.0
            pk[j * V7X_LANES + h, base + 3 + j] = -1.0
    return {"pq": jnp.asarray(pq, BF16), "cq": jnp.asarray(cq), "pk": jnp.asarray(pk, BF16), "ck": jnp.asarray(ck)}


def _mixer_weights(w_in, b_fgate, conv_w, conv_b, w_rgate, b_rgate, w_igate, b_igate, lru_lambda, g_qk, g_mix_out):
    w = conv_w.shape[1]
    n_heads = b_fgate.shape[0]
    hw = n_heads * HEAD_DIM
    o_sb = 2 * w
    o_fx = o_sb + 3 * hw
    o_f = o_fx + 3 * hw
    wf = jnp.zeros((w_in.shape[0], V7X_LANES), F32).at[:, :3 * n_heads].set(jnp.tile(w_in[:, o_f:o_f + n_heads], (1, 3)))
    bf = jnp.zeros((1, V7X_LANES), F32).at[0, :3 * n_heads].set(jnp.tile(b_fgate, 3))
    head_of = jnp.arange(hw) // HEAD_DIM
    hsum = (head_of[:, None] == head_of[None, :]).astype(BF16)
    return {
        "wlru": w_in[:, :o_sb].astype(BF16),
        "wsb": w_in[:, o_sb:o_fx].astype(BF16),
        "wfox": w_in[:, o_fx:o_f].astype(BF16),
        "wf": wf.astype(BF16),
        "bf": bf,
        "cw": conv_w,
        "cb": conv_b.reshape(1, w),
        "wgate": jnp.stack([_diag_groups(_block_diag(w_rgate)), _diag_groups(_block_diag(w_igate))]).astype(BF16),
        "bgate": jnp.concatenate([b_rgate, b_igate]).reshape(1, 2 * w),
        "lam": lru_lambda.reshape(1, w),
        "gq": jnp.tile(g_qk[0], n_heads).reshape(1, hw),
        "gk": jnp.tile(g_qk[1], n_heads).reshape(1, hw),
        "gmix": g_mix_out[:w].reshape(1, w),
        "hsum": hsum,
        **_aug_placement(n_heads),
    }


def kernel(x, c, w_ada, b_ada, g_norm, w_ffn_up, w_ffn_down, w_in, b_fgate, conv_w, conv_b, w_rgate, b_rgate,
           w_igate, b_igate, lru_lambda, g_qk, g_mix_out, w_out):
    b, s, d = x.shape
    depth = w_ada.shape[0]
    w = conv_w.shape[-1]
    hw = b_fgate.shape[-1] * HEAD_DIM

    mod = _adaln_mod(c, w_ada, b_ada).reshape(depth, b, N_SUB, 3, 1, d)

    for l in range(depth):
        shift = lambda j: mod[l, :, j, 0]
        scale = lambda j: mod[l, :, j, 1]
        gate = lambda j: mod[l, :, j, 2]
        gn = lambda j: g_norm[l, j].reshape(1, d)

        def ffn(xx, j, i):
            return _ffn(xx.reshape(b * s, d), shift(j), scale(j), gate(j), gn(j), w_ffn_up[l, i].astype(BF16),
                        w_ffn_down[l, i].astype(BF16), s).reshape(b, s, d)

        x = ffn(x, 0, 0)

        wts = _mixer_weights(w_in[l], b_fgate[l], conv_w[l], conv_b[l], w_rgate[l], b_rgate[l], w_igate[l],
                             b_igate[l], lru_lambda[l], g_qk[l], g_mix_out[l])
        ylru, sbq, sbk, sbv, fxq, fxk, fxv, fxqa, fxka = _mixer_in(x, shift(1), scale(1), gn(1), wts)
        ysb = _sb_attention(sbq, sbk, sbv)
        yfox = _fox_attention(fxq, fxqa, fxk, fxka, fxv)
        x = _mixer_out(x, gate(1), ylru, ysb, yfox, g_mix_out[l, w:w + hw].reshape(1, hw),
                       g_mix_out[l, w + hw:].reshape(1, hw), w_out[l].astype(BF16))

        x = ffn(x, 2, 1)
    return x
```

```python
import functools

import jax
import jax.numpy as jnp
import numpy as np
from jax import lax
from jax.experimental import pallas as pl
from jax.experimental.pallas import tpu as pltpu

F32 = jnp.float32
BF16 = jnp.bfloat16

HEAD_DIM = 64
CONV_WIDTH = 4
LRU_C = 8.0
N_SUB = 3
EPS = 1e-6

V7X_LANES = 128
V7X_SUBLANES = 8
V7X_MXU_DIM = 256
V7X_VMEM_BYTES = 64 * 1024 * 1024
VMEM_LIMIT_BYTES = V7X_VMEM_BYTES - 8 * 1024 * 1024

HEADS_PER_PAIR = V7X_LANES // HEAD_DIM
AUG_PER_HEAD = 6
NEG_BIG = -1e30


def _params(*sem):
    return pltpu.CompilerParams(dimension_semantics=sem, vmem_limit_bytes=VMEM_LIMIT_BYTES)


def _sigmoid(x):
    return 1.0 / (1.0 + jnp.exp(-x))


def _log_sigmoid(x):
    return jnp.minimum(x, 0.0) - jnp.log(1.0 + jnp.exp(-jnp.abs(x)))


def _split2(v):
    hi = v.astype(BF16)
    lo = (v - hi.astype(F32)).astype(BF16)
    return hi, lo


def _split3(v):
    hi = v.astype(BF16).astype(F32)
    r = v - hi
    mid = r.astype(BF16).astype(F32)
    lo = (r - mid).astype(BF16).astype(F32)
    return hi, mid, lo


def _mod_norm(x, g, scale, shift):
    ms = jnp.mean(x * x, axis=-1, keepdims=True)
    return (x * lax.rsqrt(ms + EPS)) * (g * (1.0 + scale)) + shift


def _mod_kernel(c_ref, w_ref, b_ref, o_ref):
    c = c_ref[...]
    ca = c * _sigmoid(c)
    o_ref[0] = jnp.dot(ca.astype(BF16), w_ref[0].astype(BF16), preferred_element_type=F32) + b_ref[0]


def _adaln_mod(c, w_ada, b_ada):
    depth, d, n = w_ada.shape
    b = c.shape[0]
    tn = 1536
    assert n % tn == 0
    return pl.pallas_call(
        _mod_kernel,
        grid=(depth, n // tn),
        in_specs=[
            pl.BlockSpec((b, d), lambda l, j: (0, 0)),
            pl.BlockSpec((1, d, tn), lambda l, j: (l, 0, j)),
            pl.BlockSpec((1, 1, tn), lambda l, j: (l, 0, j)),
        ],
        out_specs=pl.BlockSpec((1, b, tn), lambda l, j: (l, 0, j)),
        out_shape=jax.ShapeDtypeStruct((depth, b, n), F32),
        compiler_params=_params("arbitrary", "arbitrary"),
        name="adaln_mod",
    )(c, w_ada, b_ada.reshape(depth, 1, n))


def _ffn_kernel(x_ref, shift_ref, scale_ref, gate_ref, g_ref, wup_ref, wd_ref, o_ref,
                h_scr, a_scr, acc_scr):
    ff = wd_ref.shape[0]
    tf = a_scr.shape[1]
    n_chunks = ff // tf
    h_scr[...] = _mod_norm(x_ref[...], g_ref[...], scale_ref[0], shift_ref[0]).astype(BF16)

    def up(c):
        hb = h_scr[...]
        off = pl.multiple_of(c * tf, tf)
        g = jnp.dot(hb, wup_ref[:, pl.ds(off, tf)], preferred_element_type=F32)
        u = jnp.dot(hb, wup_ref[:, pl.ds(off + ff, tf)], preferred_element_type=F32)
        return ((g * _sigmoid(g)) * u).astype(BF16)

    def down(c):
        return jnp.dot(a_scr[...], wd_ref[pl.ds(pl.multiple_of(c * tf, tf), tf), :], preferred_element_type=F32)

    a_scr[...] = up(0)
    d = down(0)
    a_next = up(1)
    acc_scr[...] = d
    a_scr[...] = a_next

    def chunk(c, carry):
        d = down(c)
        a_next = up(c + 1)
        acc_scr[...] += d
        a_scr[...] = a_next
        return carry

    lax.fori_loop(1, n_chunks - 1, chunk, 0, unroll=3)
    o_ref[...] = x_ref[...] + (0.5 * (1.0 + gate_ref[0])) * (acc_scr[...] + down(n_chunks - 1))


def _ffn(x2, shift, scale, gate, g, wup, wd, seq, tf=256):
    m, d = x2.shape
    tm = 1024
    assert m % tm == 0 and seq % tm == 0 and wd.shape[0] % tf == 0 and wup.shape[1] == 2 * wd.shape[0]
    per_seq = seq // tm
    vec = pl.BlockSpec((1, 1, d), lambda i: (i // per_seq, 0, 0))
    full = lambda a: pl.BlockSpec(a.shape, lambda i: (0,) * a.ndim)
    return pl.pallas_call(
        _ffn_kernel,
        grid=(m // tm,),
        in_specs=[pl.BlockSpec((tm, d), lambda i: (i, 0)), vec, vec, vec, full(g), full(wup), full(wd)],
        out_specs=pl.BlockSpec((tm, d), lambda i: (i, 0)),
        out_shape=jax.ShapeDtypeStruct((m, d), F32),
        scratch_shapes=[
            pltpu.VMEM((tm, d), BF16),
            pltpu.VMEM((tm, tf), BF16),
            pltpu.VMEM((tm, d), F32),
        ],
        compiler_params=_params("arbitrary"),
        name="ffn",
    )(x2, shift, scale, gate, g, wup, wd)


def _mixin_kernel(x_ref, shift_ref, scale_ref, g_ref, wlru_ref, wsb_ref, wfox_ref, wf_ref, bf_ref,
                  cw_ref, cb_ref, wgate_ref, bgate_ref, lam_ref, gq_ref, gk_ref, gmix_ref, hsum_ref,
                  pq_ref, cq_ref, pk_ref, ck_ref,
                  ylru_ref, sbq_ref, sbk_ref, sbv_ref, fxq_ref, fxk_ref, fxv_ref, fxqa_ref, fxka_ref,
                  xp_scr, a_scr, b_scr, hs_scr, hc_scr, fc_scr):
    ts = x_ref.shape[1]
    w = cw_ref.shape[1]
    n_heads = fxq_ref.shape[2] // HEAD_DIM
    pad = V7X_SUBLANES

    @pl.when(pl.program_id(1) == 0)
    def _():
        xp_scr[0:pad, :] = jnp.zeros((pad, w), F32)
        hc_scr[...] = jnp.zeros_like(hc_scr)
        fc_scr[...] = jnp.zeros_like(fc_scr)

    h = _mod_norm(x_ref[0], g_ref[...], scale_ref[0], shift_ref[0]).astype(BF16)

    p = jnp.dot(h, wlru_ref[...], preferred_element_type=F32)
    lru_g = p[:, w:]
    xp_scr[pad:pad + ts, :] = p[:, :w]
    u = cb_ref[...]
    for k in range(CONV_WIDTH):
        off = pad - (CONV_WIDTH - 1) + k
        u = u + cw_ref[k:k + 1, :] * xp_scr[off:off + ts, :]
    xp_scr[0:pad, :] = xp_scr[ts:ts + pad, :]

    ub = u.astype(BF16)

    def gate(which):
        n_grp, grp = wgate_ref.shape[1], wgate_ref.shape[2]
        pre = [jnp.dot(ub[:, g * grp:(g + 1) * grp], wgate_ref[which, g], preferred_element_type=F32)
               for g in range(n_grp)]
        return _sigmoid(jnp.concatenate(pre, axis=1) + bgate_ref[:, which * w:(which + 1) * w])

    r = gate(0)
    ig = gate(1)
    lam = lam_ref[...]
    softplus_neg_lam = jnp.maximum(-lam, 0.0) + jnp.log1p(jnp.exp(-jnp.abs(lam)))
    a = jnp.exp((-LRU_C * softplus_neg_lam) * r)
    a_scr[...] = a
    s = 1.0 - a * a
    b_scr[...] = jnp.where(s > 0.0, s * lax.rsqrt(s), 0.0) * (ig * u)

    row8 = lax.broadcasted_iota(jnp.int32, (V7X_SUBLANES, w), 0)

    def scan8(i, hprev):
        r0 = pl.multiple_of(i * V7X_SUBLANES, V7X_SUBLANES)
        aa = a_scr[pl.ds(r0, V7X_SUBLANES), :]
        bb = b_scr[pl.ds(r0, V7X_SUBLANES), :]
        for sh in (1, 2, 4):
            a_s = pltpu.roll(aa, sh, 0)
            b_s = pltpu.roll(bb, sh, 0)
            m = row8 >= sh
            bb = jnp.where(m, aa * b_s + bb, bb)
            aa = jnp.where(m, aa * a_s, aa)
        hh = aa * hprev + bb
        hs_scr[pl.ds(r0, V7X_SUBLANES), :] = hh
        return hh[V7X_SUBLANES - 1:V7X_SUBLANES, :]

    hc_scr[0:1, :] = lax.fori_loop(0, ts // V7X_SUBLANES, scan8, hc_scr[0:1, :], unroll=True)

    y = hs_scr[...] * jax.nn.gelu(lru_g)
    yn = y * lax.rsqrt(jnp.mean(y * y, axis=-1, keepdims=True) + EPS) * gmix_ref[...]
    ylru_ref[0] = yn.astype(BF16)

    scale = HEAD_DIM ** -0.5
    psb = jnp.dot(h, wsb_ref[...], preferred_element_type=F32)
    hw = n_heads * HEAD_DIM
    sbq_ref[0] = (psb[:, :hw] * scale).astype(BF16)
    sbk_ref[0] = psb[:, hw:2 * hw].astype(BF16)
    sbv_ref[0] = psb[:, 2 * hw:].astype(BF16)

    pfx = jnp.dot(h, wfox_ref[...], preferred_element_type=F32)

    def head_rms(t):
        ssq = jnp.dot((t * t).astype(BF16), hsum_ref[...], preferred_element_type=F32)
        return t * lax.rsqrt(ssq * (1.0 / HEAD_DIM) + EPS)

    fxq_ref[0] = (head_rms(pfx[:, :hw]) * (gq_ref[...] * scale)).astype(BF16)
    fxk_ref[0] = (head_rms(pfx[:, hw:2 * hw]) * gk_ref[...]).astype(BF16)
    fxv_ref[0] = pfx[:, 2 * hw:].astype(BF16)

    lane = lax.broadcasted_iota(jnp.int32, (ts, V7X_LANES), 1)
    logf = _log_sigmoid(jnp.dot(h, wf_ref[...], preferred_element_type=F32) + bf_ref[...])
    logf = jnp.where(lane < 3 * n_heads, logf, 0.0)
    hi, mid, lo = _split3(logf)
    pieces = jnp.where(lane < n_heads, hi, jnp.where(lane < 2 * n_heads, mid, lo)).astype(BF16)
    tri = (lax.broadcasted_iota(jnp.int32, (ts, ts), 0) >= lax.broadcasted_iota(jnp.int32, (ts, ts), 1))
    cs = jnp.dot(tri.astype(BF16), pieces, preferred_element_type=F32)
    cs = cs + pltpu.roll(cs, V7X_LANES - n_heads, 1) + pltpu.roll(cs, V7X_LANES - 2 * n_heads, 1)
    fcum = cs + fc_scr[0:1, :]
    fc_scr[0:1, :] = fcum[ts - 1:ts, :]

    f_pieces = jnp.concatenate(_split3(fcum), axis=1).astype(BF16)
    fxqa_ref[0] = (jnp.dot(f_pieces, pq_ref[...], preferred_element_type=F32) + cq_ref[...]).astype(BF16)
    fxka_ref[0] = (jnp.dot(f_pieces, pk_ref[...], preferred_element_type=F32) + ck_ref[...]).astype(BF16)


def _mixer_in(x, shift, scale, g, wts, seq_tile=512):
    b, s, d = x.shape
    ts = seq_tile
    assert s % ts == 0
    w = wts["cw"].shape[1]
    hw = wts["wsb"].shape[1] // 3
    n_pairs = hw // V7X_LANES
    vec = pl.BlockSpec((1, 1, d), lambda i, j: (i, 0, 0))
    full = lambda a: pl.BlockSpec(a.shape, lambda i, j: (0,) * a.ndim)
    names = ("wlru", "wsb", "wfox", "wf", "bf", "cw", "cb", "wgate", "bgate", "lam", "gq", "gk", "gmix", "hsum",
             "pq", "cq", "pk", "ck")
    consts = [wts[k] for k in names]
    seq_out = lambda width: pl.BlockSpec((1, ts, width), lambda i, j: (i, j, 0))
    widths = (w, hw, hw, hw, hw, hw, hw, n_pairs * V7X_LANES, n_pairs * V7X_LANES)
    return pl.pallas_call(
        _mixin_kernel,
        grid=(b, s // ts),
        in_specs=[pl.BlockSpec((1, ts, d), lambda i, j: (i, j, 0)), vec, vec, full(g)] + [full(a) for a in consts],
        out_specs=[seq_out(n) for n in widths],
        out_shape=[jax.ShapeDtypeStruct((b, s, n), BF16) for n in widths],
        scratch_shapes=[
            pltpu.VMEM((ts + V7X_SUBLANES, w), F32),
            pltpu.VMEM((ts, w), F32),
            pltpu.VMEM((ts, w), F32),
            pltpu.VMEM((ts, w), F32),
            pltpu.VMEM((V7X_SUBLANES, w), F32),
            pltpu.VMEM((V7X_SUBLANES, V7X_LANES), F32),
        ],
        compiler_params=_params("arbitrary", "arbitrary"),
        name="mixer_in",
    )(x, shift, scale, g, *consts)


def _head_lanes(h):
    pr, e = divmod(h, HEADS_PER_PAIR)
    return slice(pr * V7X_LANES, (pr + 1) * V7X_LANES), e * HEAD_DIM


def _neg_abs(x):
    bits = lax.bitcast_convert_type(x, jnp.uint32) | jnp.uint32(0x80000000)
    return lax.bitcast_convert_type(bits, F32)


def _sb_kernel(q_ref, k_ref, v_ref, o_ref, qm_scr, z_scr, w_scr, carry_scr, acc_scr):
    tq = q_ref.shape[1]
    tk = tq
    n_heads = q_ref.shape[2] // HEAD_DIM
    heads = range(n_heads)
    i = pl.program_id(1)
    lane = lax.broadcasted_iota(jnp.int32, (tq, V7X_LANES), 1)
    past = lax.broadcasted_iota(jnp.int32, (tq, tk), 1) < lax.broadcasted_iota(jnp.int32, (tq, tk), 0)
    strict = (lax.broadcasted_iota(jnp.int32, (tk, tk), 0) > lax.broadcasted_iota(jnp.int32, (tk, tk), 1)).astype(BF16)

    for h in heads:
        cols, lo_lane = _head_lanes(h)
        q_pair = q_ref[0, :, cols]
        qm_scr[h] = jnp.where((lane >= lo_lane) & (lane < lo_lane + HEAD_DIM), q_pair, jnp.zeros_like(q_pair))
    carry_scr[...] = jnp.zeros_like(carry_scr)
    acc_scr[...] = jnp.zeros_like(acc_scr)

    def score(h, j):
        k0 = pl.multiple_of(j * tk, tk)
        return lax.dot_general(qm_scr[h], k_ref[0, pl.ds(k0, tk), _head_lanes(h)[0]], (((1,), (1,)), ((), ())),
                               preferred_element_type=F32)

    def weighted_values(h, j):
        k0 = pl.multiple_of(j * tk, tk)
        v = v_ref[0, pl.ds(k0, tk), _head_lanes(h)[0]]
        acc_scr[h] += jnp.dot(w_scr[h], v, preferred_element_type=F32)

    def consume(masked, ahead, behind):
        log_beta, log_1mb, after = [], [], []
        for h in heads:
            z = z_scr[h].astype(BF16)
            lb = jnp.minimum(z, 0.0) - jnp.log(1.0 + jnp.exp(-jnp.abs(z)))
            log_beta.append(lb)
            log_1mb.append(jnp.where(past, lb - z, 0.0) if masked else lb - z)
            carry = jnp.concatenate([carry_scr[h]] * (tk // V7X_LANES), axis=1)
            after.append(carry + jnp.dot(log_1mb[h].astype(BF16), strict, preferred_element_type=F32))
            if behind is not None:
                weighted_values(h, behind)
            if ahead is not None:
                z_scr[h] = score(h, ahead)
        for h in heads:
            w_h = jnp.exp(log_beta[h] + after[h].astype(BF16))
            w_scr[h] = jnp.where(past, w_h, jnp.zeros_like(w_h)) if masked else w_h
            carry_scr[h] = jnp.broadcast_to(after[h][:, 0:1] + log_1mb[h][:, 0:1], (tq, V7X_LANES))

    for h in heads:
        z_scr[h] = score(h, i)
    consume(True, jnp.maximum(i - 1, 0), None)

    def step(u, c):
        consume(False, i - u - 1, i - u + 1)
        return c

    lax.fori_loop(1, i, step, 0)

    @pl.when(i > 0)
    def _():
        consume(False, None, 1)

    for h in heads:
        weighted_values(h, 0)

    for h in range(0, n_heads, HEADS_PER_PAIR):
        cols, _ = _head_lanes(h)
        o_ref[0, :, cols] = jnp.where(lane < HEAD_DIM, acc_scr[h], acc_scr[h + 1])


def _sb_attention(q, k, v, tq=256):
    b, s, hw = q.shape
    n_heads = hw // HEAD_DIM
    assert s % tq == 0
    return pl.pallas_call(
        _sb_kernel,
        grid=(b, s // tq),
        in_specs=[
            pl.BlockSpec((1, tq, hw), lambda i, j: (i, j, 0)),
            pl.BlockSpec((1, s, hw), lambda i, j: (i, 0, 0)),
            pl.BlockSpec((1, s, hw), lambda i, j: (i, 0, 0)),
        ],
        out_specs=pl.BlockSpec((1, tq, hw), lambda i, j: (i, j, 0)),
        out_shape=jax.ShapeDtypeStruct((b, s, hw), F32),
        scratch_shapes=[
            pltpu.VMEM((n_heads, tq, V7X_LANES), BF16),
            pltpu.VMEM((n_heads, tq, tq), F32),
            pltpu.VMEM((n_heads, tq, tq), BF16),
            pltpu.VMEM((n_heads, tq, V7X_LANES), F32),
            pltpu.VMEM((n_heads, tq, V7X_LANES), F32),
        ],
        compiler_params=_params("arbitrary", "arbitrary"),
        name="sb_attention",
    )(q, k, v)


def _fox_kernel(q_ref, qa_ref, k_ref, ka_ref, v_ref, o_ref, qm_scr, z_scr, m_scr, l_scr, acc_scr, *, tk):
    tq = q_ref.shape[1]
    n_heads = q_ref.shape[2] // HEAD_DIM
    heads = range(n_heads)
    i = pl.program_id(1)
    last = (i * tq) // tk
    lane = lax.broadcasted_iota(jnp.int32, (tq, V7X_LANES), 1)
    col_minus_row = lax.broadcasted_iota(jnp.int32, (tq, tk), 1) - lax.broadcasted_iota(jnp.int32, (tq, tk), 0)
    visible = col_minus_row <= i * tq - last * tk
    ones = jnp.ones((tk, V7X_LANES), BF16)

    for h in heads:
        cols, lo_lane = _head_lanes(h)
        e = h % HEADS_PER_PAIR
        q_pair = q_ref[0, :, cols]
        qa_pair = qa_ref[0, :, cols]
        in_head = (lane >= lo_lane) & (lane < lo_lane + HEAD_DIM)
        in_aug = (lane >= e * AUG_PER_HEAD) & (lane < (e + 1) * AUG_PER_HEAD)
        qm_scr[h] = jnp.concatenate([jnp.where(in_head, q_pair, jnp.zeros_like(q_pair)),
                                     jnp.where(in_aug, qa_pair, jnp.zeros_like(qa_pair))], axis=1)
    m_scr[...] = jnp.full_like(m_scr, NEG_BIG)
    l_scr[...] = jnp.zeros_like(l_scr)
    acc_scr[...] = jnp.zeros_like(acc_scr)

    def score(h, j):
        k0 = pl.multiple_of(j * tk, tk)
        cols, _ = _head_lanes(h)
        k = jnp.concatenate([k_ref[0, pl.ds(k0, tk), cols], ka_ref[0, pl.ds(k0, tk), cols]], axis=1)
        return lax.dot_general(qm_scr[h], k, (((1,), (1,)), ((), ())), preferred_element_type=F32)

    def consume(j, masked, ahead):
        k0 = pl.multiple_of(j * tk, tk)
        for h in heads:
            z = jnp.where(visible, z_scr[h], -jnp.inf) if masked else z_scr[h]
            m_prev = m_scr[h]
            m_new = jnp.maximum(m_prev, jnp.max(z, axis=-1, keepdims=True))
            alpha = jnp.exp(m_prev - m_new)
            p = jnp.exp(z - jnp.concatenate([m_new] * (tk // V7X_LANES), axis=1)).astype(BF16)
            m_scr[h] = m_new
            v = jnp.concatenate([v_ref[0, pl.ds(k0, tk), _head_lanes(h)[0]], ones], axis=1)
            pv = jnp.dot(p, v, preferred_element_type=F32)
            if ahead is not None:
                z_scr[h] = score(h, ahead)
            l_scr[h] = alpha * l_scr[h] + pv[:, V7X_LANES:]
            acc_scr[h] = alpha * acc_scr[h] + pv[:, :V7X_LANES]

    for h in heads:
        z_scr[h] = score(h, 0)

    def step(t, c):
        consume(t, False, t + 1)
        return c

    lax.fori_loop(0, last, step, 0)
    consume(last, True, None)
    for h in range(0, n_heads, HEADS_PER_PAIR):
        cols, _ = _head_lanes(h)
        o_ref[0, :, cols] = jnp.where(lane < HEAD_DIM, acc_scr[h] / l_scr[h], acc_scr[h + 1] / l_scr[h + 1])


def _fox_attention(q, qa, k, ka, v, tq=512, tk=512):
    b, s, hw = q.shape
    n_heads = hw // HEAD_DIM
    assert s % tq == 0 and s % tk == 0 and tk % tq == 0
    qspec = pl.BlockSpec((1, tq, hw), lambda i, j: (i, j, 0))
    kspec = pl.BlockSpec((1, s, hw), lambda i, j: (i, 0, 0))
    return pl.pallas_call(
        functools.partial(_fox_kernel, tk=tk),
        grid=(b, s // tq),
        in_specs=[qspec, qspec, kspec, kspec, kspec],
        out_specs=pl.BlockSpec((1, tq, hw), lambda i, j: (i, j, 0)),
        out_shape=jax.ShapeDtypeStruct((b, s, hw), F32),
        scratch_shapes=[
            pltpu.VMEM((n_heads, tq, 2 * V7X_LANES), BF16),
            pltpu.VMEM((n_heads, tq, tk), F32),
            pltpu.VMEM((n_heads, tq, V7X_LANES), F32),
            pltpu.VMEM((n_heads, tq, V7X_LANES), F32),
            pltpu.VMEM((n_heads, tq, V7X_LANES), F32),
        ],
        compiler_params=_params("arbitrary", "arbitrary"),
        name="fox_attention",
    )(q, qa, k, ka, v)


def _mixout_kernel(x_ref, gate_ref, ylru_ref, ysb_ref, yfox_ref, gsb_ref, gfox_ref, wout_ref, o_ref):
    def group_norm(y, g):
        return (y * lax.rsqrt(jnp.mean(y * y, axis=-1, keepdims=True) + EPS) * g).astype(BF16)

    y = jnp.concatenate([ylru_ref[0], group_norm(ysb_ref[0], gsb_ref[...]), group_norm(yfox_ref[0], gfox_ref[...])],
                        axis=1)
    o_ref[0] = x_ref[0] + (1.0 + gate_ref[0]) * jnp.dot(y, wout_ref[...], preferred_element_type=F32)


def _mixer_out(x, gate, ylru, ysb, yfox, gsb, gfox, wout, seq_tile=1024):
    b, s, d = x.shape
    ts = seq_tile
    assert s % ts == 0
    seq = lambda a: pl.BlockSpec((1, ts, a.shape[2]), lambda i, j: (i, j, 0))
    full = lambda a: pl.BlockSpec(a.shape, lambda i, j: (0,) * a.ndim)
    return pl.pallas_call(
        _mixout_kernel,
        grid=(b, s // ts),
        in_specs=[seq(x), pl.BlockSpec((1, 1, d), lambda i, j: (i, 0, 0)), seq(ylru), seq(ysb), seq(yfox),
                  full(gsb), full(gfox), full(wout)],
        out_specs=seq(x),
        out_shape=jax.ShapeDtypeStruct((b, s, d), F32),
        compiler_params=_params("arbitrary", "arbitrary"),
        name="mixer_out",
    )(x, gate, ylru, ysb, yfox, gsb, gfox, wout)


def _block_diag(w):
    h, di, dj = w.shape
    return jnp.einsum("hij,hg->higj", w, jnp.eye(h, dtype=w.dtype)).reshape(h * di, h * dj)


def _diag_groups(bd):
    n = bd.shape[0]
    assert n % V7X_MXU_DIM == 0 and V7X_MXU_DIM % HEAD_DIM == 0
    n_grp = n // V7X_MXU_DIM
    blocks = bd.reshape(n_grp, V7X_MXU_DIM, n_grp, V7X_MXU_DIM)
    return jnp.stack([blocks[g, :, g, :] for g in range(n_grp)])


def _aug_placement(n_heads):
    n_out = (n_heads // HEADS_PER_PAIR) * V7X_LANES
    pq = np.zeros((3 * V7X_LANES, n_out), np.float32)
    pk = np.zeros((3 * V7X_LANES, n_out), np.float32)
    cq = np.zeros((1, n_out), np.float32)
    ck = np.zeros((1, n_out), np.float32)
    for h in range(n_heads):
        pr, e = divmod(h, HEADS_PER_PAIR)
        base = pr * V7X_LANES + e * AUG_PER_HEAD
        for j in range(3):
            pq[j * V7X_LANES + h, base + j] = 1.0
            cq[0, base + 3 + j] = 1.0
            ck[0, base + j] = 1.0
            pk[j * V7X_LANES + h, base + 3 + j] = -1.0
    return {"pq": jnp.asarray(pq, BF16), "cq": jnp.asarray(cq), "pk": jnp.asarray(pk, BF16), "ck": jnp.asarray(ck)}


def _mixer_weights(w_in, b_fgate, conv_w, conv_b, w_rgate, b_rgate, w_igate, b_igate, lru_lambda, g_qk, g_mix_out):
    w = conv_w.shape[1]
    n_heads = b_fgate.shape[0]
    hw = n_heads * HEAD_DIM
    o_sb = 2 * w
    o_fx = o_sb + 3 * hw
    o_f = o_fx + 3 * hw
    wf = jnp.zeros((w_in.shape[0], V7X_LANES), F32).at[:, :3 * n_heads].set(jnp.tile(w_in[:, o_f:o_f + n_heads], (1, 3)))
    bf = jnp.zeros((1, V7X_LANES), F32).at[0, :3 * n_heads].set(jnp.tile(b_fgate, 3))
    head_of = jnp.arange(hw) // HEAD_DIM
    hsum = (head_of[:, None] == head_of[None, :]).astype(BF16)
    return {
        "wlru": w_in[:, :o_sb].astype(BF16),
        "wsb": w_in[:, o_sb:o_fx].astype(BF16),
        "wfox": w_in[:, o_fx:o_f].astype(BF16),
        "wf": wf.astype(BF16),
        "bf": bf,
        "cw": conv_w,
        "cb": conv_b.reshape(1, w),
        "wgate": jnp.stack([_diag_groups(_block_diag(w_rgate)), _diag_groups(_block_diag(w_igate))]).astype(BF16),
        "bgate": jnp.concatenate([b_rgate, b_igate]).reshape(1, 2 * w),
        "lam": lru_lambda.reshape(1, w),
        "gq": jnp.tile(g_qk[0], n_heads).reshape(1, hw),
        "gk": jnp.tile(g_qk[1], n_heads).reshape(1, hw),
        "gmix": g_mix_out[:w].reshape(1, w),
        "hsum": hsum,
        **_aug_placement(n_heads),
    }


def kernel(x, c, w_ada, b_ada, g_norm, w_ffn_up, w_ffn_down, w_in, b_fgate, conv_w, conv_b, w_rgate, b_rgate,
           w_igate, b_igate, lru_lambda, g_qk, g_mix_out, w_out):
    b, s, d = x.shape
    depth = w_ada.shape[0]
    w = conv_w.shape[-1]
    hw = b_fgate.shape[-1] * HEAD_DIM

    mod = _adaln_mod(c, w_ada, b_ada).reshape(depth, b, N_SUB, 3, 1, d)

    for l in range(depth):
        shift = lambda j: mod[l, :, j, 0]
        scale = lambda j: mod[l, :, j, 1]
        gate = lambda j: mod[l, :, j, 2]
        gn = lambda j: g_norm[l, j].reshape(1, d)

        def ffn(xx, j, i):
            return _ffn(xx.reshape(b * s, d), shift(j), scale(j), gate(j), gn(j), w_ffn_up[l, i].astype(BF16),
                        w_ffn_down[l, i].astype(BF16), s).reshape(b, s, d)

        x = ffn(x, 0, 0)

        wts = _mixer_weights(w_in[l], b_fgate[l], conv_w[l], conv_b[l], w_rgate[l], b_rgate[l], w_igate[l],
                             b_igate[l], lru_lambda[l], g_qk[l], g_mix_out[l])
        ylru, sbq, sbk, sbv, fxq, fxk, fxv, fxqa, fxka = _mixer_in(x, shift(1), scale(1), gn(1), wts)
        ysb = _sb_attention(sbq, sbk, sbv)
        yfox = _fox_attention(fxq, fxqa, fxk, fxka, fxv)
        x = _mixer_out(x, gate(1), ylru, ysb, yfox, g_mix_out[l, w:w + hw].reshape(1, hw),
                       g_mix_out[l, w + hw:].reshape(1, hw), w_out[l].astype(BF16))

        x = ffn(x, 2, 1)
    return x
```

```python
import functools

import jax
import jax.numpy as jnp
import numpy as np
from jax import lax
from jax.experimental import pallas as pl
from jax.experimental.pallas import tpu as pltpu

F32 = jnp.float32
BF16 = jnp.bfloat16

HEAD_DIM = 64
CONV_WIDTH = 4
LRU_C = 8.0
N_SUB = 3
EPS = 1e-6

V7X_LANES = 128
V7X_SUBLANES = 8
V7X_MXU_DIM = 256
V7X_VMEM_BYTES = 64 * 1024 * 1024
VMEM_LIMIT_BYTES = V7X_VMEM_BYTES - 8 * 1024 * 1024

HEADS_PER_PAIR = V7X_LANES // HEAD_DIM
AUG_PER_HEAD = 6
NEG_BIG = -1e30


def _params(*sem):
    return pltpu.CompilerParams(dimension_semantics=sem, vmem_limit_bytes=VMEM_LIMIT_BYTES)


def _sigmoid(x):
    return 1.0 / (1.0 + jnp.exp(-x))


def _log_sigmoid(x):
    return jnp.minimum(x, 0.0) - jnp.log(1.0 + jnp.exp(-jnp.abs(x)))


def _split2(v):
    hi = v.astype(BF16)
    lo = (v - hi.astype(F32)).astype(BF16)
    return hi, lo


def _split3(v):
    hi = v.astype(BF16).astype(F32)
    r = v - hi
    mid = r.astype(BF16).astype(F32)
    lo = (r - mid).astype(BF16).astype(F32)
    return hi, mid, lo


def _mod_norm(x, g, scale, shift):
    ms = jnp.mean(x * x, axis=-1, keepdims=True)
    return (x * lax.rsqrt(ms + EPS)) * (g * (1.0 + scale)) + shift


def _mod_kernel(c_ref, w_ref, b_ref, o_ref):
    c = c_ref[...]
    ca = c * _sigmoid(c)
    o_ref[0] = jnp.dot(ca.astype(BF16), w_ref[0].astype(BF16), preferred_element_type=F32) + b_ref[0]


def _adaln_mod(c, w_ada, b_ada):
    depth, d, n = w_ada.shape
    b = c.shape[0]
    tn = 1536
    assert n % tn == 0
    return pl.pallas_call(
        _mod_kernel,
        grid=(depth, n // tn),
        in_specs=[
            pl.BlockSpec((b, d), lambda l, j: (0, 0)),
            pl.BlockSpec((1, d, tn), lambda l, j: (l, 0, j)),
            pl.BlockSpec((1, 1, tn), lambda l, j: (l, 0, j)),
        ],
        out_specs=pl.BlockSpec((1, b, tn), lambda l, j: (l, 0, j)),
        out_shape=jax.ShapeDtypeStruct((depth, b, n), F32),
        compiler_params=_params("arbitrary", "arbitrary"),
        name="adaln_mod",
    )(c, w_ada, b_ada.reshape(depth, 1, n))


def _ffn_kernel(x_ref, shift_ref, scale_ref, gate_ref, g_ref, wup_ref, wd_ref, o_ref,
                h_scr, a_scr, acc_scr):
    ff = wd_ref.shape[0]
    tf = a_scr.shape[1]
    n_chunks = ff // tf
    h_scr[...] = _mod_norm(x_ref[...], g_ref[...], scale_ref[0], shift_ref[0]).astype(BF16)

    def up(c):
        hb = h_scr[...]
        off = pl.multiple_of(c * tf, tf)
        g = jnp.dot(hb, wup_ref[:, pl.ds(off, tf)], preferred_element_type=F32)
        u = jnp.dot(hb, wup_ref[:, pl.ds(off + ff, tf)], preferred_element_type=F32)
        return ((g * _sigmoid(g)) * u).astype(BF16)

    def down(c):
        return jnp.dot(a_scr[...], wd_ref[pl.ds(pl.multiple_of(c * tf, tf), tf), :], preferred_element_type=F32)

    a_scr[...] = up(0)
    d = down(0)
    a_next = up(1)
    acc_scr[...] = d
    a_scr[...] = a_next

    def chunk(c, carry):
        d = down(c)
        a_next = up(c + 1)
        acc_scr[...] += d
        a_scr[...] = a_next
        return carry

    lax.fori_loop(1, n_chunks - 1, chunk, 0, unroll=3)
    o_ref[...] = x_ref[...] + (0.5 * (1.0 + gate_ref[0])) * (acc_scr[...] + down(n_chunks - 1))


def _ffn(x2, shift, scale, gate, g, wup, wd, seq, tf=256):
    m, d = x2.shape
    tm = 1024
    assert m % tm == 0 and seq % tm == 0 and wd.shape[0] % tf == 0 and wup.shape[1] == 2 * wd.shape[0]
    per_seq = seq // tm
    vec = pl.BlockSpec((1, 1, d), lambda i: (i // per_seq, 0, 0))
    full = lambda a: pl.BlockSpec(a.shape, lambda i: (0,) * a.ndim)
    return pl.pallas_call(
        _ffn_kernel,
        grid=(m // tm,),
        in_specs=[pl.BlockSpec((tm, d), lambda i: (i, 0)), vec, vec, vec, full(g), full(wup), full(wd)],
        out_specs=pl.BlockSpec((tm, d), lambda i: (i, 0)),
        out_shape=jax.ShapeDtypeStruct((m, d), F32),
        scratch_shapes=[
            pltpu.VMEM((tm, d), BF16),
            pltpu.VMEM((tm, tf), BF16),
            pltpu.VMEM((tm, d), F32),
        ],
        compiler_params=_params("arbitrary"),
        name="ffn",
    )(x2, shift, scale, gate, g, wup, wd)


def _mixin_kernel(x_ref, shift_ref, scale_ref, g_ref, wlru_ref, wsb_ref, wfox_ref, wf_ref, bf_ref,
                  cw_ref, cb_ref, wgate_ref, bgate_ref, lam_ref, gq_ref, gk_ref, gmix_ref, hsum_ref,
                  pq_ref, cq_ref, pk_ref, ck_ref,
                  ylru_ref, sbq_ref, sbk_ref, sbv_ref, fxq_ref, fxk_ref, fxv_ref, fxqa_ref, fxka_ref,
                  xp_scr, a_scr, b_scr, hs_scr, hc_scr, fc_scr):
    ts = x_ref.shape[1]
    w = cw_ref.shape[1]
    n_heads = fxq_ref.shape[2] // HEAD_DIM
    pad = V7X_SUBLANES

    @pl.when(pl.program_id(1) == 0)
    def _():
        xp_scr[0:pad, :] = jnp.zeros((pad, w), F32)
        hc_scr[...] = jnp.zeros_like(hc_scr)
        fc_scr[...] = jnp.zeros_like(fc_scr)

    h = _mod_norm(x_ref[0], g_ref[...], scale_ref[0], shift_ref[0]).astype(BF16)

    p = jnp.dot(h, wlru_ref[...], preferred_element_type=F32)
    lru_g = p[:, w:]
    xp_scr[pad:pad + ts, :] = p[:, :w]
    u = cb_ref[...]
    for k in range(CONV_WIDTH):
        off = pad - (CONV_WIDTH - 1) + k
        u = u + cw_ref[k:k + 1, :] * xp_scr[off:off + ts, :]
    xp_scr[0:pad, :] = xp_scr[ts:ts + pad, :]

    ub = u.astype(BF16)

    def gate(which):
        n_grp, grp = wgate_ref.shape[1], wgate_ref.shape[2]
        pre = [jnp.dot(ub[:, g * grp:(g + 1) * grp], wgate_ref[which, g], preferred_element_type=F32)
               for g in range(n_grp)]
        return _sigmoid(jnp.concatenate(pre, axis=1) + bgate_ref[:, which * w:(which + 1) * w])

    r = gate(0)
    ig = gate(1)
    lam = lam_ref[...]
    softplus_neg_lam = jnp.maximum(-lam, 0.0) + jnp.log1p(jnp.exp(-jnp.abs(lam)))
    a = jnp.exp((-LRU_C * softplus_neg_lam) * r)
    a_scr[...] = a
    s = 1.0 - a * a
    b_scr[...] = jnp.where(s > 0.0, s * lax.rsqrt(s), 0.0) * (ig * u)

    row8 = lax.broadcasted_iota(jnp.int32, (V7X_SUBLANES, w), 0)

    def scan8(i, hprev):
        r0 = pl.multiple_of(i * V7X_SUBLANES, V7X_SUBLANES)
        aa = a_scr[pl.ds(r0, V7X_SUBLANES), :]
        bb = b_scr[pl.ds(r0, V7X_SUBLANES), :]
        for sh in (1, 2, 4):
            a_s = pltpu.roll(aa, sh, 0)
            b_s = pltpu.roll(bb, sh, 0)
            m = row8 >= sh
            bb = jnp.where(m, aa * b_s + bb, bb)
            aa = jnp.where(m, aa * a_s, aa)
        hh = aa * hprev + bb
        hs_scr[pl.ds(r0, V7X_SUBLANES), :] = hh
        return hh[V7X_SUBLANES - 1:V7X_SUBLANES, :]

    hc_scr[0:1, :] = lax.fori_loop(0, ts // V7X_SUBLANES, scan8, hc_scr[0:1, :], unroll=True)

    y = hs_scr[...] * jax.nn.gelu(lru_g)
    yn = y * lax.rsqrt(jnp.mean(y * y, axis=-1, keepdims=True) + EPS) * gmix_ref[...]
    ylru_ref[0] = yn.astype(BF16)

    scale = HEAD_DIM ** -0.5
    psb = jnp.dot(h, wsb_ref[...], preferred_element_type=F32)
    hw = n_heads * HEAD_DIM
    sbq_ref[0] = (psb[:, :hw] * scale).astype(BF16)
    sbk_ref[0] = psb[:, hw:2 * hw].astype(BF16)
    sbv_ref[0] = psb[:, 2 * hw:].astype(BF16)

    pfx = jnp.dot(h, wfox_ref[...], preferred_element_type=F32)

    def head_rms(t):
        ssq = jnp.dot((t * t).astype(BF16), hsum_ref[...], preferred_element_type=F32)
        return t * lax.rsqrt(ssq * (1.0 / HEAD_DIM) + EPS)

    fxq_ref[0] = (head_rms(pfx[:, :hw]) * (gq_ref[...] * scale)).astype(BF16)
    fxk_ref[0] = (head_rms(pfx[:, hw:2 * hw]) * gk_ref[...]).astype(BF16)
    fxv_ref[0] = pfx[:, 2 * hw:].astype(BF16)

    lane = lax.broadcasted_iota(jnp.int32, (ts, V7X_LANES), 1)
    logf = _log_sigmoid(jnp.dot(h, wf_ref[...], preferred_element_type=F32) + bf_ref[...])
    logf = jnp.where(lane < 3 * n_heads, logf, 0.0)
    hi, mid, lo = _split3(logf)
    pieces = jnp.where(lane < n_heads, hi, jnp.where(lane < 2 * n_heads, mid, lo)).astype(BF16)
    tri = (lax.broadcasted_iota(jnp.int32, (ts, ts), 0) >= lax.broadcasted_iota(jnp.int32, (ts, ts), 1))
    cs = jnp.dot(tri.astype(BF16), pieces, preferred_element_type=F32)
    cs = cs + pltpu.roll(cs, V7X_LANES - n_heads, 1) + pltpu.roll(cs, V7X_LANES - 2 * n_heads, 1)
    fcum = cs + fc_scr[0:1, :]
    fc_scr[0:1, :] = fcum[ts - 1:ts, :]

    f_pieces = jnp.concatenate(_split3(fcum), axis=1).astype(BF16)
    fxqa_ref[0] = (jnp.dot(f_pieces, pq_ref[...], preferred_element_type=F32) + cq_ref[...]).astype(BF16)
    fxka_ref[0] = (jnp.dot(f_pieces, pk_ref[...], preferred_element_type=F32) + ck_ref[...]).astype(BF16)


def _mixer_in(x, shift, scale, g, wts, seq_tile=512):
    b, s, d = x.shape
    ts = seq_tile
    assert s % ts == 0
    w = wts["cw"].shape[1]
    hw = wts["wsb"].shape[1] // 3
    n_pairs = hw // V7X_LANES
    vec = pl.BlockSpec((1, 1, d), lambda i, j: (i, 0, 0))
    full = lambda a: pl.BlockSpec(a.shape, lambda i, j: (0,) * a.ndim)
    names = ("wlru", "wsb", "wfox", "wf", "bf", "cw", "cb", "wgate", "bgate", "lam", "gq", "gk", "gmix", "hsum",
             "pq", "cq", "pk", "ck")
    consts = [wts[k] for k in names]
    seq_out = lambda width: pl.BlockSpec((1, ts, width), lambda i, j: (i, j, 0))
    widths = (w, hw, hw, hw, hw, hw, hw, n_pairs * V7X_LANES, n_pairs * V7X_LANES)
    return pl.pallas_call(
        _mixin_kernel,
        grid=(b, s // ts),
        in_specs=[pl.BlockSpec((1, ts, d), lambda i, j: (i, j, 0)), vec, vec, full(g)] + [full(a) for a in consts],
        out_specs=[seq_out(n) for n in widths],
        out_shape=[jax.ShapeDtypeStruct((b, s, n), BF16) for n in widths],
        scratch_shapes=[
            pltpu.VMEM((ts + V7X_SUBLANES, w), F32),
            pltpu.VMEM((ts, w), F32),
            pltpu.VMEM((ts, w), F32),
            pltpu.VMEM((ts, w), F32),
            pltpu.VMEM((V7X_SUBLANES, w), F32),
            pltpu.VMEM((V7X_SUBLANES, V7X_LANES), F32),
        ],
        compiler_params=_params("arbitrary", "arbitrary"),
        name="mixer_in",
    )(x, shift, scale, g, *consts)


def _head_lanes(h):
    pr, e = divmod(h, HEADS_PER_PAIR)
    return slice(pr * V7X_LANES, (pr + 1) * V7X_LANES), e * HEAD_DIM


def _neg_abs(x):
    bits = lax.bitcast_convert_type(x, jnp.uint32) | jnp.uint32(0x80000000)
    return lax.bitcast_convert_type(bits, F32)


def _sb_kernel(q_ref, k_ref, v_ref, o_ref, qm_scr, z_scr, w_scr, carry_scr, acc_scr):
    tq = q_ref.shape[1]
    tk = tq
    n_heads = q_ref.shape[2] // HEAD_DIM
    heads = range(n_heads)
    i = pl.program_id(1)
    lane = lax.broadcasted_iota(jnp.int32, (tq, V7X_LANES), 1)
    past = lax.broadcasted_iota(jnp.int32, (tq, tk), 1) < lax.broadcasted_iota(jnp.int32, (tq, tk), 0)
    strict = (lax.broadcasted_iota(jnp.int32, (tk, tk), 0) > lax.broadcasted_iota(jnp.int32, (tk, tk), 1)).astype(BF16)

    for h in heads:
        cols, lo_lane = _head_lanes(h)
        q_pair = q_ref[0, :, cols]
        qm_scr[h] = jnp.where((lane >= lo_lane) & (lane < lo_lane + HEAD_DIM), q_pair, jnp.zeros_like(q_pair))
    carry_scr[...] = jnp.zeros_like(carry_scr)
    acc_scr[...] = jnp.zeros_like(acc_scr)

    def score(h, j):
        k0 = pl.multiple_of(j * tk, tk)
        return lax.dot_general(qm_scr[h], k_ref[0, pl.ds(k0, tk), _head_lanes(h)[0]], (((1,), (1,)), ((), ())),
                               preferred_element_type=F32)

    def weighted_values(h, j):
        k0 = pl.multiple_of(j * tk, tk)
        v = v_ref[0, pl.ds(k0, tk), _head_lanes(h)[0]]
        acc_scr[h] += jnp.dot(w_scr[h], v, preferred_element_type=F32)

    def consume(masked, ahead, behind):
        log_beta, log_1mb, after = [], [], []
        for h in heads:
            z = z_scr[h].astype(BF16)
            lb = jnp.minimum(z, 0.0) - jnp.log(1.0 + jnp.exp(-jnp.abs(z)))
            log_beta.append(lb)
            log_1mb.append(jnp.where(past, lb - z, 0.0) if masked else lb - z)
            carry = jnp.concatenate([carry_scr[h]] * (tk // V7X_LANES), axis=1)
            after.append(carry + jnp.dot(log_1mb[h].astype(BF16), strict, preferred_element_type=F32))
            if behind is not None:
                weighted_values(h, behind)
            if ahead is not None:
                z_scr[h] = score(h, ahead)
        for h in heads:
            w_h = jnp.exp(log_beta[h] + after[h].astype(BF16))
            w_scr[h] = jnp.where(past, w_h, jnp.zeros_like(w_h)) if masked else w_h
            carry_scr[h] = jnp.broadcast_to(after[h][:, 0:1] + log_1mb[h][:, 0:1], (tq, V7X_LANES))

    for h in heads:
        z_scr[h] = score(h, i)
    consume(True, jnp.maximum(i - 1, 0), None)

    def middle(j):
        consume(False, j - 1, j + 1)

    n_middle = jnp.maximum(i - 1, 0)

    def two_tiles(pair, c):
        j = i - 1 - 2 * pair
        middle(j)
        middle(j - 1)
        return c

    lax.fori_loop(0, n_middle // 2, two_tiles, 0)

    @pl.when(n_middle % 2 == 1)
    def _():
        middle(1)

    @pl.when(i > 0)
    def _():
        consume(False, None, 1)

    for h in heads:
        weighted_values(h, 0)

    for h in range(0, n_heads, HEADS_PER_PAIR):
        cols, _ = _head_lanes(h)
        o_ref[0, :, cols] = jnp.where(lane < HEAD_DIM, acc_scr[h], acc_scr[h + 1])


def _sb_attention(q, k, v, tq=256):
    b, s, hw = q.shape
    n_heads = hw // HEAD_DIM
    assert s % tq == 0
    return pl.pallas_call(
        _sb_kernel,
        grid=(b, s // tq),
        in_specs=[
            pl.BlockSpec((1, tq, hw), lambda i, j: (i, j, 0)),
            pl.BlockSpec((1, s, hw), lambda i, j: (i, 0, 0)),
            pl.BlockSpec((1, s, hw), lambda i, j: (i, 0, 0)),
        ],
        out_specs=pl.BlockSpec((1, tq, hw), lambda i, j: (i, j, 0)),
        out_shape=jax.ShapeDtypeStruct((b, s, hw), F32),
        scratch_shapes=[
            pltpu.VMEM((n_heads, tq, V7X_LANES), BF16),
            pltpu.VMEM((n_heads, tq, tq), F32),
            pltpu.VMEM((n_heads, tq, tq), BF16),
            pltpu.VMEM((n_heads, tq, V7X_LANES), F32),
            pltpu.VMEM((n_heads, tq, V7X_LANES), F32),
        ],
        compiler_params=_params("arbitrary", "arbitrary"),
        name="sb_attention",
    )(q, k, v)


def _fox_kernel(q_ref, qa_ref, k_ref, ka_ref, v_ref, o_ref, qm_scr, z_scr, m_scr, l_scr, acc_scr, *, tk):
    tq = q_ref.shape[1]
    n_heads = q_ref.shape[2] // HEAD_DIM
    heads = range(n_heads)
    i = pl.program_id(1)
    last = (i * tq) // tk
    lane = lax.broadcasted_iota(jnp.int32, (tq, V7X_LANES), 1)
    col_minus_row = lax.broadcasted_iota(jnp.int32, (tq, tk), 1) - lax.broadcasted_iota(jnp.int32, (tq, tk), 0)
    visible = col_minus_row <= i * tq - last * tk
    ones = jnp.ones((tk, V7X_LANES), BF16)

    for h in heads:
        cols, lo_lane = _head_lanes(h)
        e = h % HEADS_PER_PAIR
        q_pair = q_ref[0, :, cols]
        qa_pair = qa_ref[0, :, cols]
        in_head = (lane >= lo_lane) & (lane < lo_lane + HEAD_DIM)
        in_aug = (lane >= e * AUG_PER_HEAD) & (lane < (e + 1) * AUG_PER_HEAD)
        qm_scr[h] = jnp.concatenate([jnp.where(in_head, q_pair, jnp.zeros_like(q_pair)),
                                     jnp.where(in_aug, qa_pair, jnp.zeros_like(qa_pair))], axis=1)
    m_scr[...] = jnp.full_like(m_scr, NEG_BIG)
    l_scr[...] = jnp.zeros_like(l_scr)
    acc_scr[...] = jnp.zeros_like(acc_scr)

    def score(h, j):
        k0 = pl.multiple_of(j * tk, tk)
        cols, _ = _head_lanes(h)
        k = jnp.concatenate([k_ref[0, pl.ds(k0, tk), cols], ka_ref[0, pl.ds(k0, tk), cols]], axis=1)
        return lax.dot_general(qm_scr[h], k, (((1,), (1,)), ((), ())), preferred_element_type=F32)

    def consume(j, masked, ahead):
        k0 = pl.multiple_of(j * tk, tk)
        for h in heads:
            z = jnp.where(visible, z_scr[h], -jnp.inf) if masked else z_scr[h]
            m_prev = m_scr[h]
            m_new = jnp.maximum(m_prev, jnp.max(z, axis=-1, keepdims=True))
            alpha = jnp.exp(m_prev - m_new)
            p = jnp.exp(z - jnp.concatenate([m_new] * (tk // V7X_LANES), axis=1)).astype(BF16)
            m_scr[h] = m_new
            v = jnp.concatenate([v_ref[0, pl.ds(k0, tk), _head_lanes(h)[0]], ones], axis=1)
            pv = jnp.dot(p, v, preferred_element_type=F32)
            if ahead is not None:
                z_scr[h] = score(h, ahead)
            l_scr[h] = alpha * l_scr[h] + pv[:, V7X_LANES:]
            acc_scr[h] = alpha * acc_scr[h] + pv[:, :V7X_LANES]

    for h in heads:
        z_scr[h] = score(h, 0)

    def step(t, c):
        consume(t, False, t + 1)
        return c

    lax.fori_loop(0, last, step, 0)
    consume(last, True, None)
    for h in range(0, n_heads, HEADS_PER_PAIR):
        cols, _ = _head_lanes(h)
        o_ref[0, :, cols] = jnp.where(lane < HEAD_DIM, acc_scr[h] / l_scr[h], acc_scr[h + 1] / l_scr[h + 1])


def _fox_attention(q, qa, k, ka, v, tq=512, tk=512):
    b, s, hw = q.shape
    n_heads = hw // HEAD_DIM
    assert s % tq == 0 and s % tk == 0 and tk % tq == 0
    qspec = pl.BlockSpec((1, tq, hw), lambda i, j: (i, j, 0))
    kspec = pl.BlockSpec((1, s, hw), lambda i, j: (i, 0, 0))
    return pl.pallas_call(
        functools.partial(_fox_kernel, tk=tk),
        grid=(b, s // tq),
        in_specs=[qspec, qspec, kspec, kspec, kspec],
        out_specs=pl.BlockSpec((1, tq, hw), lambda i, j: (i, j, 0)),
        out_shape=jax.ShapeDtypeStruct((b, s, hw), F32),
        scratch_shapes=[
            pltpu.VMEM((n_heads, tq, 2 * V7X_LANES), BF16),
            pltpu.VMEM((n_heads, tq, tk), F32),
            pltpu.VMEM((n_heads, tq, V7X_LANES), F32),
            pltpu.VMEM((n_heads, tq, V7X_LANES), F32),
            pltpu.VMEM((n_heads, tq, V7X_LANES), F32),
        ],
        compiler_params=_params("arbitrary", "arbitrary"),
        name="fox_attention",
    )(q, qa, k, ka, v)


def _mixout_kernel(x_ref, gate_ref, ylru_ref, ysb_ref, yfox_ref, gsb_ref, gfox_ref, wout_ref, o_ref):
    def group_norm(y, g):
        return (y * lax.rsqrt(jnp.mean(y * y, axis=-1, keepdims=True) + EPS) * g).astype(BF16)

    y = jnp.concatenate([ylru_ref[0], group_norm(ysb_ref[0], gsb_ref[...]), group_norm(yfox_ref[0], gfox_ref[...])],
                        axis=1)
    o_ref[0] = x_ref[0] + (1.0 + gate_ref[0]) * jnp.dot(y, wout_ref[...], preferred_element_type=F32)


def _mixer_out(x, gate, ylru, ysb, yfox, gsb, gfox, wout, seq_tile=1024):
    b, s, d = x.shape
    ts = seq_tile
    assert s % ts == 0
    seq = lambda a: pl.BlockSpec((1, ts, a.shape[2]), lambda i, j: (i, j, 0))
    full = lambda a: pl.BlockSpec(a.shape, lambda i, j: (0,) * a.ndim)
    return pl.pallas_call(
        _mixout_kernel,
        grid=(b, s // ts),
        in_specs=[seq(x), pl.BlockSpec((1, 1, d), lambda i, j: (i, 0, 0)), seq(ylru), seq(ysb), seq(yfox),
                  full(gsb), full(gfox), full(wout)],
        out_specs=seq(x),
        out_shape=jax.ShapeDtypeStruct((b, s, d), F32),
        compiler_params=_params("arbitrary", "arbitrary"),
        name="mixer_out",
    )(x, gate, ylru, ysb, yfox, gsb, gfox, wout)


def _block_diag(w):
    h, di, dj = w.shape
    return jnp.einsum("hij,hg->higj", w, jnp.eye(h, dtype=w.dtype)).reshape(h * di, h * dj)


def _diag_groups(bd):
    n = bd.shape[0]
    assert n % V7X_MXU_DIM == 0 and V7X_MXU_DIM % HEAD_DIM == 0
    n_grp = n // V7X_MXU_DIM
    blocks = bd.reshape(n_grp, V7X_MXU_DIM, n_grp, V7X_MXU_DIM)
    return jnp.stack([blocks[g, :, g, :] for g in range(n_grp)])


def _aug_placement(n_heads):
    n_out = (n_heads // HEADS_PER_PAIR) * V7X_LANES
    pq = np.zeros((3 * V7X_LANES, n_out), np.float32)
    pk = np.zeros((3 * V7X_LANES, n_out), np.float32)
    cq = np.zeros((1, n_out), np.float32)
    ck = np.zeros((1, n_out), np.float32)
    for h in range(n_heads):
        pr, e = divmod(h, HEADS_PER_PAIR)
        base = pr * V7X_LANES + e * AUG_PER_HEAD
        for j in range(3):
            pq[j * V7X_LANES + h, base + j] = 1.0
            cq[0, base + 3 + j] = 1.0
            ck[0, base + j] = 1.0
            pk[j * V7X_LANES + h, base + 3 + j] = -1.0
    return {"pq": jnp.asarray(pq, BF16), "cq": jnp.asarray(cq), "pk": jnp.asarray(pk, BF16), "ck": jnp.asarray(ck)}


def _mixer_weights(w_in, b_fgate, conv_w, conv_b, w_rgate, b_rgate, w_igate, b_igate, lru_lambda, g_qk, g_mix_out):
    w = conv_w.shape[1]
    n_heads = b_fgate.shape[0]
    hw = n_heads * HEAD_DIM
    o_sb = 2 * w
    o_fx = o_sb + 3 * hw
    o_f = o_fx + 3 * hw
    wf = jnp.zeros((w_in.shape[0], V7X_LANES), F32).at[:, :3 * n_heads].set(jnp.tile(w_in[:, o_f:o_f + n_heads], (1, 3)))
    bf = jnp.zeros((1, V7X_LANES), F32).at[0, :3 * n_heads].set(jnp.tile(b_fgate, 3))
    head_of = jnp.arange(hw) // HEAD_DIM
    hsum = (head_of[:, None] == head_of[None, :]).astype(BF16)
    return {
        "wlru": w_in[:, :o_sb].astype(BF16),
        "wsb": w_in[:, o_sb:o_fx].astype(BF16),
        "wfox": w_in[:, o_fx:o_f].astype(BF16),
        "wf": wf.astype(BF16),
        "bf": bf,
        "cw": conv_w,
        "cb": conv_b.reshape(1, w),
        "wgate": jnp.stack([_diag_groups(_block_diag(w_rgate)), _diag_groups(_block_diag(w_igate))]).astype(BF16),
        "bgate": jnp.concatenate([b_rgate, b_igate]).reshape(1, 2 * w),
        "lam": lru_lambda.reshape(1, w),
        "gq": jnp.tile(g_qk[0], n_heads).reshape(1, hw),
        "gk": jnp.tile(g_qk[1], n_heads).reshape(1, hw),
        "gmix": g_mix_out[:w].reshape(1, w),
        "hsum": hsum,
        **_aug_placement(n_heads),
    }


def kernel(x, c, w_ada, b_ada, g_norm, w_ffn_up, w_ffn_down, w_in, b_fgate, conv_w, conv_b, w_rgate, b_rgate,
           w_igate, b_igate, lru_lambda, g_qk, g_mix_out, w_out):
    b, s, d = x.shape
    depth = w_ada.shape[0]
    w = conv_w.shape[-1]
    hw = b_fgate.shape[-1] * HEAD_DIM

    mod = _adaln_mod(c, w_ada, b_ada).reshape(depth, b, N_SUB, 3, 1, d)

    for l in range(depth):
        shift = lambda j: mod[l, :, j, 0]
        scale = lambda j: mod[l, :, j, 1]
        gate = lambda j: mod[l, :, j, 2]
        gn = lambda j: g_norm[l, j].reshape(1, d)

        def ffn(xx, j, i):
            return _ffn(xx.reshape(b * s, d), shift(j), scale(j), gate(j), gn(j), w_ffn_up[l, i].astype(BF16),
                        w_ffn_down[l, i].astype(BF16), s).reshape(b, s, d)

        x = ffn(x, 0, 0)

        wts = _mixer_weights(w_in[l], b_fgate[l], conv_w[l], conv_b[l], w_rgate[l], b_rgate[l], w_igate[l],
                             b_igate[l], lru_lambda[l], g_qk[l], g_mix_out[l])
        ylru, sbq, sbk, sbv, fxq, fxk, fxv, fxqa, fxka = _mixer_in(x, shift(1), scale(1), gn(1), wts)
        ysb = _sb_attention(sbq, sbk, sbv)
        yfox = _fox_attention(fxq, fxqa, fxk, fxka, fxv)
        x = _mixer_out(x, gate(1), ylru, ysb, yfox, g_mix_out[l, w:w + hw].reshape(1, hw),
                       g_mix_out[l, w + hw:].reshape(1, hw), w_out[l].astype(BF16))

        x = ffn(x, 2, 1)
    return x
```

```python
import functools

import jax
import jax.numpy as jnp
import numpy as np
from jax import lax
from jax.experimental import pallas as pl
from jax.experimental.pallas import tpu as pltpu

F32 = jnp.float32
BF16 = jnp.bfloat16

HEAD_DIM = 64
CONV_WIDTH = 4
LRU_C = 8.0
N_SUB = 3
EPS = 1e-6

V7X_LANES = 128
V7X_SUBLANES = 8
V7X_MXU_DIM = 256
V7X_VMEM_BYTES = 64 * 1024 * 1024
VMEM_LIMIT_BYTES = V7X_VMEM_BYTES - 8 * 1024 * 1024

HEADS_PER_PAIR = V7X_LANES // HEAD_DIM
AUG_PER_HEAD = 6
NEG_BIG = -1e30


def _params(*sem):
    return pltpu.CompilerParams(dimension_semantics=sem, vmem_limit_bytes=VMEM_LIMIT_BYTES)


def _sigmoid(x):
    return 1.0 / (1.0 + jnp.exp(-x))


def _log_sigmoid(x):
    return jnp.minimum(x, 0.0) - jnp.log(1.0 + jnp.exp(-jnp.abs(x)))


def _split3(v):
    hi = v.astype(BF16).astype(F32)
    r = v - hi
    mid = r.astype(BF16).astype(F32)
    lo = (r - mid).astype(BF16).astype(F32)
    return hi, mid, lo


def _mod_norm(x, g, scale, shift):
    ms = jnp.mean(x * x, axis=-1, keepdims=True)
    return (x * lax.rsqrt(ms + EPS)) * (g * (1.0 + scale)) + shift


def _mod_kernel(c_ref, w_ref, b_ref, o_ref):
    c = c_ref[...]
    ca = c * _sigmoid(c)
    o_ref[0] = jnp.dot(ca.astype(BF16), w_ref[0].astype(BF16), preferred_element_type=F32) + b_ref[0]


def _adaln_mod(c, w_ada, b_ada):
    depth, d, n = w_ada.shape
    b = c.shape[0]
    tn = 1536
    assert n % tn == 0
    return pl.pallas_call(
        _mod_kernel,
        grid=(depth, n // tn),
        in_specs=[
            pl.BlockSpec((b, d), lambda l, j: (0, 0)),
            pl.BlockSpec((1, d, tn), lambda l, j: (l, 0, j)),
            pl.BlockSpec((1, 1, tn), lambda l, j: (l, 0, j)),
        ],
        out_specs=pl.BlockSpec((1, b, tn), lambda l, j: (l, 0, j)),
        out_shape=jax.ShapeDtypeStruct((depth, b, n), F32),
        compiler_params=_params("arbitrary", "arbitrary"),
        name="adaln_mod",
    )(c, w_ada, b_ada.reshape(depth, 1, n))


def _group_norm(y, g):
    return (y * lax.rsqrt(jnp.mean(y * y, axis=-1, keepdims=True) + EPS) * g).astype(BF16)


def _ffn_kernel(*refs, after_mixer):
    if after_mixer:
        x_ref, mgate_ref, ylru_ref, ysb_ref, yfox_ref, gsb_ref, gfox_ref, wout_ref, *refs = refs
    else:
        x_ref, *refs = refs
    shift_ref, scale_ref, gate_ref, g_ref, wup_ref, wd_ref, o_ref, h_scr, a_scr, acc_scr = refs
    ff = wd_ref.shape[0]
    tf = a_scr.shape[1]
    n_chunks = ff // tf
    if after_mixer:
        y = jnp.concatenate([ylru_ref[...], _group_norm(ysb_ref[...], gsb_ref[...]),
                             _group_norm(yfox_ref[...], gfox_ref[...])], axis=1)
        o_ref[...] = x_ref[...] + (1.0 + mgate_ref[0]) * jnp.dot(y, wout_ref[...], preferred_element_type=F32)
        x_ref = o_ref
    h_scr[...] = _mod_norm(x_ref[...], g_ref[...], scale_ref[0], shift_ref[0]).astype(BF16)

    def up(c):
        hb = h_scr[...]
        off = pl.multiple_of(c * tf, tf)
        g = jnp.dot(hb, wup_ref[:, pl.ds(off, tf)], preferred_element_type=F32)
        u = jnp.dot(hb, wup_ref[:, pl.ds(off + ff, tf)], preferred_element_type=F32)
        return ((g * _sigmoid(g)) * u).astype(BF16)

    def down(c):
        return jnp.dot(a_scr[...], wd_ref[pl.ds(pl.multiple_of(c * tf, tf), tf), :], preferred_element_type=F32)

    a_scr[...] = up(0)
    d = down(0)
    a_next = up(1)
    acc_scr[...] = d
    a_scr[...] = a_next

    def chunk(c, carry):
        d = down(c)
        a_next = up(c + 1)
        acc_scr[...] += d
        a_scr[...] = a_next
        return carry

    lax.fori_loop(1, n_chunks - 1, chunk, 0, unroll=3)
    o_ref[...] = x_ref[...] + (0.5 * (1.0 + gate_ref[0])) * (acc_scr[...] + down(n_chunks - 1))


def _ffn(x2, shift, scale, gate, g, wup, wd, seq, mixer=None, tf=256):
    m, d = x2.shape
    tm = 1024
    assert m % tm == 0 and seq % tm == 0 and wd.shape[0] % tf == 0 and wup.shape[1] == 2 * wd.shape[0]
    per_seq = seq // tm
    vec = pl.BlockSpec((1, 1, d), lambda i: (i // per_seq, 0, 0))
    full = lambda a: pl.BlockSpec(a.shape, lambda i: (0,) * a.ndim)
    rows = lambda a: pl.BlockSpec((tm, a.shape[1]), lambda i: (i, 0))
    mixer_args, mixer_specs = (), []
    if mixer is not None:
        mgate, ylru, ysb, yfox, gsb, gfox, wout = mixer
        mixer_args = (mgate, ylru, ysb, yfox, gsb, gfox, wout)
        mixer_specs = [vec, rows(ylru), rows(ysb), rows(yfox), full(gsb), full(gfox), full(wout)]
    return pl.pallas_call(
        functools.partial(_ffn_kernel, after_mixer=mixer is not None),
        grid=(m // tm,),
        in_specs=[rows(x2)] + mixer_specs + [vec, vec, vec, full(g), full(wup), full(wd)],
        out_specs=pl.BlockSpec((tm, d), lambda i: (i, 0)),
        out_shape=jax.ShapeDtypeStruct((m, d), F32),
        scratch_shapes=[
            pltpu.VMEM((tm, d), BF16),
            pltpu.VMEM((tm, tf), BF16),
            pltpu.VMEM((tm, d), F32),
        ],
        compiler_params=_params("arbitrary"),
        name="ffn",
    )(x2, *mixer_args, shift, scale, gate, g, wup, wd)


def _mixin_kernel(x_ref, shift_ref, scale_ref, g_ref, wlru_ref, wsb_ref, wfox_ref, wf_ref, bf_ref,
                  cw_ref, cb_ref, wgate_ref, bgate_ref, lam_ref, gq_ref, gk_ref, gmix_ref, hsum_ref,
                  pq_ref, cq_ref, pk_ref, ck_ref,
                  ylru_ref, sbq_ref, sbk_ref, sbv_ref, fxq_ref, fxk_ref, fxv_ref, fxqa_ref, fxka_ref,
                  xp_scr, a_scr, b_scr, hs_scr, hc_scr, fc_scr):
    ts = x_ref.shape[1]
    w = cw_ref.shape[1]
    n_heads = fxq_ref.shape[2] // HEAD_DIM
    pad = V7X_SUBLANES

    @pl.when(pl.program_id(1) == 0)
    def _():
        xp_scr[0:pad, :] = jnp.zeros((pad, w), F32)
        hc_scr[...] = jnp.zeros_like(hc_scr)
        fc_scr[...] = jnp.zeros_like(fc_scr)

    h = _mod_norm(x_ref[0], g_ref[...], scale_ref[0], shift_ref[0]).astype(BF16)

    p = jnp.dot(h, wlru_ref[...], preferred_element_type=F32)
    lru_g = p[:, w:]
    xp_scr[pad:pad + ts, :] = p[:, :w]
    u = cb_ref[...]
    for k in range(CONV_WIDTH):
        off = pad - (CONV_WIDTH - 1) + k
        u = u + cw_ref[k:k + 1, :] * xp_scr[off:off + ts, :]
    xp_scr[0:pad, :] = xp_scr[ts:ts + pad, :]

    ub = u.astype(BF16)

    def gate(which):
        n_grp, grp = wgate_ref.shape[1], wgate_ref.shape[2]
        pre = [jnp.dot(ub[:, g * grp:(g + 1) * grp], wgate_ref[which, g], preferred_element_type=F32)
               for g in range(n_grp)]
        return _sigmoid(jnp.concatenate(pre, axis=1) + bgate_ref[:, which * w:(which + 1) * w])

    r = gate(0)
    ig = gate(1)
    lam = lam_ref[...]
    softplus_neg_lam = jnp.maximum(-lam, 0.0) + jnp.log1p(jnp.exp(-jnp.abs(lam)))
    a = jnp.exp((-LRU_C * softplus_neg_lam) * r)
    a_scr[...] = a
    s = 1.0 - a * a
    b_scr[...] = jnp.where(s > 0.0, s * lax.rsqrt(s), 0.0) * (ig * u)

    row8 = lax.broadcasted_iota(jnp.int32, (V7X_SUBLANES, w), 0)

    def scan8(i, hprev):
        r0 = pl.multiple_of(i * V7X_SUBLANES, V7X_SUBLANES)
        aa = a_scr[pl.ds(r0, V7X_SUBLANES), :]
        bb = b_scr[pl.ds(r0, V7X_SUBLANES), :]
        for sh in (1, 2, 4):
            a_s = pltpu.roll(aa, sh, 0)
            b_s = pltpu.roll(bb, sh, 0)
            m = row8 >= sh
            bb = jnp.where(m, aa * b_s + bb, bb)
            aa = jnp.where(m, aa * a_s, aa)
        hh = aa * hprev + bb
        hs_scr[pl.ds(r0, V7X_SUBLANES), :] = hh
        return hh[V7X_SUBLANES - 1:V7X_SUBLANES, :]

    hc_scr[0:1, :] = lax.fori_loop(0, ts // V7X_SUBLANES, scan8, hc_scr[0:1, :], unroll=True)

    y = hs_scr[...] * jax.nn.gelu(lru_g)
    yn = y * lax.rsqrt(jnp.mean(y * y, axis=-1, keepdims=True) + EPS) * gmix_ref[...]
    ylru_ref[0] = yn.astype(BF16)

    scale = HEAD_DIM ** -0.5
    psb = jnp.dot(h, wsb_ref[...], preferred_element_type=F32)
    hw = n_heads * HEAD_DIM
    sbq_ref[0] = (psb[:, :hw] * scale).astype(BF16)
    sbk_ref[0] = psb[:, hw:2 * hw].astype(BF16)
    sbv_ref[0] = psb[:, 2 * hw:].astype(BF16)

    pfx = jnp.dot(h, wfox_ref[...], preferred_element_type=F32)

    def head_rms(t):
        ssq = jnp.dot((t * t).astype(BF16), hsum_ref[...], preferred_element_type=F32)
        return t * lax.rsqrt(ssq * (1.0 / HEAD_DIM) + EPS)

    fxq_ref[0] = (head_rms(pfx[:, :hw]) * (gq_ref[...] * scale)).astype(BF16)
    fxk_ref[0] = (head_rms(pfx[:, hw:2 * hw]) * gk_ref[...]).astype(BF16)
    fxv_ref[0] = pfx[:, 2 * hw:].astype(BF16)

    lane = lax.broadcasted_iota(jnp.int32, (ts, V7X_LANES), 1)
    logf = _log_sigmoid(jnp.dot(h, wf_ref[...], preferred_element_type=F32) + bf_ref[...])
    logf = jnp.where(lane < 3 * n_heads, logf, 0.0)
    hi, mid, lo = _split3(logf)
    pieces = jnp.where(lane < n_heads, hi, jnp.where(lane < 2 * n_heads, mid, lo)).astype(BF16)
    tri = (lax.broadcasted_iota(jnp.int32, (ts, ts), 0) >= lax.broadcasted_iota(jnp.int32, (ts, ts), 1))
    cs = jnp.dot(tri.astype(BF16), pieces, preferred_element_type=F32)
    cs = cs + pltpu.roll(cs, V7X_LANES - n_heads, 1) + pltpu.roll(cs, V7X_LANES - 2 * n_heads, 1)
    fcum = cs + fc_scr[0:1, :]
    fc_scr[0:1, :] = fcum[ts - 1:ts, :]

    f_pieces = jnp.concatenate(_split3(fcum), axis=1).astype(BF16)
    fxqa_ref[0] = (jnp.dot(f_pieces, pq_ref[...], preferred_element_type=F32) + cq_ref[...]).astype(BF16)
    fxka_ref[0] = (jnp.dot(f_pieces, pk_ref[...], preferred_element_type=F32) + ck_ref[...]).astype(BF16)


def _mixer_in(x, shift, scale, g, wts, seq_tile=512):
    b, s, d = x.shape
    ts = seq_tile
    assert s % ts == 0
    w = wts["cw"].shape[1]
    hw = wts["wsb"].shape[1] // 3
    n_pairs = hw // V7X_LANES
    vec = pl.BlockSpec((1, 1, d), lambda i, j: (i, 0, 0))
    full = lambda a: pl.BlockSpec(a.shape, lambda i, j: (0,) * a.ndim)
    names = ("wlru", "wsb", "wfox", "wf", "bf", "cw", "cb", "wgate", "bgate", "lam", "gq", "gk", "gmix", "hsum",
             "pq", "cq", "pk", "ck")
    consts = [wts[k] for k in names]
    seq_out = lambda width: pl.BlockSpec((1, ts, width), lambda i, j: (i, j, 0))
    widths = (w, hw, hw, hw, hw, hw, hw, n_pairs * V7X_LANES, n_pairs * V7X_LANES)
    return pl.pallas_call(
        _mixin_kernel,
        grid=(b, s // ts),
        in_specs=[pl.BlockSpec((1, ts, d), lambda i, j: (i, j, 0)), vec, vec, full(g)] + [full(a) for a in consts],
        out_specs=[seq_out(n) for n in widths],
        out_shape=[jax.ShapeDtypeStruct((b, s, n), BF16) for n in widths],
        scratch_shapes=[
            pltpu.VMEM((ts + V7X_SUBLANES, w), F32),
            pltpu.VMEM((ts, w), F32),
            pltpu.VMEM((ts, w), F32),
            pltpu.VMEM((ts, w), F32),
            pltpu.VMEM((V7X_SUBLANES, w), F32),
            pltpu.VMEM((V7X_SUBLANES, V7X_LANES), F32),
        ],
        compiler_params=_params("arbitrary", "arbitrary"),
        name="mixer_in",
    )(x, shift, scale, g, *consts)


def _head_lanes(h):
    pr, e = divmod(h, HEADS_PER_PAIR)
    return slice(pr * V7X_LANES, (pr + 1) * V7X_LANES), e * HEAD_DIM


def _neg_abs(x):
    bits = lax.bitcast_convert_type(x, jnp.uint32) | jnp.uint32(0x80000000)
    return lax.bitcast_convert_type(bits, F32)


def _sb_kernel(q_ref, k_ref, v_ref, o_ref, qm_scr, z_scr, w_scr, carry_scr, acc_scr):
    tq = q_ref.shape[1]
    tk = tq
    n_heads = q_ref.shape[2] // HEAD_DIM
    heads = range(n_heads)
    i = pl.program_id(1)
    lane = lax.broadcasted_iota(jnp.int32, (tq, V7X_LANES), 1)
    past = lax.broadcasted_iota(jnp.int32, (tq, tk), 1) < lax.broadcasted_iota(jnp.int32, (tq, tk), 0)
    strict = (lax.broadcasted_iota(jnp.int32, (tk, tk), 0) > lax.broadcasted_iota(jnp.int32, (tk, tk), 1)).astype(BF16)

    for h in heads:
        cols, lo_lane = _head_lanes(h)
        q_pair = q_ref[0, :, cols]
        qm_scr[h] = jnp.where((lane >= lo_lane) & (lane < lo_lane + HEAD_DIM), q_pair, jnp.zeros_like(q_pair))
    carry_scr[...] = jnp.zeros_like(carry_scr)
    acc_scr[...] = jnp.zeros_like(acc_scr)

    def score(h, j):
        k0 = pl.multiple_of(j * tk, tk)
        return lax.dot_general(qm_scr[h], k_ref[0, pl.ds(k0, tk), _head_lanes(h)[0]], (((1,), (1,)), ((), ())),
                               preferred_element_type=F32)

    def weighted_values(h, j):
        k0 = pl.multiple_of(j * tk, tk)
        v = v_ref[0, pl.ds(k0, tk), _head_lanes(h)[0]]
        acc_scr[h] += jnp.dot(w_scr[h], v, preferred_element_type=F32)

    def consume(masked, ahead, behind):
        log_beta, log_1mb, after = [], [], []
        for h in heads:
            z = z_scr[h].astype(BF16)
            lb = jnp.minimum(z, 0.0) - jnp.log(1.0 + jnp.exp(-jnp.abs(z)))
            log_beta.append(lb)
            log_1mb.append(jnp.where(past, lb - z, 0.0) if masked else lb - z)
            carry = jnp.concatenate([carry_scr[h]] * (tk // V7X_LANES), axis=1)
            after.append(carry + jnp.dot(log_1mb[h].astype(BF16), strict, preferred_element_type=F32))
            if behind is not None:
                weighted_values(h, behind)
            if ahead is not None:
                z_scr[h] = score(h, ahead)
        for h in heads:
            w_h = jnp.exp(log_beta[h] + after[h].astype(BF16))
            w_scr[h] = jnp.where(past, w_h, jnp.zeros_like(w_h)) if masked else w_h
            carry_scr[h] = jnp.broadcast_to(after[h][:, 0:1] + log_1mb[h][:, 0:1], (tq, V7X_LANES))

    for h in heads:
        z_scr[h] = score(h, i)
    consume(True, jnp.maximum(i - 1, 0), None)

    def middle(j):
        consume(False, j - 1, j + 1)

    n_middle = jnp.maximum(i - 1, 0)

    def two_tiles(pair, c):
        j = i - 1 - 2 * pair
        middle(j)
        middle(j - 1)
        return c

    lax.fori_loop(0, n_middle // 2, two_tiles, 0)

    @pl.when(n_middle % 2 == 1)
    def _():
        middle(1)

    @pl.when(i > 0)
    def _():
        consume(False, None, 1)

    for h in heads:
        weighted_values(h, 0)

    for h in range(0, n_heads, HEADS_PER_PAIR):
        cols, _ = _head_lanes(h)
        o_ref[0, :, cols] = jnp.where(lane < HEAD_DIM, acc_scr[h], acc_scr[h + 1])


def _sb_attention(q, k, v, tq=256):
    b, s, hw = q.shape
    n_heads = hw // HEAD_DIM
    assert s % tq == 0
    return pl.pallas_call(
        _sb_kernel,
        grid=(b, s // tq),
        in_specs=[
            pl.BlockSpec((1, tq, hw), lambda i, j: (i, j, 0)),
            pl.BlockSpec((1, s, hw), lambda i, j: (i, 0, 0)),
            pl.BlockSpec((1, s, hw), lambda i, j: (i, 0, 0)),
        ],
        out_specs=pl.BlockSpec((1, tq, hw), lambda i, j: (i, j, 0)),
        out_shape=jax.ShapeDtypeStruct((b, s, hw), F32),
        scratch_shapes=[
            pltpu.VMEM((n_heads, tq, V7X_LANES), BF16),
            pltpu.VMEM((n_heads, tq, tq), F32),
            pltpu.VMEM((n_heads, tq, tq), BF16),
            pltpu.VMEM((n_heads, tq, V7X_LANES), F32),
            pltpu.VMEM((n_heads, tq, V7X_LANES), F32),
        ],
        compiler_params=_params("arbitrary", "arbitrary"),
        name="sb_attention",
    )(q, k, v)


def _fox_kernel(q_ref, qa_ref, k_ref, ka_ref, v_ref, o_ref, qm_scr, z_scr, m_scr, l_scr, acc_scr, *, tk):
    tq = q_ref.shape[1]
    n_heads = q_ref.shape[2] // HEAD_DIM
    heads = range(n_heads)
    i = pl.program_id(1)
    last = (i * tq) // tk
    lane = lax.broadcasted_iota(jnp.int32, (tq, V7X_LANES), 1)
    col_minus_row = lax.broadcasted_iota(jnp.int32, (tq, tk), 1) - lax.broadcasted_iota(jnp.int32, (tq, tk), 0)
    visible = col_minus_row <= i * tq - last * tk
    ones = jnp.ones((tk, V7X_LANES), BF16)

    for h in heads:
        cols, lo_lane = _head_lanes(h)
        e = h % HEADS_PER_PAIR
        q_pair = q_ref[0, :, cols]
        qa_pair = qa_ref[0, :, cols]
        in_head = (lane >= lo_lane) & (lane < lo_lane + HEAD_DIM)
        in_aug = (lane >= e * AUG_PER_HEAD) & (lane < (e + 1) * AUG_PER_HEAD)
        qm_scr[h] = jnp.concatenate([jnp.where(in_head, q_pair, jnp.zeros_like(q_pair)),
                                     jnp.where(in_aug, qa_pair, jnp.zeros_like(qa_pair))], axis=1)
    m_scr[...] = jnp.full_like(m_scr, NEG_BIG)
    l_scr[...] = jnp.zeros_like(l_scr)
    acc_scr[...] = jnp.zeros_like(acc_scr)

    def score(h, j):
        k0 = pl.multiple_of(j * tk, tk)
        cols, _ = _head_lanes(h)
        k = jnp.concatenate([k_ref[0, pl.ds(k0, tk), cols], ka_ref[0, pl.ds(k0, tk), cols]], axis=1)
        return lax.dot_general(qm_scr[h], k, (((1,), (1,)), ((), ())), preferred_element_type=F32)

    def consume(j, masked, ahead):
        k0 = pl.multiple_of(j * tk, tk)
        for h in heads:
            z = jnp.where(visible, z_scr[h], -jnp.inf) if masked else z_scr[h]
            m_prev = m_scr[h]
            m_new = jnp.maximum(m_prev, jnp.max(z, axis=-1, keepdims=True))
            alpha = jnp.exp(m_prev - m_new)
            p = jnp.exp(z - jnp.concatenate([m_new] * (tk // V7X_LANES), axis=1)).astype(BF16)
            m_scr[h] = m_new
            v = jnp.concatenate([v_ref[0, pl.ds(k0, tk), _head_lanes(h)[0]], ones], axis=1)
            pv = jnp.dot(p, v, preferred_element_type=F32)
            if ahead is not None:
                z_scr[h] = score(h, ahead)
            l_scr[h] = alpha * l_scr[h] + pv[:, V7X_LANES:]
            acc_scr[h] = alpha * acc_scr[h] + pv[:, :V7X_LANES]

    for h in heads:
        z_scr[h] = score(h, 0)

    def step(t, c):
        consume(t, False, t + 1)
        return c

    lax.fori_loop(0, last, step, 0)
    consume(last, True, None)
    for h in range(0, n_heads, HEADS_PER_PAIR):
        cols, _ = _head_lanes(h)
        o_ref[0, :, cols] = jnp.where(lane < HEAD_DIM, acc_scr[h] / l_scr[h], acc_scr[h + 1] / l_scr[h + 1])


def _fox_attention(q, qa, k, ka, v, tq=512, tk=512):
    b, s, hw = q.shape
    n_heads = hw // HEAD_DIM
    assert s % tq == 0 and s % tk == 0 and tk % tq == 0
    qspec = pl.BlockSpec((1, tq, hw), lambda i, j: (i, j, 0))
    kspec = pl.BlockSpec((1, s, hw), lambda i, j: (i, 0, 0))
    return pl.pallas_call(
        functools.partial(_fox_kernel, tk=tk),
        grid=(b, s // tq),
        in_specs=[qspec, qspec, kspec, kspec, kspec],
        out_specs=pl.BlockSpec((1, tq, hw), lambda i, j: (i, j, 0)),
        out_shape=jax.ShapeDtypeStruct((b, s, hw), F32),
        scratch_shapes=[
            pltpu.VMEM((n_heads, tq, 2 * V7X_LANES), BF16),
            pltpu.VMEM((n_heads, tq, tk), F32),
            pltpu.VMEM((n_heads, tq, V7X_LANES), F32),
            pltpu.VMEM((n_heads, tq, V7X_LANES), F32),
            pltpu.VMEM((n_heads, tq, V7X_LANES), F32),
        ],
        compiler_params=_params("arbitrary", "arbitrary"),
        name="fox_attention",
    )(q, qa, k, ka, v)


def _block_diag(w):
    h, di, dj = w.shape
    return jnp.einsum("hij,hg->higj", w, jnp.eye(h, dtype=w.dtype)).reshape(h * di, h * dj)


def _diag_groups(bd):
    n = bd.shape[0]
    assert n % V7X_MXU_DIM == 0 and V7X_MXU_DIM % HEAD_DIM == 0
    n_grp = n // V7X_MXU_DIM
    blocks = bd.reshape(n_grp, V7X_MXU_DIM, n_grp, V7X_MXU_DIM)
    return jnp.stack([blocks[g, :, g, :] for g in range(n_grp)])


def _aug_placement(n_heads):
    n_out = (n_heads // HEADS_PER_PAIR) * V7X_LANES
    pq = np.zeros((3 * V7X_LANES, n_out), np.float32)
    pk = np.zeros((3 * V7X_LANES, n_out), np.float32)
    cq = np.zeros((1, n_out), np.float32)
    ck = np.zeros((1, n_out), np.float32)
    for h in range(n_heads):
        pr, e = divmod(h, HEADS_PER_PAIR)
        base = pr * V7X_LANES + e * AUG_PER_HEAD
        for j in range(3):
            pq[j * V7X_LANES + h, base + j] = 1.0
            cq[0, base + 3 + j] = 1.0
            ck[0, base + j] = 1.0
            pk[j * V7X_LANES + h, base + 3 + j] = -1.0
    return {"pq": jnp.asarray(pq, BF16), "cq": jnp.asarray(cq), "pk": jnp.asarray(pk, BF16), "ck": jnp.asarray(ck)}


def _mixer_weights(w_in, b_fgate, conv_w, conv_b, w_rgate, b_rgate, w_igate, b_igate, lru_lambda, g_qk, g_mix_out):
    w = conv_w.shape[1]
    n_heads = b_fgate.shape[0]
    hw = n_heads * HEAD_DIM
    o_sb = 2 * w
    o_fx = o_sb + 3 * hw
    o_f = o_fx + 3 * hw
    wf = jnp.zeros((w_in.shape[0], V7X_LANES), F32).at[:, :3 * n_heads].set(jnp.tile(w_in[:, o_f:o_f + n_heads], (1, 3)))
    bf = jnp.zeros((1, V7X_LANES), F32).at[0, :3 * n_heads].set(jnp.tile(b_fgate, 3))
    head_of = jnp.arange(hw) // HEAD_DIM
    hsum = (head_of[:, None] == head_of[None, :]).astype(BF16)
    return {
        "wlru": w_in[:, :o_sb].astype(BF16),
        "wsb": w_in[:, o_sb:o_fx].astype(BF16),
        "wfox": w_in[:, o_fx:o_f].astype(BF16),
        "wf": wf.astype(BF16),
        "bf": bf,
        "cw": conv_w,
        "cb": conv_b.reshape(1, w),
        "wgate": jnp.stack([_diag_groups(_block_diag(w_rgate)), _diag_groups(_block_diag(w_igate))]).astype(BF16),
        "bgate": jnp.concatenate([b_rgate, b_igate]).reshape(1, 2 * w),
        "lam": lru_lambda.reshape(1, w),
        "gq": jnp.tile(g_qk[0], n_heads).reshape(1, hw),
        "gk": jnp.tile(g_qk[1], n_heads).reshape(1, hw),
        "gmix": g_mix_out[:w].reshape(1, w),
        "hsum": hsum,
        **_aug_placement(n_heads),
    }


def kernel(x, c, w_ada, b_ada, g_norm, w_ffn_up, w_ffn_down, w_in, b_fgate, conv_w, conv_b, w_rgate, b_rgate,
           w_igate, b_igate, lru_lambda, g_qk, g_mix_out, w_out):
    b, s, d = x.shape
    depth = w_ada.shape[0]
    w = conv_w.shape[-1]
    hw = b_fgate.shape[-1] * HEAD_DIM

    mod = _adaln_mod(c, w_ada, b_ada).reshape(depth, b, N_SUB, 3, 1, d)

    for l in range(depth):
        shift = lambda j: mod[l, :, j, 0]
        scale = lambda j: mod[l, :, j, 1]
        gate = lambda j: mod[l, :, j, 2]
        gn = lambda j: g_norm[l, j].reshape(1, d)

        def ffn(xx, j, i, mixer=None):
            return _ffn(xx.reshape(b * s, d), shift(j), scale(j), gate(j), gn(j), w_ffn_up[l, i].astype(BF16),
                        w_ffn_down[l, i].astype(BF16), s, mixer).reshape(b, s, d)

        x = ffn(x, 0, 0)

        wts = _mixer_weights(w_in[l], b_fgate[l], conv_w[l], conv_b[l], w_rgate[l], b_rgate[l], w_igate[l],
                             b_igate[l], lru_lambda[l], g_qk[l], g_mix_out[l])
        ylru, sbq, sbk, sbv, fxq, fxk, fxv, fxqa, fxka = _mixer_in(x, shift(1), scale(1), gn(1), wts)
        ysb = _sb_attention(sbq, sbk, sbv)
        yfox = _fox_attention(fxq, fxqa, fxk, fxka, fxv)
        rows = lambda a: a.reshape(b * s, a.shape[-1])
        x = ffn(x, 2, 1, mixer=(gate(1), rows(ylru), rows(ysb), rows(yfox), g_mix_out[l, w:w + hw].reshape(1, hw),
                                g_mix_out[l, w + hw:].reshape(1, hw), w_out[l].astype(BF16)))
    return x
```

```python
import functools

import jax
import jax.numpy as jnp
import numpy as np
from jax import lax
from jax.experimental import pallas as pl
from jax.experimental.pallas import tpu as pltpu

F32 = jnp.float32
BF16 = jnp.bfloat16

HEAD_DIM = 64
CONV_WIDTH = 4
LRU_C = 8.0
N_SUB = 3
EPS = 1e-6

V7X_LANES = 128
V7X_SUBLANES = 8
V7X_MXU_DIM = 256
V7X_VMEM_BYTES = 64 * 1024 * 1024
VMEM_LIMIT_BYTES = V7X_VMEM_BYTES - 8 * 1024 * 1024

ADALN_COLS = 1536
FFN_ROWS = 1024
FFN_FF_CHUNK = V7X_MXU_DIM
MIXER_ROWS = 512
SB_TILE = V7X_MXU_DIM
FOX_Q_TILE = 512
FOX_K_TILE = 512

HEADS_PER_PAIR = V7X_LANES // HEAD_DIM
AUG_PER_HEAD = 6
NEG_BIG = -1e30


def _params(*sem):
    return pltpu.CompilerParams(dimension_semantics=sem, vmem_limit_bytes=VMEM_LIMIT_BYTES)


def _sigmoid(x):
    return 1.0 / (1.0 + jnp.exp(-x))


def _log_sigmoid(x):
    return jnp.minimum(x, 0.0) - jnp.log(1.0 + jnp.exp(-jnp.abs(x)))


def _split3(v):
    hi = v.astype(BF16).astype(F32)
    r = v - hi
    mid = r.astype(BF16).astype(F32)
    lo = (r - mid).astype(BF16).astype(F32)
    return hi, mid, lo


def _mod_norm(x, g, scale, shift):
    ms = jnp.mean(x * x, axis=-1, keepdims=True)
    return (x * lax.rsqrt(ms + EPS)) * (g * (1.0 + scale)) + shift


def _mod_kernel(c_ref, w_ref, b_ref, o_ref):
    c = c_ref[...]
    ca = c * _sigmoid(c)
    o_ref[0] = jnp.dot(ca.astype(BF16), w_ref[0].astype(BF16), preferred_element_type=F32) + b_ref[0]


def _adaln_mod(c, w_ada, b_ada):
    depth, d, n = w_ada.shape
    b = c.shape[0]
    tn = ADALN_COLS
    assert n % tn == 0
    return pl.pallas_call(
        _mod_kernel,
        grid=(depth, n // tn),
        in_specs=[
            pl.BlockSpec((b, d), lambda l, j: (0, 0)),
            pl.BlockSpec((1, d, tn), lambda l, j: (l, 0, j)),
            pl.BlockSpec((1, 1, tn), lambda l, j: (l, 0, j)),
        ],
        out_specs=pl.BlockSpec((1, b, tn), lambda l, j: (l, 0, j)),
        out_shape=jax.ShapeDtypeStruct((depth, b, n), F32),
        compiler_params=_params("arbitrary", "arbitrary"),
        name="adaln_mod",
    )(c, w_ada, b_ada.reshape(depth, 1, n))


def _group_norm(y, g):
    return (y * lax.rsqrt(jnp.mean(y * y, axis=-1, keepdims=True) + EPS) * g).astype(BF16)


def _ffn_kernel(*refs, after_mixer):
    if after_mixer:
        x_ref, mgate_ref, ylru_ref, ysb_ref, yfox_ref, gsb_ref, gfox_ref, wout_ref, *refs = refs
    else:
        x_ref, *refs = refs
    shift_ref, scale_ref, gate_ref, g_ref, wup_ref, wd_ref, o_ref, h_scr, a_scr, acc_scr = refs
    ff = wd_ref.shape[0]
    tf = a_scr.shape[1]
    n_chunks = ff // tf
    if after_mixer:
        y = jnp.concatenate([ylru_ref[...], _group_norm(ysb_ref[...], gsb_ref[...]),
                             _group_norm(yfox_ref[...], gfox_ref[...])], axis=1)
        o_ref[...] = x_ref[...] + (1.0 + mgate_ref[0]) * jnp.dot(y, wout_ref[...], preferred_element_type=F32)
        x_ref = o_ref
    h_scr[...] = _mod_norm(x_ref[...], g_ref[...], scale_ref[0], shift_ref[0]).astype(BF16)

    def up(c):
        hb = h_scr[...]
        off = pl.multiple_of(c * tf, tf)
        g = jnp.dot(hb, wup_ref[:, pl.ds(off, tf)], preferred_element_type=F32)
        u = jnp.dot(hb, wup_ref[:, pl.ds(off + ff, tf)], preferred_element_type=F32)
        return ((g * _sigmoid(g)) * u).astype(BF16)

    def down(c):
        return jnp.dot(a_scr[...], wd_ref[pl.ds(pl.multiple_of(c * tf, tf), tf), :], preferred_element_type=F32)

    a_scr[...] = up(0)
    d = down(0)
    a_next = up(1)
    acc_scr[...] = d
    a_scr[...] = a_next

    def chunk(c, carry):
        d = down(c)
        a_next = up(c + 1)
        acc_scr[...] += d
        a_scr[...] = a_next
        return carry

    lax.fori_loop(1, n_chunks - 1, chunk, 0, unroll=3)
    o_ref[...] = x_ref[...] + (0.5 * (1.0 + gate_ref[0])) * (acc_scr[...] + down(n_chunks - 1))


def _ffn(x2, shift, scale, gate, g, wup, wd, seq, mixer=None):
    m, d = x2.shape
    tm, tf = FFN_ROWS, FFN_FF_CHUNK
    assert m % tm == 0 and seq % tm == 0 and wd.shape[0] % tf == 0 and wup.shape[1] == 2 * wd.shape[0]
    per_seq = seq // tm
    vec = pl.BlockSpec((1, 1, d), lambda i: (i // per_seq, 0, 0))
    full = lambda a: pl.BlockSpec(a.shape, lambda i: (0,) * a.ndim)
    rows = lambda a: pl.BlockSpec((tm, a.shape[1]), lambda i: (i, 0))
    mixer_args, mixer_specs = (), []
    if mixer is not None:
        mgate, ylru, ysb, yfox, gsb, gfox, wout = mixer
        mixer_args = (mgate, ylru, ysb, yfox, gsb, gfox, wout)
        mixer_specs = [vec, rows(ylru), rows(ysb), rows(yfox), full(gsb), full(gfox), full(wout)]
    return pl.pallas_call(
        functools.partial(_ffn_kernel, after_mixer=mixer is not None),
        grid=(m // tm,),
        in_specs=[rows(x2)] + mixer_specs + [vec, vec, vec, full(g), full(wup), full(wd)],
        out_specs=pl.BlockSpec((tm, d), lambda i: (i, 0)),
        out_shape=jax.ShapeDtypeStruct((m, d), F32),
        scratch_shapes=[
            pltpu.VMEM((tm, d), BF16),
            pltpu.VMEM((tm, tf), BF16),
            pltpu.VMEM((tm, d), F32),
        ],
        compiler_params=_params("arbitrary"),
        name="ffn",
    )(x2, *mixer_args, shift, scale, gate, g, wup, wd)


def _mixin_kernel(x_ref, shift_ref, scale_ref, g_ref, wlru_ref, wsb_ref, wfox_ref, wf_ref, bf_ref,
                  cw_ref, cb_ref, wgate_ref, bgate_ref, lam_ref, gq_ref, gk_ref, gmix_ref, hsum_ref,
                  pq_ref, cq_ref, pk_ref, ck_ref,
                  ylru_ref, sbq_ref, sbk_ref, sbv_ref, fxq_ref, fxk_ref, fxv_ref, fxqa_ref, fxka_ref,
                  xp_scr, a_scr, b_scr, hs_scr, hc_scr, fc_scr):
    ts = x_ref.shape[1]
    w = cw_ref.shape[1]
    n_heads = fxq_ref.shape[2] // HEAD_DIM
    pad = V7X_SUBLANES

    @pl.when(pl.program_id(1) == 0)
    def _():
        xp_scr[0:pad, :] = jnp.zeros((pad, w), F32)
        hc_scr[...] = jnp.zeros_like(hc_scr)
        fc_scr[...] = jnp.zeros_like(fc_scr)

    h = _mod_norm(x_ref[0], g_ref[...], scale_ref[0], shift_ref[0]).astype(BF16)

    p = jnp.dot(h, wlru_ref[...], preferred_element_type=F32)
    lru_g = p[:, w:]
    xp_scr[pad:pad + ts, :] = p[:, :w]
    x_all = xp_scr[...]
    u = cb_ref[...] + cw_ref[CONV_WIDTH - 1:CONV_WIDTH, :] * x_all[pad:, :]
    for back in range(1, CONV_WIDTH):
        tap = CONV_WIDTH - 1 - back
        u = u + cw_ref[tap:tap + 1, :] * pltpu.roll(x_all, back, 0)[pad:, :]
    xp_scr[0:pad, :] = x_all[ts:ts + pad, :]

    ub = u.astype(BF16)

    def gate(which):
        n_grp, grp = wgate_ref.shape[1], wgate_ref.shape[2]
        pre = [jnp.dot(ub[:, g * grp:(g + 1) * grp], wgate_ref[which, g], preferred_element_type=F32)
               for g in range(n_grp)]
        return _sigmoid(jnp.concatenate(pre, axis=1) + bgate_ref[:, which * w:(which + 1) * w])

    r = gate(0)
    ig = gate(1)
    lam = lam_ref[...]
    softplus_neg_lam = jnp.maximum(-lam, 0.0) + jnp.log1p(jnp.exp(-jnp.abs(lam)))
    a = jnp.exp((-LRU_C * softplus_neg_lam) * r)
    a_scr[...] = a
    s = 1.0 - a * a
    b_scr[...] = jnp.where(s > 0.0, s * lax.rsqrt(s), 0.0) * (ig * u)

    row8 = lax.broadcasted_iota(jnp.int32, (V7X_SUBLANES, w), 0)

    def scan8(i, hprev):
        r0 = pl.multiple_of(i * V7X_SUBLANES, V7X_SUBLANES)
        aa = a_scr[pl.ds(r0, V7X_SUBLANES), :]
        bb = b_scr[pl.ds(r0, V7X_SUBLANES), :]
        for sh in (1, 2, 4):
            a_s = pltpu.roll(aa, sh, 0)
            b_s = pltpu.roll(bb, sh, 0)
            m = row8 >= sh
            bb = jnp.where(m, aa * b_s + bb, bb)
            aa = jnp.where(m, aa * a_s, aa)
        hh = aa * hprev + bb
        hs_scr[pl.ds(r0, V7X_SUBLANES), :] = hh
        return hh[V7X_SUBLANES - 1:V7X_SUBLANES, :]

    hc_scr[0:1, :] = lax.fori_loop(0, ts // V7X_SUBLANES, scan8, hc_scr[0:1, :], unroll=True)

    y = hs_scr[...] * jax.nn.gelu(lru_g)
    yn = y * lax.rsqrt(jnp.mean(y * y, axis=-1, keepdims=True) + EPS) * gmix_ref[...]
    ylru_ref[0] = yn.astype(BF16)

    scale = HEAD_DIM ** -0.5
    psb = jnp.dot(h, wsb_ref[...], preferred_element_type=F32)
    hw = n_heads * HEAD_DIM
    sbq_ref[0] = (psb[:, :hw] * scale).astype(BF16)
    sbk_ref[0] = psb[:, hw:2 * hw].astype(BF16)
    sbv_ref[0] = psb[:, 2 * hw:].astype(BF16)

    pfx = jnp.dot(h, wfox_ref[...], preferred_element_type=F32)

    def head_rms(t):
        ssq = jnp.dot((t * t).astype(BF16), hsum_ref[...], preferred_element_type=F32)
        return t * lax.rsqrt(ssq * (1.0 / HEAD_DIM) + EPS)

    fxq_ref[0] = (head_rms(pfx[:, :hw]) * (gq_ref[...] * scale)).astype(BF16)
    fxk_ref[0] = (head_rms(pfx[:, hw:2 * hw]) * gk_ref[...]).astype(BF16)
    fxv_ref[0] = pfx[:, 2 * hw:].astype(BF16)

    lane = lax.broadcasted_iota(jnp.int32, (ts, V7X_LANES), 1)
    logf = _log_sigmoid(jnp.dot(h, wf_ref[...], preferred_element_type=F32) + bf_ref[...])
    logf = jnp.where(lane < 3 * n_heads, logf, 0.0)
    hi, mid, lo = _split3(logf)
    pieces = jnp.where(lane < n_heads, hi, jnp.where(lane < 2 * n_heads, mid, lo)).astype(BF16)
    tri = (lax.broadcasted_iota(jnp.int32, (ts, ts), 0) >= lax.broadcasted_iota(jnp.int32, (ts, ts), 1))
    cs = jnp.dot(tri.astype(BF16), pieces, preferred_element_type=F32)
    cs = cs + pltpu.roll(cs, V7X_LANES - n_heads, 1) + pltpu.roll(cs, V7X_LANES - 2 * n_heads, 1)
    fcum = cs + fc_scr[0:1, :]
    fc_scr[0:1, :] = fcum[ts - 1:ts, :]

    f_pieces = jnp.concatenate(_split3(fcum), axis=1).astype(BF16)
    fxqa_ref[0] = (jnp.dot(f_pieces, pq_ref[...], preferred_element_type=F32) + cq_ref[...]).astype(BF16)
    fxka_ref[0] = (jnp.dot(f_pieces, pk_ref[...], preferred_element_type=F32) + ck_ref[...]).astype(BF16)


def _mixer_in(x, shift, scale, g, wts):
    b, s, d = x.shape
    ts = MIXER_ROWS
    assert s % ts == 0
    w = wts["cw"].shape[1]
    hw = wts["wsb"].shape[1] // 3
    n_pairs = hw // V7X_LANES
    vec = pl.BlockSpec((1, 1, d), lambda i, j: (i, 0, 0))
    full = lambda a: pl.BlockSpec(a.shape, lambda i, j: (0,) * a.ndim)
    names = ("wlru", "wsb", "wfox", "wf", "bf", "cw", "cb", "wgate", "bgate", "lam", "gq", "gk", "gmix", "hsum",
             "pq", "cq", "pk", "ck")
    consts = [wts[k] for k in names]
    seq_out = lambda width: pl.BlockSpec((1, ts, width), lambda i, j: (i, j, 0))
    widths = (w, hw, hw, hw, hw, hw, hw, n_pairs * V7X_LANES, n_pairs * V7X_LANES)
    return pl.pallas_call(
        _mixin_kernel,
        grid=(b, s // ts),
        in_specs=[pl.BlockSpec((1, ts, d), lambda i, j: (i, j, 0)), vec, vec, full(g)] + [full(a) for a in consts],
        out_specs=[seq_out(n) for n in widths],
        out_shape=[jax.ShapeDtypeStruct((b, s, n), BF16) for n in widths],
        scratch_shapes=[
            pltpu.VMEM((ts + V7X_SUBLANES, w), F32),
            pltpu.VMEM((ts, w), F32),
            pltpu.VMEM((ts, w), F32),
            pltpu.VMEM((ts, w), F32),
            pltpu.VMEM((V7X_SUBLANES, w), F32),
            pltpu.VMEM((V7X_SUBLANES, V7X_LANES), F32),
        ],
        compiler_params=_params("arbitrary", "arbitrary"),
        name="mixer_in",
    )(x, shift, scale, g, *consts)


def _head_lanes(h):
    pr, e = divmod(h, HEADS_PER_PAIR)
    return slice(pr * V7X_LANES, (pr + 1) * V7X_LANES), e * HEAD_DIM


def _sb_kernel(q_ref, k_ref, v_ref, o_ref, qm_scr, z_scr, w_scr, carry_scr, acc_scr):
    tq = q_ref.shape[1]
    tk = tq
    n_heads = q_ref.shape[2] // HEAD_DIM
    heads = range(n_heads)
    i = pl.program_id(1)
    lane = lax.broadcasted_iota(jnp.int32, (tq, V7X_LANES), 1)
    past = lax.broadcasted_iota(jnp.int32, (tq, tk), 1) < lax.broadcasted_iota(jnp.int32, (tq, tk), 0)
    strict = (lax.broadcasted_iota(jnp.int32, (tk, tk), 0) > lax.broadcasted_iota(jnp.int32, (tk, tk), 1)).astype(BF16)

    for h in heads:
        cols, lo_lane = _head_lanes(h)
        q_pair = q_ref[0, :, cols]
        qm_scr[h] = jnp.where((lane >= lo_lane) & (lane < lo_lane + HEAD_DIM), q_pair, jnp.zeros_like(q_pair))
    carry_scr[...] = jnp.zeros_like(carry_scr)
    acc_scr[...] = jnp.zeros_like(acc_scr)

    def score(h, j):
        k0 = pl.multiple_of(j * tk, tk)
        return lax.dot_general(qm_scr[h], k_ref[0, pl.ds(k0, tk), _head_lanes(h)[0]], (((1,), (1,)), ((), ())),
                               preferred_element_type=F32)

    def weighted_values(h, j):
        k0 = pl.multiple_of(j * tk, tk)
        v = v_ref[0, pl.ds(k0, tk), _head_lanes(h)[0]]
        acc_scr[h] += jnp.dot(w_scr[h], v, preferred_element_type=F32)

    def consume(masked, ahead, behind):
        log_beta, log_1mb, after = [], [], []
        for h in heads:
            z = z_scr[h].astype(BF16)
            lb = jnp.minimum(z, 0.0) - jnp.log(1.0 + jnp.exp(-jnp.abs(z)))
            log_beta.append(lb)
            log_1mb.append(jnp.where(past, lb - z, 0.0) if masked else lb - z)
            carry = jnp.concatenate([carry_scr[h]] * (tk // V7X_LANES), axis=1)
            after.append(carry + jnp.dot(log_1mb[h].astype(BF16), strict, preferred_element_type=F32))
            if behind is not None:
                weighted_values(h, behind)
            if ahead is not None:
                z_scr[h] = score(h, ahead)
        for h in heads:
            w_h = jnp.exp(log_beta[h] + after[h].astype(BF16))
            w_scr[h] = jnp.where(past, w_h, jnp.zeros_like(w_h)) if masked else w_h
            carry_scr[h] = jnp.broadcast_to(after[h][:, 0:1] + log_1mb[h][:, 0:1], (tq, V7X_LANES))

    for h in heads:
        z_scr[h] = score(h, i)
    consume(True, jnp.maximum(i - 1, 0), None)

    def middle(j):
        consume(False, j - 1, j + 1)

    n_middle = jnp.maximum(i - 1, 0)

    def two_tiles(pair, c):
        j = i - 1 - 2 * pair
        middle(j)
        middle(j - 1)
        return c

    lax.fori_loop(0, n_middle // 2, two_tiles, 0)

    @pl.when(n_middle % 2 == 1)
    def _():
        middle(1)

    @pl.when(i > 0)
    def _():
        consume(False, None, 1)

    for h in heads:
        weighted_values(h, 0)

    for h in range(0, n_heads, HEADS_PER_PAIR):
        cols, _ = _head_lanes(h)
        o_ref[0, :, cols] = jnp.where(lane < HEAD_DIM, acc_scr[h], acc_scr[h + 1])


def _sb_attention(q, k, v):
    b, s, hw = q.shape
    tq = SB_TILE
    n_heads = hw // HEAD_DIM
    assert s % tq == 0
    return pl.pallas_call(
        _sb_kernel,
        grid=(b, s // tq),
        in_specs=[
            pl.BlockSpec((1, tq, hw), lambda i, j: (i, j, 0)),
            pl.BlockSpec((1, s, hw), lambda i, j: (i, 0, 0)),
            pl.BlockSpec((1, s, hw), lambda i, j: (i, 0, 0)),
        ],
        out_specs=pl.BlockSpec((1, tq, hw), lambda i, j: (i, j, 0)),
        out_shape=jax.ShapeDtypeStruct((b, s, hw), F32),
        scratch_shapes=[
            pltpu.VMEM((n_heads, tq, V7X_LANES), BF16),
            pltpu.VMEM((n_heads, tq, tq), F32),
            pltpu.VMEM((n_heads, tq, tq), BF16),
            pltpu.VMEM((n_heads, tq, V7X_LANES), F32),
            pltpu.VMEM((n_heads, tq, V7X_LANES), F32),
        ],
        compiler_params=_params("arbitrary", "arbitrary"),
        name="sb_attention",
    )(q, k, v)


def _fox_kernel(q_ref, qa_ref, k_ref, ka_ref, v_ref, o_ref, qm_scr, z_scr, m_scr, l_scr, acc_scr, *, tk):
    tq = q_ref.shape[1]
    n_heads = q_ref.shape[2] // HEAD_DIM
    heads = range(n_heads)
    i = pl.program_id(1)
    last = (i * tq) // tk
    lane = lax.broadcasted_iota(jnp.int32, (tq, V7X_LANES), 1)
    col_minus_row = lax.broadcasted_iota(jnp.int32, (tq, tk), 1) - lax.broadcasted_iota(jnp.int32, (tq, tk), 0)
    visible = col_minus_row <= i * tq - last * tk
    ones = jnp.ones((tk, V7X_LANES), BF16)

    for h in heads:
        cols, lo_lane = _head_lanes(h)
        e = h % HEADS_PER_PAIR
        q_pair = q_ref[0, :, cols]
        qa_pair = qa_ref[0, :, cols]
        in_head = (lane >= lo_lane) & (lane < lo_lane + HEAD_DIM)
        in_aug = (lane >= e * AUG_PER_HEAD) & (lane < (e + 1) * AUG_PER_HEAD)
        qm_scr[h] = jnp.concatenate([jnp.where(in_head, q_pair, jnp.zeros_like(q_pair)),
                                     jnp.where(in_aug, qa_pair, jnp.zeros_like(qa_pair))], axis=1)
    m_scr[...] = jnp.full_like(m_scr, NEG_BIG)
    l_scr[...] = jnp.zeros_like(l_scr)
    acc_scr[...] = jnp.zeros_like(acc_scr)

    def score(h, j):
        k0 = pl.multiple_of(j * tk, tk)
        cols, _ = _head_lanes(h)
        k = jnp.concatenate([k_ref[0, pl.ds(k0, tk), cols], ka_ref[0, pl.ds(k0, tk), cols]], axis=1)
        return lax.dot_general(qm_scr[h], k, (((1,), (1,)), ((), ())), preferred_element_type=F32)

    def consume(j, masked, ahead):
        k0 = pl.multiple_of(j * tk, tk)
        for h in heads:
            z = jnp.where(visible, z_scr[h], -jnp.inf) if masked else z_scr[h]
            m_prev = m_scr[h]
            m_new = jnp.maximum(m_prev, jnp.max(z, axis=-1, keepdims=True))
            alpha = jnp.exp(m_prev - m_new)
            p = jnp.exp(z - jnp.concatenate([m_new] * (tk // V7X_LANES), axis=1)).astype(BF16)
            m_scr[h] = m_new
            v = jnp.concatenate([v_ref[0, pl.ds(k0, tk), _head_lanes(h)[0]], ones], axis=1)
            pv = jnp.dot(p, v, preferred_element_type=F32)
            if ahead is not None:
                z_scr[h] = score(h, ahead)
            l_scr[h] = alpha * l_scr[h] + pv[:, V7X_LANES:]
            acc_scr[h] = alpha * acc_scr[h] + pv[:, :V7X_LANES]

    for h in heads:
        z_scr[h] = score(h, 0)

    def step(t, c):
        consume(t, False, t + 1)
        return c

    lax.fori_loop(0, last, step, 0)
    consume(last, True, None)
    for h in range(0, n_heads, HEADS_PER_PAIR):
        cols, _ = _head_lanes(h)
        o_ref[0, :, cols] = jnp.where(lane < HEAD_DIM, acc_scr[h] / l_scr[h], acc_scr[h + 1] / l_scr[h + 1])


def _fox_attention(q, qa, k, ka, v):
    b, s, hw = q.shape
    tq, tk = FOX_Q_TILE, FOX_K_TILE
    n_heads = hw // HEAD_DIM
    assert s % tq == 0 and s % tk == 0 and tk % tq == 0
    qspec = pl.BlockSpec((1, tq, hw), lambda i, j: (i, j, 0))
    kspec = pl.BlockSpec((1, s, hw), lambda i, j: (i, 0, 0))
    return pl.pallas_call(
        functools.partial(_fox_kernel, tk=tk),
        grid=(b, s // tq),
        in_specs=[qspec, qspec, kspec, kspec, kspec],
        out_specs=pl.BlockSpec((1, tq, hw), lambda i, j: (i, j, 0)),
        out_shape=jax.ShapeDtypeStruct((b, s, hw), F32),
        scratch_shapes=[
            pltpu.VMEM((n_heads, tq, 2 * V7X_LANES), BF16),
            pltpu.VMEM((n_heads, tq, tk), F32),
            pltpu.VMEM((n_heads, tq, V7X_LANES), F32),
            pltpu.VMEM((n_heads, tq, V7X_LANES), F32),
            pltpu.VMEM((n_heads, tq, V7X_LANES), F32),
        ],
        compiler_params=_params("arbitrary", "arbitrary"),
        name="fox_attention",
    )(q, qa, k, ka, v)


def _block_diag(w):
    h, di, dj = w.shape
    return jnp.einsum("hij,hg->higj", w, jnp.eye(h, dtype=w.dtype)).reshape(h * di, h * dj)


def _diag_groups(bd):
    n = bd.shape[0]
    assert n % V7X_MXU_DIM == 0 and V7X_MXU_DIM % HEAD_DIM == 0
    n_grp = n // V7X_MXU_DIM
    blocks = bd.reshape(n_grp, V7X_MXU_DIM, n_grp, V7X_MXU_DIM)
    return jnp.stack([blocks[g, :, g, :] for g in range(n_grp)])


def _aug_placement(n_heads):
    n_out = (n_heads // HEADS_PER_PAIR) * V7X_LANES
    pq = np.zeros((3 * V7X_LANES, n_out), np.float32)
    pk = np.zeros((3 * V7X_LANES, n_out), np.float32)
    cq = np.zeros((1, n_out), np.float32)
    ck = np.zeros((1, n_out), np.float32)
    for h in range(n_heads):
        pr, e = divmod(h, HEADS_PER_PAIR)
        base = pr * V7X_LANES + e * AUG_PER_HEAD
        for j in range(3):
            pq[j * V7X_LANES + h, base + j] = 1.0
            cq[0, base + 3 + j] = 1.0
            ck[0, base + j] = 1.0
            pk[j * V7X_LANES + h, base + 3 + j] = -1.0
    return {"pq": jnp.asarray(pq, BF16), "cq": jnp.asarray(cq), "pk": jnp.asarray(pk, BF16), "ck": jnp.asarray(ck)}


def _mixer_weights(w_in, b_fgate, conv_w, conv_b, w_rgate, b_rgate, w_igate, b_igate, lru_lambda, g_qk, g_mix_out):
    w = conv_w.shape[1]
    n_heads = b_fgate.shape[0]
    hw = n_heads * HEAD_DIM
    o_sb = 2 * w
    o_fx = o_sb + 3 * hw
    o_f = o_fx + 3 * hw
    wf = jnp.zeros((w_in.shape[0], V7X_LANES), F32).at[:, :3 * n_heads].set(jnp.tile(w_in[:, o_f:o_f + n_heads], (1, 3)))
    bf = jnp.zeros((1, V7X_LANES), F32).at[0, :3 * n_heads].set(jnp.tile(b_fgate, 3))
    head_of = jnp.arange(hw) // HEAD_DIM
    hsum = (head_of[:, None] == head_of[None, :]).astype(BF16)
    return {
        "wlru": w_in[:, :o_sb].astype(BF16),
        "wsb": w_in[:, o_sb:o_fx].astype(BF16),
        "wfox": w_in[:, o_fx:o_f].astype(BF16),
        "wf": wf.astype(BF16),
        "bf": bf,
        "cw": conv_w,
        "cb": conv_b.reshape(1, w),
        "wgate": jnp.stack([_diag_groups(_block_diag(w_rgate)), _diag_groups(_block_diag(w_igate))]).astype(BF16),
        "bgate": jnp.concatenate([b_rgate, b_igate]).reshape(1, 2 * w),
        "lam": lru_lambda.reshape(1, w),
        "gq": jnp.tile(g_qk[0], n_heads).reshape(1, hw),
        "gk": jnp.tile(g_qk[1], n_heads).reshape(1, hw),
        "gmix": g_mix_out[:w].reshape(1, w),
        "hsum": hsum,
        **_aug_placement(n_heads),
    }


def kernel(x, c, w_ada, b_ada, g_norm, w_ffn_up, w_ffn_down, w_in, b_fgate, conv_w, conv_b, w_rgate, b_rgate,
           w_igate, b_igate, lru_lambda, g_qk, g_mix_out, w_out):
    b, s, d = x.shape
    depth = w_ada.shape[0]
    w = conv_w.shape[-1]
    hw = b_fgate.shape[-1] * HEAD_DIM

    mod = _adaln_mod(c, w_ada, b_ada).reshape(depth, b, N_SUB, 3, 1, d)

    for l in range(depth):
        shift = lambda j: mod[l, :, j, 0]
        scale = lambda j: mod[l, :, j, 1]
        gate = lambda j: mod[l, :, j, 2]
        gn = lambda j: g_norm[l, j].reshape(1, d)

        def ffn(xx, j, i, mixer=None):
            return _ffn(xx.reshape(b * s, d), shift(j), scale(j), gate(j), gn(j), w_ffn_up[l, i].astype(BF16),
                        w_ffn_down[l, i].astype(BF16), s, mixer).reshape(b, s, d)

        x = ffn(x, 0, 0)

        wts = _mixer_weights(w_in[l], b_fgate[l], conv_w[l], conv_b[l], w_rgate[l], b_rgate[l], w_igate[l],
                             b_igate[l], lru_lambda[l], g_qk[l], g_mix_out[l])
        ylru, sbq, sbk, sbv, fxq, fxk, fxv, fxqa, fxka = _mixer_in(x, shift(1), scale(1), gn(1), wts)
        ysb = _sb_attention(sbq, sbk, sbv)
        yfox = _fox_attention(fxq, fxqa, fxk, fxka, fxv)
        rows = lambda a: a.reshape(b * s, a.shape[-1])
        x = ffn(x, 2, 1, mixer=(gate(1), rows(ylru), rows(ysb), rows(yfox), g_mix_out[l, w:w + hw].reshape(1, hw),
                                g_mix_out[l, w + hw:].reshape(1, hw), w_out[l].astype(BF16)))
    return x
```

```python
import functools

import jax
import jax.numpy as jnp
import numpy as np
from jax import lax
from jax.experimental import pallas as pl
from jax.experimental.pallas import tpu as pltpu

F32 = jnp.float32
BF16 = jnp.bfloat16

HEAD_DIM = 64
CONV_WIDTH = 4
LRU_C = 8.0
N_SUB = 3
EPS = 1e-6

V7X_LANES = 128
V7X_SUBLANES = 8
V7X_MXU_DIM = 256
V7X_VMEM_BYTES = 64 * 1024 * 1024
VMEM_LIMIT_BYTES = V7X_VMEM_BYTES - 8 * 1024 * 1024

ADALN_COLS = 1536
FFN_ROWS = 1024
FFN_FF_CHUNK = V7X_MXU_DIM
MIXER_ROWS = 512
SB_TILE = V7X_MXU_DIM
FOX_Q_TILE = 512
FOX_K_TILE = 512

HEADS_PER_PAIR = V7X_LANES // HEAD_DIM
AUG_PER_HEAD = 6
NEG_BIG = -1e30


def _params(*sem):
    return pltpu.CompilerParams(dimension_semantics=sem, vmem_limit_bytes=VMEM_LIMIT_BYTES)


def _sigmoid(x):
    return 1.0 / (1.0 + jnp.exp(-x))


def _log_sigmoid(x):
    return jnp.minimum(x, 0.0) - jnp.log(1.0 + jnp.exp(-jnp.abs(x)))


def _split3(v):
    hi = v.astype(BF16).astype(F32)
    r = v - hi
    mid = r.astype(BF16).astype(F32)
    lo = (r - mid).astype(BF16).astype(F32)
    return hi, mid, lo


def _mod_norm(x, g, scale, shift):
    ms = jnp.mean(x * x, axis=-1, keepdims=True)
    return (x * lax.rsqrt(ms + EPS)) * (g * (1.0 + scale)) + shift


def _mod_kernel(c_ref, w_ref, b_ref, o_ref):
    c = c_ref[...]
    ca = c * _sigmoid(c)
    o_ref[0] = jnp.dot(ca.astype(BF16), w_ref[0].astype(BF16), preferred_element_type=F32) + b_ref[0]


def _adaln_mod(c, w_ada, b_ada):
    depth, d, n = w_ada.shape
    b = c.shape[0]
    tn = ADALN_COLS
    assert n % tn == 0
    return pl.pallas_call(
        _mod_kernel,
        grid=(depth, n // tn),
        in_specs=[
            pl.BlockSpec((b, d), lambda l, j: (0, 0)),
            pl.BlockSpec((1, d, tn), lambda l, j: (l, 0, j)),
            pl.BlockSpec((1, 1, tn), lambda l, j: (l, 0, j)),
        ],
        out_specs=pl.BlockSpec((1, b, tn), lambda l, j: (l, 0, j)),
        out_shape=jax.ShapeDtypeStruct((depth, b, n), F32),
        compiler_params=_params("arbitrary", "arbitrary"),
        name="adaln_mod",
    )(c, w_ada, b_ada.reshape(depth, 1, n))


def _group_norm(y, g):
    return (y * lax.rsqrt(jnp.mean(y * y, axis=-1, keepdims=True) + EPS) * g).astype(BF16)


def _ffn_kernel(*refs, after_mixer):
    if after_mixer:
        x_ref, mgate_ref, ylru_ref, ysb_ref, yfox_ref, gsb_ref, gfox_ref, wout_ref, *refs = refs
    else:
        x_ref, *refs = refs
    shift_ref, scale_ref, gate_ref, g_ref, wup_ref, wd_ref, o_ref, h_scr, a_scr, acc_scr = refs
    ff = wd_ref.shape[0]
    tf = a_scr.shape[1]
    n_chunks = ff // tf
    if after_mixer:
        y = jnp.concatenate([ylru_ref[...], _group_norm(ysb_ref[...], gsb_ref[...]),
                             _group_norm(yfox_ref[...], gfox_ref[...])], axis=1)
        o_ref[...] = x_ref[...] + (1.0 + mgate_ref[0]) * jnp.dot(y, wout_ref[...], preferred_element_type=F32)
        x_ref = o_ref
    h_scr[...] = _mod_norm(x_ref[...], g_ref[...], scale_ref[0], shift_ref[0]).astype(BF16)

    def up(c):
        hb = h_scr[...]
        off = pl.multiple_of(c * tf, tf)
        g = jnp.dot(hb, wup_ref[:, pl.ds(off, tf)], preferred_element_type=F32)
        u = jnp.dot(hb, wup_ref[:, pl.ds(off + ff, tf)], preferred_element_type=F32)
        return ((g * _sigmoid(g)) * u).astype(BF16)

    def down(c):
        return jnp.dot(a_scr[...], wd_ref[pl.ds(pl.multiple_of(c * tf, tf), tf), :], preferred_element_type=F32)

    a_scr[...] = up(0)
    d = down(0)
    a_next = up(1)
    acc_scr[...] = d
    a_scr[...] = a_next

    def chunk(c, carry):
        d = down(c)
        a_next = up(c + 1)
        acc_scr[...] += d
        a_scr[...] = a_next
        return carry

    lax.fori_loop(1, n_chunks - 1, chunk, 0, unroll=3)
    o_ref[...] = x_ref[...] + (0.5 * (1.0 + gate_ref[0])) * (acc_scr[...] + down(n_chunks - 1))


def _ffn(x2, shift, scale, gate, g, wup, wd, seq, mixer=None):
    m, d = x2.shape
    tm, tf = FFN_ROWS, FFN_FF_CHUNK
    assert m % tm == 0 and seq % tm == 0 and wd.shape[0] % tf == 0 and wup.shape[1] == 2 * wd.shape[0]
    per_seq = seq // tm
    vec = pl.BlockSpec((1, 1, d), lambda i: (i // per_seq, 0, 0))
    full = lambda a: pl.BlockSpec(a.shape, lambda i: (0,) * a.ndim)
    rows = lambda a: pl.BlockSpec((tm, a.shape[1]), lambda i: (i, 0))
    mixer_args, mixer_specs = (), []
    if mixer is not None:
        mgate, ylru, ysb, yfox, gsb, gfox, wout = mixer
        mixer_args = (mgate, ylru, ysb, yfox, gsb, gfox, wout)
        mixer_specs = [vec, rows(ylru), rows(ysb), rows(yfox), full(gsb), full(gfox), full(wout)]
    return pl.pallas_call(
        functools.partial(_ffn_kernel, after_mixer=mixer is not None),
        grid=(m // tm,),
        in_specs=[rows(x2)] + mixer_specs + [vec, vec, vec, full(g), full(wup), full(wd)],
        out_specs=pl.BlockSpec((tm, d), lambda i: (i, 0)),
        out_shape=jax.ShapeDtypeStruct((m, d), F32),
        scratch_shapes=[
            pltpu.VMEM((tm, d), BF16),
            pltpu.VMEM((tm, tf), BF16),
            pltpu.VMEM((tm, d), F32),
        ],
        compiler_params=_params("arbitrary"),
        name="ffn",
    )(x2, *mixer_args, shift, scale, gate, g, wup, wd)


def _mixin_kernel(x_ref, shift_ref, scale_ref, g_ref, wlru_ref, wsb_ref, wfox_ref, wf_ref, bf_ref,
                  cw_ref, cb_ref, wgate_ref, bgate_ref, lam_ref, gq_ref, gk_ref, gmix_ref, hsum_ref,
                  pq_ref, cq_ref, pk_ref, ck_ref,
                  ylru_ref, sbq_ref, sbk_ref, sbv_ref, fxq_ref, fxk_ref, fxv_ref, fxqa_ref, fxka_ref,
                  xp_scr, a_scr, b_scr, hs_scr, hc_scr, fc_scr):
    ts = x_ref.shape[1]
    w = cw_ref.shape[1]
    n_heads = fxq_ref.shape[2] // HEAD_DIM
    pad = V7X_SUBLANES

    @pl.when(pl.program_id(1) == 0)
    def _():
        xp_scr[0:pad, :] = jnp.zeros((pad, w), F32)
        hc_scr[...] = jnp.zeros_like(hc_scr)
        fc_scr[...] = jnp.zeros_like(fc_scr)

    h = _mod_norm(x_ref[0], g_ref[...], scale_ref[0], shift_ref[0]).astype(BF16)

    p = jnp.dot(h, wlru_ref[...], preferred_element_type=F32)
    lru_g = p[:, w:]
    xp_scr[pad:pad + ts, :] = p[:, :w]
    x_all = xp_scr[...]
    u = cb_ref[...] + cw_ref[CONV_WIDTH - 1:CONV_WIDTH, :] * x_all[pad:, :]
    for back in range(1, CONV_WIDTH):
        tap = CONV_WIDTH - 1 - back
        u = u + cw_ref[tap:tap + 1, :] * pltpu.roll(x_all, back, 0)[pad:, :]
    xp_scr[0:pad, :] = x_all[ts:ts + pad, :]

    ub = u.astype(BF16)

    def gate(which):
        n_grp, grp = wgate_ref.shape[1], wgate_ref.shape[2]
        pre = [jnp.dot(ub[:, g * grp:(g + 1) * grp], wgate_ref[which, g], preferred_element_type=F32)
               for g in range(n_grp)]
        return _sigmoid(jnp.concatenate(pre, axis=1) + bgate_ref[:, which * w:(which + 1) * w])

    r = gate(0)
    ig = gate(1)
    lam = lam_ref[...]
    softplus_neg_lam = jnp.maximum(-lam, 0.0) + jnp.log1p(jnp.exp(-jnp.abs(lam)))
    log_a = (-LRU_C * softplus_neg_lam) * r
    a = jnp.exp(log_a)
    a_scr[...] = a
    s = (1.0 + a * a) * jnp.tanh(-log_a)
    b_scr[...] = jnp.where(s > 0.0, s * lax.rsqrt(s), 0.0) * (ig * u)

    row8 = lax.broadcasted_iota(jnp.int32, (V7X_SUBLANES, w), 0)

    def scan8(i, hprev):
        r0 = pl.multiple_of(i * V7X_SUBLANES, V7X_SUBLANES)
        aa = a_scr[pl.ds(r0, V7X_SUBLANES), :]
        bb = b_scr[pl.ds(r0, V7X_SUBLANES), :]
        for sh in (1, 2, 4):
            a_s = pltpu.roll(aa, sh, 0)
            b_s = pltpu.roll(bb, sh, 0)
            m = row8 >= sh
            bb = jnp.where(m, aa * b_s + bb, bb)
            aa = jnp.where(m, aa * a_s, aa)
        hh = aa * hprev + bb
        hs_scr[pl.ds(r0, V7X_SUBLANES), :] = hh
        return hh[V7X_SUBLANES - 1:V7X_SUBLANES, :]

    hc_scr[0:1, :] = lax.fori_loop(0, ts // V7X_SUBLANES, scan8, hc_scr[0:1, :], unroll=True)

    y = hs_scr[...] * jax.nn.gelu(lru_g)
    yn = y * lax.rsqrt(jnp.mean(y * y, axis=-1, keepdims=True) + EPS) * gmix_ref[...]
    ylru_ref[0] = yn.astype(BF16)

    scale = HEAD_DIM ** -0.5
    psb = jnp.dot(h, wsb_ref[...], preferred_element_type=F32)
    hw = n_heads * HEAD_DIM
    sbq_ref[0] = (psb[:, :hw] * scale).astype(BF16)
    sbk_ref[0] = psb[:, hw:2 * hw].astype(BF16)
    sbv_ref[0] = psb[:, 2 * hw:].astype(BF16)

    pfx = jnp.dot(h, wfox_ref[...], preferred_element_type=F32)

    def head_rms(t):
        ssq = jnp.dot((t * t).astype(BF16), hsum_ref[...], preferred_element_type=F32)
        return t * lax.rsqrt(ssq * (1.0 / HEAD_DIM) + EPS)

    fxq_ref[0] = (head_rms(pfx[:, :hw]) * (gq_ref[...] * scale)).astype(BF16)
    fxk_ref[0] = (head_rms(pfx[:, hw:2 * hw]) * gk_ref[...]).astype(BF16)
    fxv_ref[0] = pfx[:, 2 * hw:].astype(BF16)

    lane = lax.broadcasted_iota(jnp.int32, (ts, V7X_LANES), 1)
    logf = _log_sigmoid(jnp.dot(h, wf_ref[...], preferred_element_type=F32) + bf_ref[...])
    logf = jnp.where(lane < 3 * n_heads, logf, 0.0)
    hi, mid, lo = _split3(logf)
    pieces = jnp.where(lane < n_heads, hi, jnp.where(lane < 2 * n_heads, mid, lo)).astype(BF16)
    tri = (lax.broadcasted_iota(jnp.int32, (ts, ts), 0) >= lax.broadcasted_iota(jnp.int32, (ts, ts), 1))
    cs = jnp.dot(tri.astype(BF16), pieces, preferred_element_type=F32)
    cs = cs + pltpu.roll(cs, V7X_LANES - n_heads, 1) + pltpu.roll(cs, V7X_LANES - 2 * n_heads, 1)
    fcum = cs + fc_scr[0:1, :]
    fc_scr[0:1, :] = fcum[ts - 1:ts, :]

    f_pieces = jnp.concatenate(_split3(fcum), axis=1).astype(BF16)
    fxqa_ref[0] = (jnp.dot(f_pieces, pq_ref[...], preferred_element_type=F32) + cq_ref[...]).astype(BF16)
    fxka_ref[0] = (jnp.dot(f_pieces, pk_ref[...], preferred_element_type=F32) + ck_ref[...]).astype(BF16)


def _mixer_in(x, shift, scale, g, wts):
    b, s, d = x.shape
    ts = MIXER_ROWS
    assert s % ts == 0
    w = wts["cw"].shape[1]
    hw = wts["wsb"].shape[1] // 3
    n_pairs = hw // V7X_LANES
    vec = pl.BlockSpec((1, 1, d), lambda i, j: (i, 0, 0))
    full = lambda a: pl.BlockSpec(a.shape, lambda i, j: (0,) * a.ndim)
    names = ("wlru", "wsb", "wfox", "wf", "bf", "cw", "cb", "wgate", "bgate", "lam", "gq", "gk", "gmix", "hsum",
             "pq", "cq", "pk", "ck")
    consts = [wts[k] for k in names]
    seq_out = lambda width: pl.BlockSpec((1, ts, width), lambda i, j: (i, j, 0))
    widths = (w, hw, hw, hw, hw, hw, hw, n_pairs * V7X_LANES, n_pairs * V7X_LANES)
    return pl.pallas_call(
        _mixin_kernel,
        grid=(b, s // ts),
        in_specs=[pl.BlockSpec((1, ts, d), lambda i, j: (i, j, 0)), vec, vec, full(g)] + [full(a) for a in consts],
        out_specs=[seq_out(n) for n in widths],
        out_shape=[jax.ShapeDtypeStruct((b, s, n), BF16) for n in widths],
        scratch_shapes=[
            pltpu.VMEM((ts + V7X_SUBLANES, w), F32),
            pltpu.VMEM((ts, w), F32),
            pltpu.VMEM((ts, w), F32),
            pltpu.VMEM((ts, w), F32),
            pltpu.VMEM((V7X_SUBLANES, w), F32),
            pltpu.VMEM((V7X_SUBLANES, V7X_LANES), F32),
        ],
        compiler_params=_params("arbitrary", "arbitrary"),
        name="mixer_in",
    )(x, shift, scale, g, *consts)


def _head_lanes(h):
    pr, e = divmod(h, HEADS_PER_PAIR)
    return slice(pr * V7X_LANES, (pr + 1) * V7X_LANES), e * HEAD_DIM


def _neg_abs(x):
    bits = lax.bitcast_convert_type(x, jnp.uint32) | jnp.uint32(0x80000000)
    return lax.bitcast_convert_type(bits, F32)


def _sb_kernel(q_ref, k_ref, v_ref, o_ref, qm_scr, z_scr, w_scr, carry_scr, acc_scr):
    tq = q_ref.shape[1]
    tk = tq
    n_heads = q_ref.shape[2] // HEAD_DIM
    heads = range(n_heads)
    i = pl.program_id(1)
    lane = lax.broadcasted_iota(jnp.int32, (tq, V7X_LANES), 1)
    past = lax.broadcasted_iota(jnp.int32, (tq, tk), 1) < lax.broadcasted_iota(jnp.int32, (tq, tk), 0)
    strict = (lax.broadcasted_iota(jnp.int32, (tk, tk), 0) > lax.broadcasted_iota(jnp.int32, (tk, tk), 1)).astype(BF16)

    for h in heads:
        cols, lo_lane = _head_lanes(h)
        q_pair = q_ref[0, :, cols]
        qm_scr[h] = jnp.where((lane >= lo_lane) & (lane < lo_lane + HEAD_DIM), q_pair, jnp.zeros_like(q_pair))
    carry_scr[...] = jnp.zeros_like(carry_scr)
    acc_scr[...] = jnp.zeros_like(acc_scr)

    def score(h, j):
        k0 = pl.multiple_of(j * tk, tk)
        return lax.dot_general(qm_scr[h], k_ref[0, pl.ds(k0, tk), _head_lanes(h)[0]], (((1,), (1,)), ((), ())),
                               preferred_element_type=F32)

    def weighted_values(h, j):
        k0 = pl.multiple_of(j * tk, tk)
        v = v_ref[0, pl.ds(k0, tk), _head_lanes(h)[0]]
        acc_scr[h] += jnp.dot(w_scr[h], v, preferred_element_type=F32)

    def consume(masked, ahead, behind):
        log_beta, log_1mb, after = [], [], []
        for h in heads:
            z = z_scr[h]
            sp = jnp.log(1.0 + jnp.exp(_neg_abs(z))).astype(BF16)
            zb = z.astype(BF16)
            l1 = jnp.minimum(-zb, 0.0) - sp
            log_beta.append(jnp.minimum(zb, 0.0) - sp)
            log_1mb.append(jnp.where(past, l1, jnp.zeros_like(l1)) if masked else l1)
            carry = jnp.concatenate([carry_scr[h]] * (tk // V7X_LANES), axis=1)
            after.append(carry + jnp.dot(log_1mb[h], strict, preferred_element_type=F32))
            if behind is not None:
                weighted_values(h, behind)
            if ahead is not None:
                z_scr[h] = score(h, ahead)
        for h in heads:
            w_h = jnp.exp(log_beta[h] + after[h].astype(BF16))
            w_scr[h] = jnp.where(past, w_h, jnp.zeros_like(w_h)) if masked else w_h
            carry_scr[h] = jnp.broadcast_to(after[h][:, 0:1] + log_1mb[h][:, 0:1].astype(F32), (tq, V7X_LANES))

    for h in heads:
        z_scr[h] = score(h, i)
    consume(True, jnp.maximum(i - 1, 0), None)

    def middle(j):
        consume(False, j - 1, j + 1)

    n_middle = jnp.maximum(i - 1, 0)

    def two_tiles(pair, c):
        j = i - 1 - 2 * pair
        middle(j)
        middle(j - 1)
        return c

    lax.fori_loop(0, n_middle // 2, two_tiles, 0)

    @pl.when(n_middle % 2 == 1)
    def _():
        middle(1)

    @pl.when(i > 0)
    def _():
        consume(False, None, 1)

    for h in heads:
        weighted_values(h, 0)

    for h in range(0, n_heads, HEADS_PER_PAIR):
        cols, _ = _head_lanes(h)
        o_ref[0, :, cols] = jnp.where(lane < HEAD_DIM, acc_scr[h], acc_scr[h + 1])


def _sb_attention(q, k, v):
    b, s, hw = q.shape
    tq = SB_TILE
    n_heads = hw // HEAD_DIM
    assert s % tq == 0
    return pl.pallas_call(
        _sb_kernel,
        grid=(b, s // tq),
        in_specs=[
            pl.BlockSpec((1, tq, hw), lambda i, j: (i, j, 0)),
            pl.BlockSpec((1, s, hw), lambda i, j: (i, 0, 0)),
            pl.BlockSpec((1, s, hw), lambda i, j: (i, 0, 0)),
        ],
        out_specs=pl.BlockSpec((1, tq, hw), lambda i, j: (i, j, 0)),
        out_shape=jax.ShapeDtypeStruct((b, s, hw), F32),
        scratch_shapes=[
            pltpu.VMEM((n_heads, tq, V7X_LANES), BF16),
            pltpu.VMEM((n_heads, tq, tq), F32),
            pltpu.VMEM((n_heads, tq, tq), BF16),
            pltpu.VMEM((n_heads, tq, V7X_LANES), F32),
            pltpu.VMEM((n_heads, tq, V7X_LANES), F32),
        ],
        compiler_params=_params("arbitrary", "arbitrary"),
        name="sb_attention",
    )(q, k, v)


def _fox_kernel(q_ref, qa_ref, k_ref, ka_ref, v_ref, o_ref, qm_scr, z_scr, m_scr, l_scr, acc_scr, *, tk):
    tq = q_ref.shape[1]
    n_heads = q_ref.shape[2] // HEAD_DIM
    heads = range(n_heads)
    i = pl.program_id(1)
    last = (i * tq) // tk
    lane = lax.broadcasted_iota(jnp.int32, (tq, V7X_LANES), 1)
    col_minus_row = lax.broadcasted_iota(jnp.int32, (tq, tk), 1) - lax.broadcasted_iota(jnp.int32, (tq, tk), 0)
    visible = col_minus_row <= i * tq - last * tk
    ones = jnp.ones((tk, V7X_LANES), BF16)

    for h in heads:
        cols, lo_lane = _head_lanes(h)
        e = h % HEADS_PER_PAIR
        q_pair = q_ref[0, :, cols]
        qa_pair = qa_ref[0, :, cols]
        in_head = (lane >= lo_lane) & (lane < lo_lane + HEAD_DIM)
        in_aug = (lane >= e * AUG_PER_HEAD) & (lane < (e + 1) * AUG_PER_HEAD)
        qm_scr[h] = jnp.concatenate([jnp.where(in_head, q_pair, jnp.zeros_like(q_pair)),
                                     jnp.where(in_aug, qa_pair, jnp.zeros_like(qa_pair))], axis=1)
    m_scr[...] = jnp.full_like(m_scr, NEG_BIG)
    l_scr[...] = jnp.zeros_like(l_scr)
    acc_scr[...] = jnp.zeros_like(acc_scr)

    def score(h, j):
        k0 = pl.multiple_of(j * tk, tk)
        cols, _ = _head_lanes(h)
        k = jnp.concatenate([k_ref[0, pl.ds(k0, tk), cols], ka_ref[0, pl.ds(k0, tk), cols]], axis=1)
        return lax.dot_general(qm_scr[h], k, (((1,), (1,)), ((), ())), preferred_element_type=F32)

    def consume(j, masked, ahead):
        k0 = pl.multiple_of(j * tk, tk)
        for h in heads:
            z = jnp.where(visible, z_scr[h], -jnp.inf) if masked else z_scr[h]
            m_prev = m_scr[h]
            m_new = jnp.maximum(m_prev, jnp.max(z, axis=-1, keepdims=True))
            alpha = jnp.exp(m_prev - m_new)
            p = jnp.exp(z - jnp.concatenate([m_new] * (tk // V7X_LANES), axis=1)).astype(BF16)
            m_scr[h] = m_new
            v = jnp.concatenate([v_ref[0, pl.ds(k0, tk), _head_lanes(h)[0]], ones], axis=1)
            pv = jnp.dot(p, v, preferred_element_type=F32)
            if ahead is not None:
                z_scr[h] = score(h, ahead)
            l_scr[h] = alpha * l_scr[h] + pv[:, V7X_LANES:]
            acc_scr[h] = alpha * acc_scr[h] + pv[:, :V7X_LANES]

    for h in heads:
        z_scr[h] = score(h, 0)

    def step(t, c):
        consume(t, False, t + 1)
        return c

    lax.fori_loop(0, last, step, 0)
    consume(last, True, None)
    for h in range(0, n_heads, HEADS_PER_PAIR):
        cols, _ = _head_lanes(h)
        o_ref[0, :, cols] = jnp.where(lane < HEAD_DIM, acc_scr[h] / l_scr[h], acc_scr[h + 1] / l_scr[h + 1])


def _fox_attention(q, qa, k, ka, v):
    b, s, hw = q.shape
    tq, tk = FOX_Q_TILE, FOX_K_TILE
    n_heads = hw // HEAD_DIM
    assert s % tq == 0 and s % tk == 0 and tk % tq == 0
    qspec = pl.BlockSpec((1, tq, hw), lambda i, j: (i, j, 0))
    kspec = pl.BlockSpec((1, s, hw), lambda i, j: (i, 0, 0))
    return pl.pallas_call(
        functools.partial(_fox_kernel, tk=tk),
        grid=(b, s // tq),
        in_specs=[qspec, qspec, kspec, kspec, kspec],
        out_specs=pl.BlockSpec((1, tq, hw), lambda i, j: (i, j, 0)),
        out_shape=jax.ShapeDtypeStruct((b, s, hw), F32),
        scratch_shapes=[
            pltpu.VMEM((n_heads, tq, 2 * V7X_LANES), BF16),
            pltpu.VMEM((n_heads, tq, tk), F32),
            pltpu.VMEM((n_heads, tq, V7X_LANES), F32),
            pltpu.VMEM((n_heads, tq, V7X_LANES), F32),
            pltpu.VMEM((n_heads, tq, V7X_LANES), F32),
        ],
        compiler_params=_params("arbitrary", "arbitrary"),
        name="fox_attention",
    )(q, qa, k, ka, v)


def _block_diag(w):
    h, di, dj = w.shape
    return jnp.einsum("hij,hg->higj", w, jnp.eye(h, dtype=w.dtype)).reshape(h * di, h * dj)


def _diag_groups(bd):
    n = bd.shape[0]
    assert n % V7X_MXU_DIM == 0 and V7X_MXU_DIM % HEAD_DIM == 0
    n_grp = n // V7X_MXU_DIM
    blocks = bd.reshape(n_grp, V7X_MXU_DIM, n_grp, V7X_MXU_DIM)
    return jnp.stack([blocks[g, :, g, :] for g in range(n_grp)])


def _aug_placement(n_heads):
    n_out = (n_heads // HEADS_PER_PAIR) * V7X_LANES
    pq = np.zeros((3 * V7X_LANES, n_out), np.float32)
    pk = np.zeros((3 * V7X_LANES, n_out), np.float32)
    cq = np.zeros((1, n_out), np.float32)
    ck = np.zeros((1, n_out), np.float32)
    for h in range(n_heads):
        pr, e = divmod(h, HEADS_PER_PAIR)
        base = pr * V7X_LANES + e * AUG_PER_HEAD
        for j in range(3):
            pq[j * V7X_LANES + h, base + j] = 1.0
            cq[0, base + 3 + j] = 1.0
            ck[0, base + j] = 1.0
            pk[j * V7X_LANES + h, base + 3 + j] = -1.0
    return {"pq": jnp.asarray(pq, BF16), "cq": jnp.asarray(cq), "pk": jnp.asarray(pk, BF16), "ck": jnp.asarray(ck)}


def _mixer_weights(w_in, b_fgate, conv_w, conv_b, w_rgate, b_rgate, w_igate, b_igate, lru_lambda, g_qk, g_mix_out):
    w = conv_w.shape[1]
    n_heads = b_fgate.shape[0]
    hw = n_heads * HEAD_DIM
    o_sb = 2 * w
    o_fx = o_sb + 3 * hw
    o_f = o_fx + 3 * hw
    wf = jnp.zeros((w_in.shape[0], V7X_LANES), F32).at[:, :3 * n_heads].set(jnp.tile(w_in[:, o_f:o_f + n_heads], (1, 3)))
    bf = jnp.zeros((1, V7X_LANES), F32).at[0, :3 * n_heads].set(jnp.tile(b_fgate, 3))
    head_of = jnp.arange(hw) // HEAD_DIM
    hsum = (head_of[:, None] == head_of[None, :]).astype(BF16)
    return {
        "wlru": w_in[:, :o_sb].astype(BF16),
        "wsb": w_in[:, o_sb:o_fx].astype(BF16),
        "wfox": w_in[:, o_fx:o_f].astype(BF16),
        "wf": wf.astype(BF16),
        "bf": bf,
        "cw": conv_w,
        "cb": conv_b.reshape(1, w),
        "wgate": jnp.stack([_diag_groups(_block_diag(w_rgate)), _diag_groups(_block_diag(w_igate))]).astype(BF16),
        "bgate": jnp.concatenate([b_rgate, b_igate]).reshape(1, 2 * w),
        "lam": lru_lambda.reshape(1, w),
        "gq": jnp.tile(g_qk[0], n_heads).reshape(1, hw),
        "gk": jnp.tile(g_qk[1], n_heads).reshape(1, hw),
        "gmix": g_mix_out[:w].reshape(1, w),
        "hsum": hsum,
        **_aug_placement(n_heads),
    }


def kernel(x, c, w_ada, b_ada, g_norm, w_ffn_up, w_ffn_down, w_in, b_fgate, conv_w, conv_b, w_rgate, b_rgate,
           w_igate, b_igate, lru_lambda, g_qk, g_mix_out, w_out):
    b, s, d = x.shape
    depth = w_ada.shape[0]
    w = conv_w.shape[-1]
    hw = b_fgate.shape[-1] * HEAD_DIM

    mod = _adaln_mod(c, w_ada, b_ada).reshape(depth, b, N_SUB, 3, 1, d)

    for l in range(depth):
        shift = lambda j: mod[l, :, j, 0]
        scale = lambda j: mod[l, :, j, 1]
        gate = lambda j: mod[l, :, j, 2]
        gn = lambda j: g_norm[l, j].reshape(1, d)

        def ffn(xx, j, i, mixer=None):
            return _ffn(xx.reshape(b * s, d), shift(j), scale(j), gate(j), gn(j), w_ffn_up[l, i].astype(BF16),
                        w_ffn_down[l, i].astype(BF16), s, mixer).reshape(b, s, d)

        x = ffn(x, 0, 0)

        wts = _mixer_weights(w_in[l], b_fgate[l], conv_w[l], conv_b[l], w_rgate[l], b_rgate[l], w_igate[l],
                             b_igate[l], lru_lambda[l], g_qk[l], g_mix_out[l])
        ylru, sbq, sbk, sbv, fxq, fxk, fxv, fxqa, fxka = _mixer_in(x, shift(1), scale(1), gn(1), wts)
        ysb = _sb_attention(sbq, sbk, sbv)
        yfox = _fox_attention(fxq, fxqa, fxk, fxka, fxv)
        rows = lambda a: a.reshape(b * s, a.shape[-1])
        x = ffn(x, 2, 1, mixer=(gate(1), rows(ylru), rows(ysb), rows(yfox), g_mix_out[l, w:w + hw].reshape(1, hw),
                                g_mix_out[l, w + hw:].reshape(1, hw), w_out[l].astype(BF16)))
    return x
```

```python
import functools

import jax
import jax.numpy as jnp
import numpy as np
from jax import lax
from jax.experimental import pallas as pl
from jax.experimental.pallas import tpu as pltpu

F32 = jnp.float32
BF16 = jnp.bfloat16

HEAD_DIM = 64
CONV_WIDTH = 4
LRU_C = 8.0
N_SUB = 3
EPS = 1e-6

V7X_LANES = 128
V7X_SUBLANES = 8
V7X_MXU_DIM = 256
V7X_VMEM_BYTES = 64 * 1024 * 1024
VMEM_LIMIT_BYTES = V7X_VMEM_BYTES - 8 * 1024 * 1024

ADALN_COLS = 1536
FFN_ROWS = 1024
FFN_FF_CHUNK = V7X_MXU_DIM
MIXER_ROWS = 512
SB_TILE = V7X_MXU_DIM
FOX_Q_TILE = 512
FOX_K_TILE = 512

HEADS_PER_PAIR = V7X_LANES // HEAD_DIM
AUG_PER_HEAD = 6
NEG_BIG = -1e30


def _params(*sem):
    return pltpu.CompilerParams(dimension_semantics=sem, vmem_limit_bytes=VMEM_LIMIT_BYTES)


def _sigmoid(x):
    return 1.0 / (1.0 + jnp.exp(-x))


def _log_sigmoid(x):
    return jnp.minimum(x, 0.0) - jnp.log(1.0 + jnp.exp(-jnp.abs(x)))


def _split3(v):
    hi = v.astype(BF16).astype(F32)
    r = v - hi
    mid = r.astype(BF16).astype(F32)
    lo = (r - mid).astype(BF16).astype(F32)
    return hi, mid, lo


def _mod_norm(x, g, scale, shift):
    ms = jnp.mean(x * x, axis=-1, keepdims=True)
    return (x * lax.rsqrt(ms + EPS)) * (g * (1.0 + scale)) + shift


def _mod_kernel(c_ref, w_ref, b_ref, o_ref):
    c = c_ref[...]
    ca = c * _sigmoid(c)
    o_ref[0] = jnp.dot(ca.astype(BF16), w_ref[0].astype(BF16), preferred_element_type=F32) + b_ref[0]


def _adaln_mod(c, w_ada, b_ada):
    depth, d, n = w_ada.shape
    b = c.shape[0]
    tn = ADALN_COLS
    assert n % tn == 0
    return pl.pallas_call(
        _mod_kernel,
        grid=(depth, n // tn),
        in_specs=[
            pl.BlockSpec((b, d), lambda l, j: (0, 0)),
            pl.BlockSpec((1, d, tn), lambda l, j: (l, 0, j)),
            pl.BlockSpec((1, 1, tn), lambda l, j: (l, 0, j)),
        ],
        out_specs=pl.BlockSpec((1, b, tn), lambda l, j: (l, 0, j)),
        out_shape=jax.ShapeDtypeStruct((depth, b, n), F32),
        compiler_params=_params("arbitrary", "arbitrary"),
        name="adaln_mod",
    )(c, w_ada, b_ada.reshape(depth, 1, n))


def _group_norm(y, g):
    return (y * lax.rsqrt(jnp.mean(y * y, axis=-1, keepdims=True) + EPS) * g).astype(BF16)


def _ffn_kernel(*refs, after_mixer):
    if after_mixer:
        x_ref, mgate_ref, ylru_ref, ysb_ref, yfox_ref, gsb_ref, gfox_ref, wout_ref, *refs = refs
    else:
        x_ref, *refs = refs
    shift_ref, scale_ref, gate_ref, g_ref, wup_ref, wd_ref, o_ref, h_scr, a_scr, acc_scr = refs
    ff = wd_ref.shape[0]
    tf = a_scr.shape[1]
    n_chunks = ff // tf
    if after_mixer:
        y = jnp.concatenate([ylru_ref[...], _group_norm(ysb_ref[...], gsb_ref[...]),
                             _group_norm(yfox_ref[...], gfox_ref[...])], axis=1)
        o_ref[...] = x_ref[...] + (1.0 + mgate_ref[0]) * jnp.dot(y, wout_ref[...], preferred_element_type=F32)
        x_ref = o_ref
    h_scr[...] = _mod_norm(x_ref[...], g_ref[...], scale_ref[0], shift_ref[0]).astype(BF16)

    def up(c):
        hb = h_scr[...]
        off = pl.multiple_of(c * tf, tf)
        g = jnp.dot(hb, wup_ref[:, pl.ds(off, tf)], preferred_element_type=F32)
        u = jnp.dot(hb, wup_ref[:, pl.ds(off + ff, tf)], preferred_element_type=F32)
        return ((g * _sigmoid(g)) * u).astype(BF16)

    def down(c):
        return jnp.dot(a_scr[...], wd_ref[pl.ds(pl.multiple_of(c * tf, tf), tf), :], preferred_element_type=F32)

    a_scr[...] = up(0)
    d = down(0)
    a_next = up(1)
    acc_scr[...] = d
    a_scr[...] = a_next

    def chunk(c, carry):
        d = down(c)
        a_next = up(c + 1)
        acc_scr[...] += d
        a_scr[...] = a_next
        return carry

    lax.fori_loop(1, n_chunks - 1, chunk, 0, unroll=3)
    o_ref[...] = x_ref[...] + (0.5 * (1.0 + gate_ref[0])) * (acc_scr[...] + down(n_chunks - 1))


def _ffn(x2, shift, scale, gate, g, wup, wd, seq, mixer=None):
    m, d = x2.shape
    tm, tf = FFN_ROWS, FFN_FF_CHUNK
    assert m % tm == 0 and seq % tm == 0 and wd.shape[0] % tf == 0 and wup.shape[1] == 2 * wd.shape[0]
    per_seq = seq // tm
    vec = pl.BlockSpec((1, 1, d), lambda i: (i // per_seq, 0, 0))
    full = lambda a: pl.BlockSpec(a.shape, lambda i: (0,) * a.ndim)
    rows = lambda a: pl.BlockSpec((tm, a.shape[1]), lambda i: (i, 0))
    mixer_args, mixer_specs = (), []
    if mixer is not None:
        mgate, ylru, ysb, yfox, gsb, gfox, wout = mixer
        mixer_args = (mgate, ylru, ysb, yfox, gsb, gfox, wout)
        mixer_specs = [vec, rows(ylru), rows(ysb), rows(yfox), full(gsb), full(gfox), full(wout)]
    return pl.pallas_call(
        functools.partial(_ffn_kernel, after_mixer=mixer is not None),
        grid=(m // tm,),
        in_specs=[rows(x2)] + mixer_specs + [vec, vec, vec, full(g), full(wup), full(wd)],
        out_specs=pl.BlockSpec((tm, d), lambda i: (i, 0)),
        out_shape=jax.ShapeDtypeStruct((m, d), F32),
        scratch_shapes=[
            pltpu.VMEM((tm, d), BF16),
            pltpu.VMEM((tm, tf), BF16),
            pltpu.VMEM((tm, d), F32),
        ],
        compiler_params=_params("arbitrary"),
        name="ffn",
    )(x2, *mixer_args, shift, scale, gate, g, wup, wd)


def _mixin_kernel(x_ref, shift_ref, scale_ref, g_ref, wlru_ref, wsb_ref, wfox_ref, wf_ref, bf_ref,
                  cw_ref, cb_ref, wgate_ref, bgate_ref, lam_ref, gq_ref, gk_ref, gmix_ref, hsum_ref,
                  pq_ref, cq_ref, pk_ref, ck_ref,
                  ylru_ref, sbq_ref, sbk_ref, sbv_ref, fxq_ref, fxk_ref, fxv_ref, fxqa_ref, fxka_ref,
                  xp_scr, a_scr, b_scr, hs_scr, hc_scr, fc_scr):
    ts = x_ref.shape[1]
    w = cw_ref.shape[1]
    n_heads = fxq_ref.shape[2] // HEAD_DIM
    pad = V7X_SUBLANES

    @pl.when(pl.program_id(1) == 0)
    def _():
        xp_scr[0:pad, :] = jnp.zeros((pad, w), F32)
        hc_scr[...] = jnp.zeros_like(hc_scr)
        fc_scr[...] = jnp.zeros_like(fc_scr)

    h = _mod_norm(x_ref[0], g_ref[...], scale_ref[0], shift_ref[0]).astype(BF16)

    p = jnp.dot(h, wlru_ref[...], preferred_element_type=F32)
    lru_g = p[:, w:]
    xp_scr[pad:pad + ts, :] = p[:, :w]
    x_all = xp_scr[...]
    u = cb_ref[...] + cw_ref[CONV_WIDTH - 1:CONV_WIDTH, :] * x_all[pad:, :]
    for back in range(1, CONV_WIDTH):
        tap = CONV_WIDTH - 1 - back
        u = u + cw_ref[tap:tap + 1, :] * pltpu.roll(x_all, back, 0)[pad:, :]
    xp_scr[0:pad, :] = x_all[ts:ts + pad, :]

    ub = u.astype(BF16)

    def gate(which):
        n_grp, grp = wgate_ref.shape[1], wgate_ref.shape[2]
        pre = [jnp.dot(ub[:, g * grp:(g + 1) * grp], wgate_ref[which, g], preferred_element_type=F32)
               for g in range(n_grp)]
        return _sigmoid(jnp.concatenate(pre, axis=1) + bgate_ref[:, which * w:(which + 1) * w])

    r = gate(0)
    ig = gate(1)
    lam = lam_ref[...]
    softplus_neg_lam = jnp.maximum(-lam, 0.0) + jnp.log1p(jnp.exp(-jnp.abs(lam)))
    log_a = (-LRU_C * softplus_neg_lam) * r
    a = jnp.exp(log_a)
    a_scr[...] = a
    s = (1.0 + a * a) * jnp.tanh(-log_a)
    b_scr[...] = jnp.where(s > 0.0, s * lax.rsqrt(s), 0.0) * (ig * u)

    row8 = lax.broadcasted_iota(jnp.int32, (V7X_SUBLANES, w), 0)

    def scan8(i, hprev):
        r0 = pl.multiple_of(i * V7X_SUBLANES, V7X_SUBLANES)
        aa = a_scr[pl.ds(r0, V7X_SUBLANES), :]
        bb = b_scr[pl.ds(r0, V7X_SUBLANES), :]
        for sh in (1, 2, 4):
            a_s = pltpu.roll(aa, sh, 0)
            b_s = pltpu.roll(bb, sh, 0)
            m = row8 >= sh
            bb = jnp.where(m, aa * b_s + bb, bb)
            aa = jnp.where(m, aa * a_s, aa)
        hh = aa * hprev + bb
        hs_scr[pl.ds(r0, V7X_SUBLANES), :] = hh
        return hh[V7X_SUBLANES - 1:V7X_SUBLANES, :]

    hc_scr[0:1, :] = lax.fori_loop(0, ts // V7X_SUBLANES, scan8, hc_scr[0:1, :], unroll=True)

    y = hs_scr[...] * jax.nn.gelu(lru_g)
    yn = y * lax.rsqrt(jnp.mean(y * y, axis=-1, keepdims=True) + EPS) * gmix_ref[...]
    ylru_ref[0] = yn.astype(BF16)

    scale = HEAD_DIM ** -0.5
    psb = jnp.dot(h, wsb_ref[...], preferred_element_type=F32)
    hw = n_heads * HEAD_DIM
    sbq_ref[0] = (psb[:, :hw] * scale).astype(BF16)
    sbk_ref[0] = psb[:, hw:2 * hw].astype(BF16)
    sbv_ref[0] = psb[:, 2 * hw:].astype(BF16)

    pfx = jnp.dot(h, wfox_ref[...], preferred_element_type=F32)

    def head_rms(t):
        ssq = jnp.dot((t * t).astype(BF16), hsum_ref[...], preferred_element_type=F32)
        return t * lax.rsqrt(ssq * (1.0 / HEAD_DIM) + EPS)

    fxq_ref[0] = (head_rms(pfx[:, :hw]) * (gq_ref[...] * scale)).astype(BF16)
    fxk_ref[0] = (head_rms(pfx[:, hw:2 * hw]) * gk_ref[...]).astype(BF16)
    fxv_ref[0] = pfx[:, 2 * hw:].astype(BF16)

    lane = lax.broadcasted_iota(jnp.int32, (ts, V7X_LANES), 1)
    logf = _log_sigmoid(jnp.dot(h, wf_ref[...], preferred_element_type=F32) + bf_ref[...])
    logf = jnp.where(lane < 3 * n_heads, logf, 0.0)
    hi, mid, lo = _split3(logf)
    pieces = jnp.where(lane < n_heads, hi, jnp.where(lane < 2 * n_heads, mid, lo)).astype(BF16)
    tri = (lax.broadcasted_iota(jnp.int32, (ts, ts), 0) >= lax.broadcasted_iota(jnp.int32, (ts, ts), 1))
    cs = jnp.dot(tri.astype(BF16), pieces, preferred_element_type=F32)
    cs = cs + pltpu.roll(cs, V7X_LANES - n_heads, 1) + pltpu.roll(cs, V7X_LANES - 2 * n_heads, 1)
    fcum = cs + fc_scr[0:1, :]
    fc_scr[0:1, :] = fcum[ts - 1:ts, :]

    f_pieces = jnp.concatenate(_split3(fcum), axis=1).astype(BF16)
    fxqa_ref[0] = (jnp.dot(f_pieces, pq_ref[...], preferred_element_type=F32) + cq_ref[...]).astype(BF16)
    fxka_ref[0] = (jnp.dot(f_pieces, pk_ref[...], preferred_element_type=F32) + ck_ref[...]).astype(BF16)


def _mixer_in(x, shift, scale, g, wts):
    b, s, d = x.shape
    ts = MIXER_ROWS
    assert s % ts == 0
    w = wts["cw"].shape[1]
    hw = wts["wsb"].shape[1] // 3
    n_pairs = hw // V7X_LANES
    vec = pl.BlockSpec((1, 1, d), lambda i, j: (i, 0, 0))
    full = lambda a: pl.BlockSpec(a.shape, lambda i, j: (0,) * a.ndim)
    names = ("wlru", "wsb", "wfox", "wf", "bf", "cw", "cb", "wgate", "bgate", "lam", "gq", "gk", "gmix", "hsum",
             "pq", "cq", "pk", "ck")
    consts = [wts[k] for k in names]
    seq_out = lambda width: pl.BlockSpec((1, ts, width), lambda i, j: (i, j, 0))
    widths = (w, hw, hw, hw, hw, hw, hw, n_pairs * V7X_LANES, n_pairs * V7X_LANES)
    return pl.pallas_call(
        _mixin_kernel,
        grid=(b, s // ts),
        in_specs=[pl.BlockSpec((1, ts, d), lambda i, j: (i, j, 0)), vec, vec, full(g)] + [full(a) for a in consts],
        out_specs=[seq_out(n) for n in widths],
        out_shape=[jax.ShapeDtypeStruct((b, s, n), BF16) for n in widths],
        scratch_shapes=[
            pltpu.VMEM((ts + V7X_SUBLANES, w), F32),
            pltpu.VMEM((ts, w), F32),
            pltpu.VMEM((ts, w), F32),
            pltpu.VMEM((ts, w), F32),
            pltpu.VMEM((V7X_SUBLANES, w), F32),
            pltpu.VMEM((V7X_SUBLANES, V7X_LANES), F32),
        ],
        compiler_params=_params("arbitrary", "arbitrary"),
        name="mixer_in",
    )(x, shift, scale, g, *consts)


def _head_lanes(h):
    pr, e = divmod(h, HEADS_PER_PAIR)
    return slice(pr * V7X_LANES, (pr + 1) * V7X_LANES), e * HEAD_DIM


def _neg_abs(x):
    bits = lax.bitcast_convert_type(x, jnp.uint32) | jnp.uint32(0x80000000)
    return lax.bitcast_convert_type(bits, F32)


def _sb_kernel(q_ref, k_ref, v_ref, o_ref, qm_scr, z_scr, w_scr, carry_scr, acc_scr):
    tq = q_ref.shape[1]
    tk = tq
    n_heads = q_ref.shape[2] // HEAD_DIM
    heads = range(n_heads)
    i = pl.program_id(1)
    lane = lax.broadcasted_iota(jnp.int32, (tq, V7X_LANES), 1)
    past = lax.broadcasted_iota(jnp.int32, (tq, tk), 1) < lax.broadcasted_iota(jnp.int32, (tq, tk), 0)
    strict = (lax.broadcasted_iota(jnp.int32, (tk, tk), 0) > lax.broadcasted_iota(jnp.int32, (tk, tk), 1)).astype(BF16)

    for h in heads:
        cols, lo_lane = _head_lanes(h)
        q_pair = q_ref[0, :, cols]
        qm_scr[h] = jnp.where((lane >= lo_lane) & (lane < lo_lane + HEAD_DIM), q_pair, jnp.zeros_like(q_pair))
    carry_scr[...] = jnp.zeros_like(carry_scr)
    acc_scr[...] = jnp.zeros_like(acc_scr)

    def score(h, j):
        k0 = pl.multiple_of(j * tk, tk)
        return lax.dot_general(qm_scr[h], k_ref[0, pl.ds(k0, tk), _head_lanes(h)[0]], (((1,), (1,)), ((), ())),
                               preferred_element_type=F32)

    def weighted_values(h, j):
        k0 = pl.multiple_of(j * tk, tk)
        v = v_ref[0, pl.ds(k0, tk), _head_lanes(h)[0]]
        acc_scr[h] += jnp.dot(w_scr[h], v, preferred_element_type=F32)

    def consume(masked, ahead, behind):
        log_beta, log_1mb, after = [], [], []
        for h in heads:
            z = z_scr[h]
            sp = jnp.log(1.0 + jnp.exp(_neg_abs(z))).astype(BF16)
            zb = z.astype(BF16)
            l1 = jnp.minimum(-zb, 0.0) - sp
            log_beta.append(jnp.minimum(zb, 0.0) - sp)
            log_1mb.append(jnp.where(past, l1, jnp.zeros_like(l1)) if masked else l1)
            carry = jnp.concatenate([carry_scr[h]] * (tk // V7X_LANES), axis=1)
            after.append(carry + jnp.dot(log_1mb[h], strict, preferred_element_type=F32))
            if behind is not None:
                weighted_values(h, behind)
            if ahead is not None:
                z_scr[h] = score(h, ahead)
        for h in heads:
            w_h = jnp.exp(log_beta[h] + after[h].astype(BF16))
            w_scr[h] = jnp.where(past, w_h, jnp.zeros_like(w_h)) if masked else w_h
            carry_scr[h] = jnp.broadcast_to(after[h][:, 0:1] + log_1mb[h][:, 0:1].astype(F32), (tq, V7X_LANES))

    for h in heads:
        z_scr[h] = score(h, i)
    consume(True, jnp.maximum(i - 1, 0), None)

    def middle(j):
        consume(False, j - 1, j + 1)

    n_middle = jnp.maximum(i - 1, 0)

    def four_tiles(quad, c):
        j = i - 1 - 4 * quad
        for t in range(4):
            middle(j - t)
        return c

    lax.fori_loop(0, n_middle // 4, four_tiles, 0)

    @pl.when(n_middle % 4 >= 2)
    def _():
        j = n_middle % 4
        middle(j)
        middle(j - 1)

    @pl.when(n_middle % 2 == 1)
    def _():
        middle(1)

    @pl.when(i > 0)
    def _():
        consume(False, None, 1)

    for h in heads:
        weighted_values(h, 0)

    for h in range(0, n_heads, HEADS_PER_PAIR):
        cols, _ = _head_lanes(h)
        o_ref[0, :, cols] = jnp.where(lane < HEAD_DIM, acc_scr[h], acc_scr[h + 1])


def _sb_attention(q, k, v):
    b, s, hw = q.shape
    tq = SB_TILE
    n_heads = hw // HEAD_DIM
    assert s % tq == 0
    return pl.pallas_call(
        _sb_kernel,
        grid=(b, s // tq),
        in_specs=[
            pl.BlockSpec((1, tq, hw), lambda i, j: (i, j, 0)),
            pl.BlockSpec((1, s, hw), lambda i, j: (i, 0, 0)),
            pl.BlockSpec((1, s, hw), lambda i, j: (i, 0, 0)),
        ],
        out_specs=pl.BlockSpec((1, tq, hw), lambda i, j: (i, j, 0)),
        out_shape=jax.ShapeDtypeStruct((b, s, hw), F32),
        scratch_shapes=[
            pltpu.VMEM((n_heads, tq, V7X_LANES), BF16),
            pltpu.VMEM((n_heads, tq, tq), F32),
            pltpu.VMEM((n_heads, tq, tq), BF16),
            pltpu.VMEM((n_heads, tq, V7X_LANES), F32),
            pltpu.VMEM((n_heads, tq, V7X_LANES), F32),
        ],
        compiler_params=_params("arbitrary", "arbitrary"),
        name="sb_attention",
    )(q, k, v)


def _fox_kernel(q_ref, qa_ref, k_ref, ka_ref, v_ref, o_ref, qm_scr, z_scr, m_scr, l_scr, acc_scr, *, tk):
    tq = q_ref.shape[1]
    n_heads = q_ref.shape[2] // HEAD_DIM
    heads = range(n_heads)
    i = pl.program_id(1)
    last = (i * tq) // tk
    lane = lax.broadcasted_iota(jnp.int32, (tq, V7X_LANES), 1)
    col_minus_row = lax.broadcasted_iota(jnp.int32, (tq, tk), 1) - lax.broadcasted_iota(jnp.int32, (tq, tk), 0)
    visible = col_minus_row <= i * tq - last * tk
    ones = jnp.ones((tk, V7X_LANES), BF16)

    for h in heads:
        cols, lo_lane = _head_lanes(h)
        e = h % HEADS_PER_PAIR
        q_pair = q_ref[0, :, cols]
        qa_pair = qa_ref[0, :, cols]
        in_head = (lane >= lo_lane) & (lane < lo_lane + HEAD_DIM)
        in_aug = (lane >= e * AUG_PER_HEAD) & (lane < (e + 1) * AUG_PER_HEAD)
        qm_scr[h] = jnp.concatenate([jnp.where(in_head, q_pair, jnp.zeros_like(q_pair)),
                                     jnp.where(in_aug, qa_pair, jnp.zeros_like(qa_pair))], axis=1)
    m_scr[...] = jnp.full_like(m_scr, NEG_BIG)
    l_scr[...] = jnp.zeros_like(l_scr)
    acc_scr[...] = jnp.zeros_like(acc_scr)

    def score(h, j):
        k0 = pl.multiple_of(j * tk, tk)
        cols, _ = _head_lanes(h)
        k = jnp.concatenate([k_ref[0, pl.ds(k0, tk), cols], ka_ref[0, pl.ds(k0, tk), cols]], axis=1)
        return lax.dot_general(qm_scr[h], k, (((1,), (1,)), ((), ())), preferred_element_type=F32)

    def consume(j, masked, ahead):
        k0 = pl.multiple_of(j * tk, tk)
        for h in heads:
            z = jnp.where(visible, z_scr[h], -jnp.inf) if masked else z_scr[h]
            m_prev = m_scr[h]
            m_new = jnp.maximum(m_prev, jnp.max(z, axis=-1, keepdims=True))
            alpha = jnp.exp(m_prev - m_new)
            p = jnp.exp(z - jnp.concatenate([m_new] * (tk // V7X_LANES), axis=1)).astype(BF16)
            m_scr[h] = m_new
            v = jnp.concatenate([v_ref[0, pl.ds(k0, tk), _head_lanes(h)[0]], ones], axis=1)
            pv = jnp.dot(p, v, preferred_element_type=F32)
            if ahead is not None:
                z_scr[h] = score(h, ahead)
            l_scr[h] = alpha * l_scr[h] + pv[:, V7X_LANES:]
            acc_scr[h] = alpha * acc_scr[h] + pv[:, :V7X_LANES]

    for h in heads:
        z_scr[h] = score(h, 0)

    def step(t, c):
        consume(t, False, t + 1)
        return c

    lax.fori_loop(0, last, step, 0)
    consume(last, True, None)
    for h in range(0, n_heads, HEADS_PER_PAIR):
        cols, _ = _head_lanes(h)
        o_ref[0, :, cols] = jnp.where(lane < HEAD_DIM, acc_scr[h] / l_scr[h], acc_scr[h + 1] / l_scr[h + 1])


def _fox_attention(q, qa, k, ka, v):
    b, s, hw = q.shape
    tq, tk = FOX_Q_TILE, FOX_K_TILE
    n_heads = hw // HEAD_DIM
    assert s % tq == 0 and s % tk == 0 and tk % tq == 0
    qspec = pl.BlockSpec((1, tq, hw), lambda i, j: (i, j, 0))
    kspec = pl.BlockSpec((1, s, hw), lambda i, j: (i, 0, 0))
    return pl.pallas_call(
        functools.partial(_fox_kernel, tk=tk),
        grid=(b, s // tq),
        in_specs=[qspec, qspec, kspec, kspec, kspec],
        out_specs=pl.BlockSpec((1, tq, hw), lambda i, j: (i, j, 0)),
        out_shape=jax.ShapeDtypeStruct((b, s, hw), F32),
        scratch_shapes=[
            pltpu.VMEM((n_heads, tq, 2 * V7X_LANES), BF16),
            pltpu.VMEM((n_heads, tq, tk), F32),
            pltpu.VMEM((n_heads, tq, V7X_LANES), F32),
            pltpu.VMEM((n_heads, tq, V7X_LANES), F32),
            pltpu.VMEM((n_heads, tq, V7X_LANES), F32),
        ],
        compiler_params=_params("arbitrary", "arbitrary"),
        name="fox_attention",
    )(q, qa, k, ka, v)


def _block_diag(w):
    h, di, dj = w.shape
    return jnp.einsum("hij,hg->higj", w, jnp.eye(h, dtype=w.dtype)).reshape(h * di, h * dj)


def _diag_groups(bd):
    n = bd.shape[0]
    assert n % V7X_MXU_DIM == 0 and V7X_MXU_DIM % HEAD_DIM == 0
    n_grp = n // V7X_MXU_DIM
    blocks = bd.reshape(n_grp, V7X_MXU_DIM, n_grp, V7X_MXU_DIM)
    return jnp.stack([blocks[g, :, g, :] for g in range(n_grp)])


def _aug_placement(n_heads):
    n_out = (n_heads // HEADS_PER_PAIR) * V7X_LANES
    pq = np.zeros((3 * V7X_LANES, n_out), np.float32)
    pk = np.zeros((3 * V7X_LANES, n_out), np.float32)
    cq = np.zeros((1, n_out), np.float32)
    ck = np.zeros((1, n_out), np.float32)
    for h in range(n_heads):
        pr, e = divmod(h, HEADS_PER_PAIR)
        base = pr * V7X_LANES + e * AUG_PER_HEAD
        for j in range(3):
            pq[j * V7X_LANES + h, base + j] = 1.0
            cq[0, base + 3 + j] = 1.0
            ck[0, base + j] = 1.0
            pk[j * V7X_LANES + h, base + 3 + j] = -1.0
    return {"pq": jnp.asarray(pq, BF16), "cq": jnp.asarray(cq), "pk": jnp.asarray(pk, BF16), "ck": jnp.asarray(ck)}


def _mixer_weights(w_in, b_fgate, conv_w, conv_b, w_rgate, b_rgate, w_igate, b_igate, lru_lambda, g_qk, g_mix_out):
    w = conv_w.shape[1]
    n_heads = b_fgate.shape[0]
    hw = n_heads * HEAD_DIM
    o_sb = 2 * w
    o_fx = o_sb + 3 * hw
    o_f = o_fx + 3 * hw
    wf = jnp.zeros((w_in.shape[0], V7X_LANES), F32).at[:, :3 * n_heads].set(jnp.tile(w_in[:, o_f:o_f + n_heads], (1, 3)))
    bf = jnp.zeros((1, V7X_LANES), F32).at[0, :3 * n_heads].set(jnp.tile(b_fgate, 3))
    head_of = jnp.arange(hw) // HEAD_DIM
    hsum = (head_of[:, None] == head_of[None, :]).astype(BF16)
    return {
        "wlru": w_in[:, :o_sb].astype(BF16),
        "wsb": w_in[:, o_sb:o_fx].astype(BF16),
        "wfox": w_in[:, o_fx:o_f].astype(BF16),
        "wf": wf.astype(BF16),
        "bf": bf,
        "cw": conv_w,
        "cb": conv_b.reshape(1, w),
        "wgate": jnp.stack([_diag_groups(_block_diag(w_rgate)), _diag_groups(_block_diag(w_igate))]).astype(BF16),
        "bgate": jnp.concatenate([b_rgate, b_igate]).reshape(1, 2 * w),
        "lam": lru_lambda.reshape(1, w),
        "gq": jnp.tile(g_qk[0], n_heads).reshape(1, hw),
        "gk": jnp.tile(g_qk[1], n_heads).reshape(1, hw),
        "gmix": g_mix_out[:w].reshape(1, w),
        "hsum": hsum,
        **_aug_placement(n_heads),
    }


def kernel(x, c, w_ada, b_ada, g_norm, w_ffn_up, w_ffn_down, w_in, b_fgate, conv_w, conv_b, w_rgate, b_rgate,
           w_igate, b_igate, lru_lambda, g_qk, g_mix_out, w_out):
    b, s, d = x.shape
    depth = w_ada.shape[0]
    w = conv_w.shape[-1]
    hw = b_fgate.shape[-1] * HEAD_DIM

    mod = _adaln_mod(c, w_ada, b_ada).reshape(depth, b, N_SUB, 3, 1, d)

    for l in range(depth):
        shift = lambda j: mod[l, :, j, 0]
        scale = lambda j: mod[l, :, j, 1]
        gate = lambda j: mod[l, :, j, 2]
        gn = lambda j: g_norm[l, j].reshape(1, d)

        def ffn(xx, j, i, mixer=None):
            return _ffn(xx.reshape(b * s, d), shift(j), scale(j), gate(j), gn(j), w_ffn_up[l, i].astype(BF16),
                        w_ffn_down[l, i].astype(BF16), s, mixer).reshape(b, s, d)

        x = ffn(x, 0, 0)

        wts = _mixer_weights(w_in[l], b_fgate[l], conv_w[l], conv_b[l], w_rgate[l], b_rgate[l], w_igate[l],
                             b_igate[l], lru_lambda[l], g_qk[l], g_mix_out[l])
        ylru, sbq, sbk, sbv, fxq, fxk, fxv, fxqa, fxka = _mixer_in(x, shift(1), scale(1), gn(1), wts)
        ysb = _sb_attention(sbq, sbk, sbv)
        yfox = _fox_attention(fxq, fxqa, fxk, fxka, fxv)
        rows = lambda a: a.reshape(b * s, a.shape[-1])
        x = ffn(x, 2, 1, mixer=(gate(1), rows(ylru), rows(ysb), rows(yfox), g_mix_out[l, w:w + hw].reshape(1, hw),
                                g_mix_out[l, w + hw:].reshape(1, hw), w_out[l].astype(BF16)))
    return x
```
